```python
import math
import jax, jax.numpy as jnp
from jax import lax
import numpy as np

D_MODEL = 1024
BATCH = 8
SEQ = 4096
DEPTH = 4

GRID_W = 64
CTX_LEN = 256
MLA_HEADS = 8
MLA_NOPE_DIM = 64
MLA_ROPE_DIM = 32
MLA_V_DIM = 64
MLA_Q_RANK = 256
MLA_KV_RANK = 128
MLA_QK_DIM = MLA_NOPE_DIM + MLA_ROPE_DIM
GDN_HEADS = 4
GDN_HEAD_DIM = 128
GDN_WIDTH = GDN_HEADS * GDN_HEAD_DIM
GDN_CONV = 5
GDN_CHUNK = 64
N_DIR = 2
D_MIX = MLA_HEADS * MLA_V_DIM + GDN_WIDTH
D_FF = 4 * D_MODEL
ROPE_THETA = 10000.0
NORM_EPS = 1e-6
Q_BLOCK = 128
IN_SIZES = (MLA_Q_RANK, MLA_KV_RANK, MLA_ROPE_DIM, 3 * GDN_WIDTH, GDN_WIDTH, N_DIR * GDN_HEADS, N_DIR * GDN_HEADS)
D_IN = MLA_Q_RANK + MLA_KV_RANK + MLA_ROPE_DIM + 4 * GDN_WIDTH + 2 * N_DIR * GDN_HEADS

kernel_name = "hybrid_mla_gdn_prefix_dit"


def _split_points(sizes):
    pts, acc = [], 0
    for s in sizes[:-1]:
        acc += s
        pts.append(acc)
    return pts


def rmsnorm(x, g):
    xf = x.astype(jnp.float32)
    y = xf * lax.rsqrt(jnp.mean(xf * xf, axis=-1, keepdims=True) + NORM_EPS)
    return (y * g.astype(jnp.float32)).astype(x.dtype)


def modulate(x, g, shift, scale):
    return rmsnorm(x, g) * (1 + scale) + shift


def l2norm(x):
    xf = x.astype(jnp.float32)
    return (xf * lax.rsqrt(jnp.sum(xf * xf, axis=-1, keepdims=True) + NORM_EPS)).astype(x.dtype)


def axial_rope_tables(n_tokens, dtype):
    rows = n_tokens // GRID_W
    row = jnp.broadcast_to(jnp.arange(rows)[:, None], (rows, GRID_W)).reshape(-1).astype(jnp.float32)
    col = jnp.broadcast_to(jnp.arange(GRID_W)[None, :], (rows, GRID_W)).reshape(-1).astype(jnp.float32)
    axis_pairs = MLA_ROPE_DIM // 4
    inv_freq = ROPE_THETA ** (-jnp.arange(axis_pairs, dtype=jnp.float32) / axis_pairs)
    ang = jnp.concatenate([row[:, None] * inv_freq, col[:, None] * inv_freq], axis=-1)
    return jnp.cos(ang).astype(dtype), jnp.sin(ang).astype(dtype)


def apply_rope(x, cos, sin):
    half = MLA_ROPE_DIM // 2
    c, s = cos[:, None, :], sin[:, None, :]
    x1, x2 = x[..., :half], x[..., half:]
    return jnp.concatenate([x1 * c - x2 * s, x2 * c + x1 * s], axis=-1)


def centred_conv(x, w):
    pad = GDN_CONV // 2
    return lax.conv_general_dilated(x, w[:, None, :].astype(x.dtype), window_strides=(1,), padding=[(pad, pad)],
                                    dimension_numbers=('NWC', 'WIO', 'NWC'), feature_group_count=x.shape[-1])


def project_tokens(h, w_in, q_a_g, w_q_b, kv_a_g, w_kv_b, conv_w, a_log, dt_bias, rope):
    B, T, _ = h.shape
    p = h @ w_in
    q_a, kv_a, k_r, qkv, z, a, b = jnp.split(p, _split_points(IN_SIZES), axis=-1)
    q = (rmsnorm(q_a, q_a_g) @ w_q_b).reshape(B, T, MLA_HEADS, MLA_QK_DIM)
    kv = (rmsnorm(kv_a, kv_a_g) @ w_kv_b).reshape(B, T, MLA_HEADS, MLA_NOPE_DIM + MLA_V_DIM)
    q_nope, q_pe = q[..., :MLA_NOPE_DIM], q[..., MLA_NOPE_DIM:]
    k_nope, v = kv[..., :MLA_NOPE_DIM], kv[..., MLA_NOPE_DIM:]
    k_pe = k_r[:, :, None, :]
    if rope is not None:
        cos, sin = rope
        q_pe = apply_rope(q_pe, cos, sin)
        k_pe = apply_rope(k_pe, cos, sin)
    q = jnp.concatenate([q_nope, q_pe], axis=-1)
    k = jnp.concatenate([k_nope, jnp.broadcast_to(k_pe, (B, T, MLA_HEADS, MLA_ROPE_DIM))], axis=-1)
    qkv = jax.nn.silu(centred_conv(qkv, conv_w)).reshape(B, T, 3, GDN_HEADS, GDN_HEAD_DIM)
    gq, gk, gv = l2norm(qkv[:, :, 0]), l2norm(qkv[:, :, 1]), qkv[:, :, 2]
    a = a.reshape(B, T, N_DIR, GDN_HEADS).astype(jnp.float32)
    b = b.reshape(B, T, N_DIR, GDN_HEADS).astype(jnp.float32)
    g = -jnp.exp(a_log.astype(jnp.float32)) * jax.nn.softplus(a + dt_bias.astype(jnp.float32))
    beta = jax.nn.sigmoid(b)
    z = z.reshape(B, T, GDN_HEADS, GDN_HEAD_DIM)
    return (q, k, v), (gq, gk, gv, g, beta, z)


def mla_attend(q, k, v):
    B, Tq, H, dqk = q.shape
    nblk = Tq // Q_BLOCK
    scale = dqk ** -0.5
    qb = q.reshape(B, nblk, Q_BLOCK, H, dqk).transpose(1, 0, 2, 3, 4)

    def block(qi):
        s = jnp.einsum('bqhd,bkhd->bhqk', qi, k).astype(jnp.float32) * scale
        p = jax.nn.softmax(s, axis=-1).astype(v.dtype)
        return jnp.einsum('bhqk,bkhd->bqhd', p, v)

    o = lax.map(block, qb)
    return o.transpose(1, 0, 2, 3, 4).reshape(B, Tq, H * v.shape[-1])


def chunk_gated_delta(q, k, v, g, beta, state0):
    f32 = jnp.float32
    B, T, H, DK = q.shape
    DV = v.shape[-1]
    N, C = T // GDN_CHUNK, GDN_CHUNK

    def to_chunks(x):
        return x.astype(f32).reshape(B, N, C, H, x.shape[-1]).transpose(0, 3, 1, 2, 4)

    q = to_chunks(q) * (DK ** -0.5)
    k, v = to_chunks(k), to_chunks(v)
    g = g.astype(f32).reshape(B, N, C, H).transpose(0, 3, 1, 2)
    beta = beta.astype(f32).reshape(B, N, C, H).transpose(0, 3, 1, 2)
    g = jnp.cumsum(g, axis=-1)
    tril = jnp.tril(jnp.ones((C, C), dtype=bool))
    strict = jnp.tril(jnp.ones((C, C), dtype=bool), k=-1)
    diff = g[..., :, None] - g[..., None, :]
    decay = jnp.where(tril, jnp.exp(jnp.where(tril, diff, 0.0)), 0.0)
    k_beta = k * beta[..., None]
    v_beta = v * beta[..., None]
    lower = jnp.where(strict, jnp.einsum('bhnid,bhnjd->bhnij', k_beta, k) * decay, 0.0)
    eye = jnp.eye(C, dtype=f32)
    rhs = jnp.concatenate([v_beta, k_beta * jnp.exp(g)[..., None]], axis=-1)
    sol = lax.linalg.triangular_solve(lower + eye, rhs, left_side=True, lower=True, unit_diagonal=True)
    u, w = sol[..., :DV], sol[..., DV:]
    qk = jnp.einsum('bhnid,bhnjd->bhnij', q, k) * decay
    q_dec = q * jnp.exp(g)[..., None]
    k_tail = k * jnp.exp(g[..., -1:] - g)[..., None]
    g_last = jnp.exp(g[..., -1])

    def step(S, inp):
        qk_n, u_n, w_n, qd_n, kt_n, gl_n = inp
        v_new = u_n - jnp.einsum('bhck,bhkv->bhcv', w_n, S)
        o = jnp.einsum('bhck,bhkv->bhcv', qd_n, S) + jnp.einsum('bhij,bhjv->bhiv', qk_n, v_new)
        S = S * gl_n[..., None, None] + jnp.einsum('bhck,bhcv->bhkv', kt_n, v_new)
        return S, o

    xs = tuple(jnp.moveaxis(t, 2, 0) for t in (qk, u, w, q_dec, k_tail, g_last))
    S_final, o = lax.scan(step, state0, xs)
    return o.transpose(1, 0, 3, 2, 4).reshape(B, T, H, DV), S_final


def gdn_direction(feats, d, state0, reverse):
    gq, gk, gv, g, beta, _ = feats
    g_d, b_d = g[:, :, d], beta[:, :, d]
    if reverse:
        gq, gk, gv, g_d, b_d = (jnp.flip(t, axis=1) for t in (gq, gk, gv, g_d, b_d))
    o, S = chunk_gated_delta(gq, gk, gv, g_d, b_d, state0)
    if reverse:
        o = jnp.flip(o, axis=1)
    return o, S


def gdn_gated_out(o, z, gdn_norm_g):
    B, T = o.shape[:2]
    y = rmsnorm(o, gdn_norm_g) * jax.nn.silu(z.astype(jnp.float32))
    return y.reshape(B, T, GDN_WIDTH).astype(z.dtype)


def gdn_mix(feat_l, feat_c, gdn_norm_g, need_ctx_out):
    B = feat_l[0].shape[0]
    o_l, o_c = 0.0, 0.0
    for d in range(N_DIR):
        zeros = jnp.zeros((B, GDN_HEADS, GDN_HEAD_DIM, GDN_HEAD_DIM), jnp.float32)
        oc, s_ctx = gdn_direction(feat_c, d, zeros, d == 1)
        ol, _ = gdn_direction(feat_l, d, s_ctx, d == 1)
        o_l = o_l + ol
        o_c = o_c + oc
    y_l = gdn_gated_out(o_l, feat_l[5], gdn_norm_g)
    y_c = gdn_gated_out(o_c, feat_c[5], gdn_norm_g) if need_ctx_out else None
    return y_l, y_c


def squared_relu_mlp(h, w_ff1, w_ff2):
    return jnp.square(jax.nn.relu(h @ w_ff1)) @ w_ff2


def layer(x, xc, mod, mod_c, rope, norm1_g, w_in, q_a_g, w_q_b, kv_a_g, w_kv_b, conv_w, a_log, dt_bias,
          gdn_norm_g, w_out, norm2_g, w_ff1, w_ff2, last):
    sh1, sc1, g1, sh2, sc2, g2 = jnp.split(mod, 6, axis=-1)
    csh1, csc1, cg1, csh2, csc2, cg2 = jnp.split(mod_c, 6, axis=-1)
    h = modulate(x, norm1_g, sh1, sc1)
    hc = modulate(xc, norm1_g, csh1, csc1)
    pw = (w_in, q_a_g, w_q_b, kv_a_g, w_kv_b, conv_w, a_log, dt_bias)
    (q, k, v), feat_l = project_tokens(h, *pw, rope)
    (qc, kc, vc), feat_c = project_tokens(hc, *pw, None)
    attn_l = mla_attend(q, jnp.concatenate([kc, k], axis=1), jnp.concatenate([vc, v], axis=1))
    gdn_l, gdn_c = gdn_mix(feat_l, feat_c, gdn_norm_g, not last)
    x = x + g1 * (jnp.concatenate([attn_l, gdn_l], axis=-1) @ w_out)
    x = x + g2 * squared_relu_mlp(modulate(x, norm2_g, sh2, sc2), w_ff1, w_ff2)
    if not last:
        attn_c = mla_attend(qc, kc, vc)
        xc = xc + cg1 * (jnp.concatenate([attn_c, gdn_c], axis=-1) @ w_out)
        xc = xc + cg2 * squared_relu_mlp(modulate(xc, norm2_g, csh2, csc2), w_ff1, w_ff2)
    return x, xc


def setup_inputs(seed: int = 0) -> dict:
    key = jax.random.key(seed)
    ks = jax.random.split(key, 24)
    f32 = jnp.float32
    L = DEPTH

    def nrm(k, shape, scale):
        return jax.random.normal(k, shape, f32) * scale

    def gain(k, shape):
        return 1.0 + 0.02 * jax.random.normal(k, shape, f32)

    dt = jnp.exp(jax.random.uniform(ks[14], (L, N_DIR, GDN_HEADS), f32, math.log(1e-3), math.log(1e-1)))
    return {
        "x": nrm(ks[0], (BATCH, SEQ, D_MODEL), 1.0),
        "c": nrm(ks[1], (BATCH, D_MODEL), 1.0),
        "ctx": nrm(ks[2], (BATCH, CTX_LEN, D_MODEL), 1.0),
        "c_ctx": nrm(ks[3], (D_MODEL,), 1.0),
        "w_ada": nrm(ks[4], (L, D_MODEL, 6 * D_MODEL), 0.5 * D_MODEL ** -0.5),
        "b_ada": nrm(ks[5], (L, 6 * D_MODEL), 0.01),
        "norm1_g": gain(ks[6], (L, D_MODEL)),
        "w_in": nrm(ks[7], (L, D_MODEL, D_IN), D_MODEL ** -0.5),
        "q_a_g": gain(ks[8], (L, MLA_Q_RANK)),
        "w_q_b": nrm(ks[9], (L, MLA_Q_RANK, MLA_HEADS * MLA_QK_DIM), MLA_Q_RANK ** -0.5),
        "kv_a_g": gain(ks[10], (L, MLA_KV_RANK)),
        "w_kv_b": nrm(ks[11], (L, MLA_KV_RANK, MLA_HEADS * (MLA_NOPE_DIM + MLA_V_DIM)), MLA_KV_RANK ** -0.5),
        "conv_w": nrm(ks[12], (L, GDN_CONV, 3 * GDN_WIDTH), GDN_CONV ** -0.5),
        "a_log": jnp.log(jax.random.uniform(ks[13], (L, N_DIR, GDN_HEADS), f32, 1.0, 16.0)),
        "dt_bias": dt + jnp.log(-jnp.expm1(-dt)),
        "gdn_norm_g": gain(ks[15], (L, GDN_HEAD_DIM)),
        "w_out": nrm(ks[16], (L, D_MIX, D_MODEL), D_MIX ** -0.5),
        "norm2_g": gain(ks[17], (L, D_MODEL)),
        "w_ff1": nrm(ks[18], (L, D_MODEL, D_FF), D_MODEL ** -0.5),
        "w_ff2": nrm(ks[19], (L, D_FF, D_MODEL), D_FF ** -0.5),
        "final_norm_g": gain(ks[20], (D_MODEL,)),
    }


def reference(x, c, ctx, c_ctx, w_ada, b_ada, norm1_g, w_in, q_a_g, w_q_b, kv_a_g, w_kv_b, conv_w, a_log,
              dt_bias, gdn_norm_g, w_out, norm2_g, w_ff1, w_ff2, final_norm_g):
    rope = axial_rope_tables(x.shape[1], x.dtype)
    sc = jax.nn.silu(c)
    scc = jax.nn.silu(c_ctx)
    xc = ctx
    for i in range(DEPTH):
        mod = (sc @ w_ada[i] + b_ada[i])[:, None, :]
        mod_c = (scc @ w_ada[i] + b_ada[i])[None, None, :]
        x, xc = layer(x, xc, mod, mod_c, rope, norm1_g[i], w_in[i], q_a_g[i], w_q_b[i], kv_a_g[i], w_kv_b[i],
                      conv_w[i], a_log[i], dt_bias[i], gdn_norm_g[i], w_out[i], norm2_g[i], w_ff1[i], w_ff2[i],
                      i == DEPTH - 1)
    return rmsnorm(x, final_norm_g)
```

```python
import functools

import jax
import jax.numpy as jnp
from jax import lax
from jax.experimental import pallas as pl
from jax.experimental.pallas import tpu as pltpu

F32 = jnp.float32
BF16 = jnp.bfloat16

GRID_W = 64
MLA_HEADS = 8
MLA_NOPE_DIM = 64
MLA_ROPE_DIM = 32
MLA_V_DIM = 64
MLA_Q_RANK = 256
MLA_KV_RANK = 128
MLA_QK_DIM = MLA_NOPE_DIM + MLA_ROPE_DIM
GDN_HEADS = 4
GDN_HEAD_DIM = 128
GDN_WIDTH = GDN_HEADS * GDN_HEAD_DIM
GDN_CONV = 5
GDN_CHUNK = 64
N_DIR = 2
ROPE_THETA = 10000.0
NORM_EPS = 1e-6

LANES = 128
BF16_SUBLANES = 16
VMEM_LIMIT_BYTES = 56 * 1024 * 1024

TOKEN_TILE = 256
GDN_BLOCK = 256
NEG_BIG = -1e30
PAD_ROWS = 16


def _sigmoid(x):
    return 1.0 / (1.0 + jnp.exp(-x))


def _silu(x):
    return x * _sigmoid(x)


def _dot(a, b):
    return jnp.dot(a, b, preferred_element_type=F32)


def _dot_nt(a, b):
    return lax.dot_general(a, b, (((1,), (1,)), ((), ())), preferred_element_type=F32)


def _dot_tn(a, b):
    return lax.dot_general(a, b, (((0,), (0,)), ((), ())), preferred_element_type=F32)


def _const_spec(shape):
    nd = len(shape)
    return pl.BlockSpec(shape, lambda *_: (0,) * nd, pipeline_mode=pl.Buffered(1))


def _params(*sem):
    return pltpu.CompilerParams(dimension_semantics=sem, vmem_limit_bytes=VMEM_LIMIT_BYTES)


def _ada_kernel(c_ref, w_ref, b_ref, o_ref):
    s = _silu(c_ref[...]).astype(BF16)
    o_ref[0] = _dot(s, w_ref[0].astype(BF16)) + b_ref[0]


def _ada_call(cc, w_ada, b_ada):
    n_layers, d, d6 = w_ada.shape
    tn = 1536
    return pl.pallas_call(
        _ada_kernel,
        grid=(n_layers, d6 // tn),
        in_specs=[
            pl.BlockSpec((PAD_ROWS, d), lambda l, n: (0, 0)),
            pl.BlockSpec((1, d, tn), lambda l, n: (l, 0, n)),
            pl.BlockSpec((1, 1, tn), lambda l, n: (l, 0, n)),
        ],
        out_specs=pl.BlockSpec((1, PAD_ROWS, tn), lambda l, n: (l, 0, n)),
        out_shape=jax.ShapeDtypeStruct((n_layers, PAD_ROWS, d6), F32),
        compiler_params=_params("parallel", "parallel"),
        name="ada",
    )(cc, w_ada, b_ada.reshape(n_layers, 1, d6))


def _rope(x, c, sa, sb):
    return x * c + pltpu.roll(x, LANES - 16, 1) * sa + pltpu.roll(x, 16, 1) * sb


def _proj_kernel(xc_ref, xp_ref, xn_ref, mod_ref, n1g_ref, wqkv_ref, wmain_ref, qag_ref, kvag_ref,
                 wq_ref, wk_ref, wv_ref, ctab_ref, sa_ref, sb_ref, convw_ref, alog_ref, dtb_ref,
                 q_ref, kt_ref, v_ref, gq_ref, gk_ref, gv_ref, gb_ref, z_ref,
                 hext_ref, pqkv_ref, *, tm, n_ctx_tiles, n_tiles, d_model):
    t = pl.program_id(1)
    halo = BF16_SUBLANES
    mod = mod_ref[...]
    sh1 = mod[:, 0:d_model]
    sc1 = mod[:, d_model:2 * d_model]
    n1g = n1g_ref[...]

    def modulate(x):
        y = x * lax.rsqrt(jnp.mean(x * x, axis=-1, keepdims=True) + NORM_EPS)
        return ((y * n1g) * (1.0 + sc1) + sh1).astype(BF16)

    hext_ref[0:halo, :] = modulate(xp_ref[...])
    hext_ref[halo:halo + tm, :] = modulate(xc_ref[...])
    hext_ref[halo + tm:, :] = modulate(xn_ref[...])

    pqkv_ref[...] = _dot(hext_ref[...], wqkv_ref[...])
    seg_first = jnp.logical_or(t == 0, t == n_ctx_tiles)
    seg_last = jnp.logical_or(t == n_ctx_tiles - 1, t == n_tiles - 1)

    @pl.when(seg_first)
    def _():
        pqkv_ref[0:halo, :] = jnp.zeros((halo, 3 * GDN_WIDTH), F32)

    @pl.when(seg_last)
    def _():
        pqkv_ref[halo + tm:, :] = jnp.zeros((halo, 3 * GDN_WIDTH), F32)

    pm = _dot(hext_ref[halo:halo + tm, :], wmain_ref[...])
    o_kva = MLA_Q_RANK
    o_kr = o_kva + MLA_KV_RANK
    o_z = o_kr + LANES
    o_ab = o_z + GDN_WIDTH
    qa = pm[:, 0:o_kva]
    kva = pm[:, o_kva:o_kr]
    kr = pm[:, o_kr:o_z]
    z_ref[...] = pm[:, o_z:o_ab]
    ab = pm[:, o_ab:o_ab + LANES]

    ctab = ctab_ref[...]
    sa = sa_ref[...]
    sb = sb_ref[...]

    qn = (qa * lax.rsqrt(jnp.mean(qa * qa, axis=-1, keepdims=True) + NORM_EPS) * qag_ref[...]).astype(BF16)
    q = _dot(qn, wq_ref[...])
    scale = MLA_QK_DIM ** -0.5
    for h in range(MLA_HEADS):
        qh = q[:, h * LANES:(h + 1) * LANES]
        q_ref[:, h * LANES:(h + 1) * LANES] = (_rope(qh, ctab, sa, sb) * scale).astype(BF16)

    kvn = (kva * lax.rsqrt(jnp.mean(kva * kva, axis=-1, keepdims=True) + NORM_EPS) * kvag_ref[...]).astype(BF16)
    kk = _dot(kvn, wk_ref[...])
    kpe = _rope(kr, ctab, sa, sb)
    for h in range(MLA_HEADS):
        kh = kk[:, h * LANES:(h + 1) * LANES] + kpe
        kt_ref[h] = kh.T.astype(BF16)
    vv = _dot(kvn, wv_ref[...])
    lane = lax.broadcasted_iota(jnp.int32, vv.shape, 1)
    v_ref[...] = jnp.where(lane % LANES == MLA_V_DIM, 1.0, vv).astype(BF16)

    cw = convw_ref[...]
    base = halo - GDN_CONV // 2
    for cb in range(3 * GDN_HEADS):
        cs = slice(cb * LANES, (cb + 1) * LANES)
        acc = pqkv_ref[base:base + tm, cs] * cw[0:1, cs]
        for j in range(1, GDN_CONV):
            acc = acc + pqkv_ref[base + j:base + j + tm, cs] * cw[j:j + 1, cs]
        act = _silu(acc)
        grp, hh = divmod(cb, GDN_HEADS)
        hs = slice(hh * LANES, (hh + 1) * LANES)
        if grp < 2:
            act = act * lax.rsqrt(jnp.sum(act * act, axis=-1, keepdims=True) + NORM_EPS)
            (gq_ref if grp == 0 else gk_ref)[:, hs] = act
        else:
            gv_ref[:, hs] = act

    x = ab + dtb_ref[...]
    softplus = jnp.maximum(x, 0.0) + jnp.log(1.0 + jnp.exp(-jnp.abs(x)))
    gval = -jnp.exp(alog_ref[...]) * softplus
    lane_ab = lax.broadcasted_iota(jnp.int32, ab.shape, 1)
    gb_ref[...] = jnp.where(lane_ab < N_DIR * GDN_HEADS, gval, _sigmoid(ab))


def _proj_call(xs, mod, layer, wts, tabs, n_ctx_tiles):
    b, ta, d = xs.shape
    tm = TOKEN_TILE
    n_tiles = ta // tm
    halo = BF16_SUBLANES
    hb = tm // halo
    n_halo_blocks = ta // halo
    d6 = 6 * d

    tok = lambda w: pl.BlockSpec((None, tm, w), lambda bi, t: (bi, t, 0))
    in_specs = [
        tok(d),
        pl.BlockSpec((None, halo, d), lambda bi, t: (bi, jnp.maximum(t * hb - 1, 0), 0)),
        pl.BlockSpec((None, halo, d), lambda bi, t: (bi, jnp.minimum((t + 1) * hb, n_halo_blocks - 1), 0)),
        pl.BlockSpec((None, None, None, 1, d6),
                     lambda bi, t: (layer, bi, jnp.where(t < n_ctx_tiles, 1, 0), 0, 0)),
        _const_spec((1, d)),
        _const_spec((d, 3 * GDN_WIDTH)),
        _const_spec(wts["w_main"].shape),
        _const_spec((1, MLA_Q_RANK)),
        _const_spec((1, MLA_KV_RANK)),
        _const_spec((MLA_Q_RANK, MLA_HEADS * LANES)),
        _const_spec((MLA_KV_RANK, MLA_HEADS * LANES)),
        _const_spec((MLA_KV_RANK, MLA_HEADS * LANES)),
        pl.BlockSpec((tm, LANES), lambda bi, t: (t, 0)),
        pl.BlockSpec((tm, LANES), lambda bi, t: (t, 0)),
        pl.BlockSpec((tm, LANES), lambda bi, t: (t, 0)),
        _const_spec((8, 3 * GDN_WIDTH)),
        _const_spec((1, LANES)),
        _const_spec((1, LANES)),
    ]
    out_specs = [
        tok(MLA_HEADS * LANES),
        pl.BlockSpec((None, MLA_HEADS, LANES, tm), lambda bi, t: (bi, 0, 0, t)),
        tok(MLA_HEADS * LANES),
        tok(GDN_WIDTH), tok(GDN_WIDTH), tok(GDN_WIDTH),
        tok(LANES),
        tok(GDN_WIDTH),
    ]
    out_shape = [
        jax.ShapeDtypeStruct((b, ta, MLA_HEADS * LANES), BF16),
        jax.ShapeDtypeStruct((b, MLA_HEADS, LANES, ta), BF16),
        jax.ShapeDtypeStruct((b, ta, MLA_HEADS * LANES), BF16),
        jax.ShapeDtypeStruct((b, ta, GDN_WIDTH), F32),
        jax.ShapeDtypeStruct((b, ta, GDN_WIDTH), F32),
        jax.ShapeDtypeStruct((b, ta, GDN_WIDTH), F32),
        jax.ShapeDtypeStruct((b, ta, LANES), F32),
        jax.ShapeDtypeStruct((b, ta, GDN_WIDTH), F32),
    ]
    kern = functools.partial(_proj_kernel, tm=tm, n_ctx_tiles=n_ctx_tiles, n_tiles=n_tiles, d_model=d)
    return pl.pallas_call(
        kern,
        grid=(b, n_tiles),
        in_specs=in_specs,
        out_specs=out_specs,
        out_shape=out_shape,
        scratch_shapes=[pltpu.VMEM((tm + 2 * halo, d), BF16),
                        pltpu.VMEM((tm + 2 * halo, 3 * GDN_WIDTH), F32)],
        compiler_params=_params("parallel", "parallel"),
        name="proj",
    )(xs, xs, xs, mod, wts["norm1_g"], wts["w_qkv"], wts["w_main"], wts["q_a_g"], wts["kv_a_g"],
      wts["w_q"], wts["w_k"], wts["w_v"], tabs[0], tabs[1], tabs[2], wts["conv_w"], wts["a_log"], wts["dt_bias"])


def _attn_kernel(q_ref, kt_ref, v_ref, o_ref, *, n_ctx_q, tc, ta, first_q):
    qi = pl.program_id(2) + first_q

    def run(nk):
        outs = []
        for hh in range(2):
            q = q_ref[:, hh * LANES:(hh + 1) * LANES]
            s = _dot(q, kt_ref[hh, :, 0:nk])
            m = jnp.max(s, axis=-1, keepdims=True)
            p = jnp.exp(s - m).astype(BF16)
            o = _dot(p, v_ref[0:nk, hh * LANES:(hh + 1) * LANES])
            outs.append(o[:, 0:MLA_V_DIM] / o[:, MLA_V_DIM:MLA_V_DIM + 1])
        o_ref[...] = jnp.concatenate(outs, axis=1).astype(BF16)

    if first_q < n_ctx_q:
        @pl.when(qi < n_ctx_q)
        def _():
            run(tc)

    @pl.when(qi >= n_ctx_q)
    def _():
        run(ta)


def _attn_call(q, kt, v, tc, last):
    b, ta, _ = q.shape
    tq = TOKEN_TILE
    n_ctx_q = tc // tq
    first_q = n_ctx_q if last else 0
    nq = ta // tq - first_q
    kern = functools.partial(_attn_kernel, n_ctx_q=n_ctx_q, tc=tc, ta=ta, first_q=first_q)
    return pl.pallas_call(
        kern,
        grid=(b, MLA_HEADS // 2, nq),
        in_specs=[
            pl.BlockSpec((None, tq, 2 * LANES), lambda bi, hp, qi: (bi, qi + first_q, hp)),
            pl.BlockSpec((None, 2, LANES, ta), lambda bi, hp, qi: (bi, hp, 0, 0)),
            pl.BlockSpec((None, ta, 2 * LANES), lambda bi, hp, qi: (bi, 0, hp)),
        ],
        out_specs=pl.BlockSpec((None, tq, 2 * MLA_V_DIM), lambda bi, hp, qi: (bi, qi + first_q, hp)),
        out_shape=jax.ShapeDtypeStruct((b, ta, MLA_HEADS * MLA_V_DIM), BF16),
        compiler_params=_params("parallel", "parallel", "arbitrary"),
        name="attn",
    )(q, kt, v)


def _gdn_kernel(qf_ref, kf_ref, vf_ref, gf_ref, qb_ref, kb_ref, vb_ref, gbk_ref, cm_ref,
                of_ref, ob_ref, s_ref):
    j = pl.program_id(1)
    blk = GDN_BLOCK
    c = GDN_CHUNK
    n_chunks = blk // c

    @pl.when(j == 0)
    def _():
        s_ref[...] = jnp.zeros(s_ref.shape, F32)

    row = lax.broadcasted_iota(jnp.int32, (blk, blk), 0)
    col = lax.broadcasted_iota(jnp.int32, (blk, blk), 1)
    same = (row // c) == (col // c)
    eye = jnp.where(row == col, 1.0, 0.0)
    pair_masks = []
    size = 1
    while size < c:
        pair_masks.append(jnp.logical_and((row // (2 * size)) == (col // (2 * size)),
                                          (row // size) != (col // size)))
        size *= 2
    cm = cm_ref[...]
    scale = GDN_HEAD_DIM ** -0.5

    chains = []
    for d, (q_ref, k_ref, v_ref, g_ref, o_ref) in enumerate(
            ((qf_ref, kf_ref, vf_ref, gf_ref, of_ref), (qb_ref, kb_ref, vb_ref, gbk_ref, ob_ref))):
        incl = jnp.logical_and(same, (row >= col) if d == 0 else (row <= col))
        strict = jnp.logical_and(same, (row > col) if d == 0 else (row < col))
        gb = g_ref[...]
        sums = jnp.dot(cm, gb, preferred_element_type=F32, precision=lax.Precision.HIGHEST)
        gc = sums[d * blk:(d + 1) * blk]
        gtot = sums[2 * blk:3 * blk]
        gct = gc.T
        for h in range(GDN_HEADS):
            ln = d * GDN_HEADS + h
            hs = slice(h * LANES, (h + 1) * LANES)
            gcol = gc[:, ln:ln + 1]
            grow = gct[ln:ln + 1, :]
            gl = gtot[:, ln:ln + 1]
            beta = gb[:, N_DIR * GDN_HEADS + ln:N_DIR * GDN_HEADS + ln + 1]
            decay = jnp.exp(jnp.where(incl, gcol - grow, NEG_BIG))
            k = k_ref[:, hs]
            q = q_ref[:, hs] * scale
            v = v_ref[:, hs]
            kbeta = k * beta
            kb16 = k.astype(BF16)
            lower = jnp.where(strict, _dot_nt(kbeta.astype(BF16), kb16) * decay, 0.0)
            egc = jnp.exp(gcol)
            x = jnp.concatenate([v * beta, kbeta * egc], axis=1)
            t_inv = eye - jnp.where(pair_masks[0], lower, 0.0)
            for pm in pair_masks[1:]:
                t16 = t_inv.astype(BF16)
                p = _dot(jnp.where(pm, lower, 0.0).astype(BF16), t16)
                t_inv = t_inv - _dot(t16, p.astype(BF16))
            x = _dot(t_inv.astype(BF16), x.astype(BF16))
            qk = _dot_nt(q.astype(BF16), kb16) * decay
            chains.append(dict(
                u=x[:, 0:LANES], w=x[:, LANES:2 * LANES], qk=qk, qd=q * egc,
                kt=k * jnp.exp(gl - gcol), glast=jnp.exp(gl), d=d, hs=hs, o_ref=o_ref, idx=ln))

    for step in range(n_chunks):
        for ch in chains:
            ci = step if ch["d"] == 0 else n_chunks - 1 - step
            r = slice(ci * c, (ci + 1) * c)
            s = s_ref[ch["idx"]]
            ws = _dot(jnp.concatenate([ch["w"][r], ch["qd"][r]], axis=0).astype(BF16), s.astype(BF16))
            vnew = ch["u"][r] - ws[0:c]
            vn16 = vnew.astype(BF16)
            o = ws[c:2 * c] + _dot(ch["qk"][r, ci * c:(ci + 1) * c].astype(BF16), vn16)
            glast = ch["glast"][ci * c:ci * c + 1, :]
            s_ref[ch["idx"]] = s * glast + _dot_tn(ch["kt"][r].astype(BF16), vn16)
            ch["o_ref"][r, ch["hs"]] = o


def _gdn_call(gq, gk, gv, gb, cm, n_ctx_blocks):
    b, ta, w = gq.shape
    blk = GDN_BLOCK
    nb = ta // blk

    def bwd_block(j):
        return jnp.where(j < n_ctx_blocks, n_ctx_blocks - 1 - j, nb - 1 - (j - n_ctx_blocks))

    f_spec = lambda wd: pl.BlockSpec((None, blk, wd), lambda bi, j: (bi, j, 0))
    b_spec = lambda wd: pl.BlockSpec((None, blk, wd), lambda bi, j: (bi, bwd_block(j), 0))
    return pl.pallas_call(
        _gdn_kernel,
        grid=(b, nb),
        in_specs=[f_spec(w), f_spec(w), f_spec(w), f_spec(LANES),
                  b_spec(w), b_spec(w), b_spec(w), b_spec(LANES),
                  _const_spec(cm.shape)],
        out_specs=[f_spec(w), b_spec(w)],
        out_shape=[jax.ShapeDtypeStruct((b, ta, w), F32), jax.ShapeDtypeStruct((b, ta, w), F32)],
        scratch_shapes=[pltpu.VMEM((N_DIR * GDN_HEADS, GDN_HEAD_DIM, GDN_HEAD_DIM), F32)],
        compiler_params=_params("parallel", "arbitrary"),
        name="gdn",
    )(gq, gk, gv, gb, gq, gk, gv, gb, cm)


def _out_kernel(x_ref, attn_ref, of_ref, ob_ref, z_ref, mod_ref, gng_ref, wout_ref, n2g_ref,
                w1_ref, w2_ref, fng_ref, o_ref, *, d_model, final):
    mod = mod_ref[...]
    g1 = mod[:, 2 * d_model:3 * d_model]
    sh2 = mod[:, 3 * d_model:4 * d_model]
    sc2 = mod[:, 4 * d_model:5 * d_model]
    g2 = mod[:, 5 * d_model:6 * d_model]

    o = of_ref[...] + ob_ref[...]
    z = z_ref[...]
    gng = gng_ref[...]
    parts = [attn_ref[...]]
    for h in range(GDN_HEADS):
        hs = slice(h * LANES, (h + 1) * LANES)
        oh = o[:, hs]
        y = oh * lax.rsqrt(jnp.mean(oh * oh, axis=-1, keepdims=True) + NORM_EPS) * gng
        parts.append((y * _silu(z[:, hs])).astype(BF16))
    mix = jnp.concatenate(parts, axis=1)
    x1 = x_ref[...] + g1 * _dot(mix, wout_ref[...])

    y2 = x1 * lax.rsqrt(jnp.mean(x1 * x1, axis=-1, keepdims=True) + NORM_EPS)
    h2 = ((y2 * n2g_ref[...]) * (1.0 + sc2) + sh2).astype(BF16)
    ff = jnp.maximum(_dot(h2, w1_ref[...]), 0.0)
    ff = (ff * ff).astype(BF16)
    x2 = x1 + g2 * _dot(ff, w2_ref[...])
    if final:
        x2 = x2 * lax.rsqrt(jnp.mean(x2 * x2, axis=-1, keepdims=True) + NORM_EPS) * fng_ref[...]
    o_ref[...] = x2


def _out_call(xs, attn, o_f, o_b, z, mod, layer, wts, fng, n_ctx_tiles, final):
    b, ta, d = xs.shape
    tm = TOKEN_TILE
    d6 = 6 * d
    first = n_ctx_tiles if final else 0
    n_tiles = ta // tm - first
    tok = lambda w: pl.BlockSpec((None, tm, w), lambda bi, t: (bi, t + first, 0))
    kern = functools.partial(_out_kernel, d_model=d, final=final)
    out_rows = n_tiles * tm
    return pl.pallas_call(
        kern,
        grid=(b, n_tiles),
        in_specs=[
            tok(d), tok(MLA_HEADS * MLA_V_DIM), tok(GDN_WIDTH), tok(GDN_WIDTH), tok(GDN_WIDTH),
            pl.BlockSpec((None, None, None, 1, d6),
                         lambda bi, t: (layer, bi, jnp.where(t + first < n_ctx_tiles, 1, 0), 0, 0)),
            _const_spec((1, GDN_HEAD_DIM)),
            _const_spec(wts["w_out"].shape),
            _const_spec((1, d)),
            _const_spec(wts["w_ff1"].shape),
            _const_spec(wts["w_ff2"].shape),
            _const_spec((1, d)),
        ],
        out_specs=pl.BlockSpec((None, tm, d), lambda bi, t: (bi, t, 0)),
        out_shape=jax.ShapeDtypeStruct((b, out_rows, d), F32),
        compiler_params=_params("parallel", "parallel"),
        name="out_final" if final else "out",
    )(xs, attn, o_f, o_b, z, mod, wts["gdn_norm_g"], wts["w_out"], wts["norm2_g"],
      wts["w_ff1"], wts["w_ff2"], fng)


def _rope_tables(t_lat, t_ctx):
    rows = t_lat // GRID_W
    row = jnp.broadcast_to(jnp.arange(rows)[:, None], (rows, GRID_W)).reshape(-1).astype(F32)
    col = jnp.broadcast_to(jnp.arange(GRID_W)[None, :], (rows, GRID_W)).reshape(-1).astype(F32)
    axis_pairs = MLA_ROPE_DIM // 4
    inv_freq = ROPE_THETA ** (-jnp.arange(axis_pairs, dtype=F32) / axis_pairs)
    ang = jnp.concatenate([row[:, None] * inv_freq, col[:, None] * inv_freq], axis=-1)
    cos, sin = jnp.cos(ang), jnp.sin(ang)
    half = MLA_ROPE_DIM // 2
    lo, mid, hi = MLA_NOPE_DIM, MLA_NOPE_DIM + half, MLA_NOPE_DIM + 2 * half
    ta = t_ctx + t_lat
    ctab = jnp.ones((ta, LANES), F32).at[t_ctx:, lo:mid].set(cos).at[t_ctx:, mid:hi].set(cos)
    sa = jnp.zeros((ta, LANES), F32).at[t_ctx:, lo:mid].set(-sin)
    sb = jnp.zeros((ta, LANES), F32).at[t_ctx:, mid:hi].set(sin)
    return ctab, sa, sb


def _chunk_sum_matrices():
    i = jnp.arange(GDN_BLOCK)
    same = (i[:, None] // GDN_CHUNK) == (i[None, :] // GDN_CHUNK)
    lower = same & (i[:, None] >= i[None, :])
    upper = same & (i[:, None] <= i[None, :])
    return jnp.concatenate([lower, upper, same], axis=0).astype(F32)


def _pad_heads(w, used):
    k = w.shape[0]
    w = w.reshape(k, MLA_HEADS, used)
    return jnp.pad(w, ((0, 0), (0, 0), (0, LANES - used))).reshape(k, MLA_HEADS * LANES)


def _layer_weights(i, norm1_g, w_in, q_a_g, w_q_b, kv_a_g, w_kv_b, conv_w, a_log, dt_bias, gdn_norm_g,
                   w_out, norm2_g, w_ff1, w_ff2):
    d = w_in.shape[1]
    wi = w_in[i]
    o = 0
    cols = []
    for sz in (MLA_Q_RANK, MLA_KV_RANK, MLA_ROPE_DIM, 3 * GDN_WIDTH, GDN_WIDTH,
               N_DIR * GDN_HEADS, N_DIR * GDN_HEADS):
        cols.append(wi[:, o:o + sz])
        o += sz
    w_qa, w_kva, w_kr, w_qkv, w_z, w_a, w_b = cols
    w_kr_pad = jnp.zeros((d, LANES), F32).at[:, MLA_NOPE_DIM:MLA_QK_DIM].set(w_kr)
    nab = N_DIR * GDN_HEADS
    w_ab_pad = jnp.zeros((d, LANES), F32).at[:, 0:nab].set(w_a).at[:, nab:2 * nab].set(w_b)
    w_main = jnp.concatenate([w_qa, w_kva, w_kr_pad, w_z, w_ab_pad], axis=1)
    kv = w_kv_b[i].reshape(MLA_KV_RANK, MLA_HEADS, MLA_NOPE_DIM + MLA_V_DIM)
    w_k = _pad_heads(kv[:, :, :MLA_NOPE_DIM].reshape(MLA_KV_RANK, -1), MLA_NOPE_DIM)
    w_v = _pad_heads(kv[:, :, MLA_NOPE_DIM:].reshape(MLA_KV_RANK, -1), MLA_V_DIM)
    row = lambda v: jnp.zeros((1, LANES), F32).at[0, 0:nab].set(v.reshape(-1))
    return dict(
        norm1_g=norm1_g[i][None, :],
        w_qkv=w_qkv.astype(BF16),
        w_main=w_main.astype(BF16),
        q_a_g=q_a_g[i][None, :],
        kv_a_g=kv_a_g[i][None, :],
        w_q=_pad_heads(w_q_b[i], MLA_QK_DIM).astype(BF16),
        w_k=w_k.astype(BF16),
        w_v=w_v.astype(BF16),
        conv_w=jnp.pad(conv_w[i], ((0, 8 - GDN_CONV), (0, 0))),
        a_log=row(a_log[i]),
        dt_bias=row(dt_bias[i]),
        gdn_norm_g=gdn_norm_g[i][None, :],
        w_out=w_out[i].astype(BF16),
        norm2_g=norm2_g[i][None, :],
        w_ff1=w_ff1[i].astype(BF16),
        w_ff2=w_ff2[i].astype(BF16),
    )


def kernel(x, c, ctx, c_ctx, w_ada, b_ada, norm1_g, w_in, q_a_g, w_q_b, kv_a_g, w_kv_b, conv_w, a_log,
           dt_bias, gdn_norm_g, w_out, norm2_g, w_ff1, w_ff2, final_norm_g):
    b, t_lat, d = x.shape
    t_ctx = ctx.shape[1]
    depth = w_ada.shape[0]
    assert t_ctx % TOKEN_TILE == 0 and t_lat % TOKEN_TILE == 0 and t_lat % GRID_W == 0
    assert b + 1 <= PAD_ROWS
    n_ctx_tiles = t_ctx // TOKEN_TILE

    xs = jnp.concatenate([ctx, x], axis=1)
    cc = jnp.zeros((PAD_ROWS, d), F32).at[0:b].set(c).at[b].set(c_ctx)
    mods = _ada_call(cc, w_ada, b_ada)
    mod = jnp.stack([mods[:, 0:b], jnp.broadcast_to(mods[:, b:b + 1], (depth, b, 6 * d))], axis=2)
    mod = mod[:, :, :, None, :]
    tabs = _rope_tables(t_lat, t_ctx)
    cm = _chunk_sum_matrices()
    fng = final_norm_g[None, :]

    for i in range(depth):
        last = i == depth - 1
        wts = _layer_weights(i, norm1_g, w_in, q_a_g, w_q_b, kv_a_g, w_kv_b, conv_w, a_log, dt_bias,
                             gdn_norm_g, w_out, norm2_g, w_ff1, w_ff2)
        q, kt, v, gq, gk, gv, gb, z = _proj_call(xs, mod, i, wts, tabs, n_ctx_tiles)
        attn = _attn_call(q, kt, v, t_ctx, last)
        o_f, o_b = _gdn_call(gq, gk, gv, gb, cm, t_ctx // GDN_BLOCK)
        xs = _out_call(xs, attn, o_f, o_b, z, mod, i, wts, fng, n_ctx_tiles, last)
    return xs
```

```python
import functools

import jax
import jax.numpy as jnp
from jax import lax
from jax.experimental import pallas as pl
from jax.experimental.pallas import tpu as pltpu

F32 = jnp.float32
BF16 = jnp.bfloat16

GRID_W = 64
MLA_HEADS = 8
MLA_NOPE_DIM = 64
MLA_ROPE_DIM = 32
MLA_V_DIM = 64
MLA_Q_RANK = 256
MLA_KV_RANK = 128
MLA_QK_DIM = MLA_NOPE_DIM + MLA_ROPE_DIM
GDN_HEADS = 4
GDN_HEAD_DIM = 128
GDN_WIDTH = GDN_HEADS * GDN_HEAD_DIM
GDN_CONV = 5
GDN_CHUNK = 64
N_DIR = 2
ROPE_THETA = 10000.0
NORM_EPS = 1e-6
LOG2_E = 1.4426950408889634

LANES = 128
BF16_SUBLANES = 16
VMEM_LIMIT_BYTES = 56 * 1024 * 1024

TOKEN_TILE = 256
GDN_BLOCK = 256
NEG_BIG = -1e30
PAD_ROWS = 16


def _sigmoid(x):
    return 1.0 / (1.0 + jnp.exp(-x))


def _silu(x):
    return x * _sigmoid(x)


def _dot(a, b):
    return jnp.dot(a, b, preferred_element_type=F32)


def _dot_nt(a, b):
    return lax.dot_general(a, b, (((1,), (1,)), ((), ())), preferred_element_type=F32)


def _dot_tn(a, b):
    return lax.dot_general(a, b, (((0,), (0,)), ((), ())), preferred_element_type=F32)


def _const_spec(shape):
    nd = len(shape)
    return pl.BlockSpec(shape, lambda *_: (0,) * nd, pipeline_mode=pl.Buffered(1))


def _params(*sem):
    return pltpu.CompilerParams(dimension_semantics=sem, vmem_limit_bytes=VMEM_LIMIT_BYTES)


def _ada_kernel(c_ref, w_ref, b_ref, o_ref):
    s = _silu(c_ref[...]).astype(BF16)
    o_ref[0] = _dot(s, w_ref[0].astype(BF16)) + b_ref[0]


def _ada_call(cc, w_ada, b_ada):
    n_layers, d, d6 = w_ada.shape
    tn = 1536
    return pl.pallas_call(
        _ada_kernel,
        grid=(n_layers, d6 // tn),
        in_specs=[
            pl.BlockSpec((PAD_ROWS, d), lambda l, n: (0, 0)),
            pl.BlockSpec((1, d, tn), lambda l, n: (l, 0, n)),
            pl.BlockSpec((1, 1, tn), lambda l, n: (l, 0, n)),
        ],
        out_specs=pl.BlockSpec((1, PAD_ROWS, tn), lambda l, n: (l, 0, n)),
        out_shape=jax.ShapeDtypeStruct((n_layers, PAD_ROWS, d6), F32),
        compiler_params=_params("parallel", "parallel"),
        name="ada",
    )(cc, w_ada, b_ada.reshape(n_layers, 1, d6))


def _rope(x, c, sa, sb):
    return x * c + pltpu.roll(x, LANES - 16, 1) * sa + pltpu.roll(x, 16, 1) * sb


def _proj_kernel(xc_ref, xp_ref, xn_ref, mod_ref, n1g_ref, wqkv_ref, wmain_ref, qag_ref, kvag_ref,
                 wq_ref, wk_ref, wv_ref, ctab_ref, sa_ref, sb_ref, convw_ref, alog_ref, dtb_ref,
                 q_ref, kt_ref, v_ref, gq_ref, gk_ref, gv_ref, gb_ref, z_ref,
                 hext_ref, pqkv_ref, *, tm, n_ctx_tiles, n_tiles, d_model):
    t = pl.program_id(1)
    halo = BF16_SUBLANES
    mod = mod_ref[...]
    sh1 = mod[:, 0:d_model]
    sc1 = mod[:, d_model:2 * d_model]
    n1g = n1g_ref[...]

    def modulate(x):
        y = x * lax.rsqrt(jnp.mean(x * x, axis=-1, keepdims=True) + NORM_EPS)
        return ((y * n1g) * (1.0 + sc1) + sh1).astype(BF16)

    hext_ref[0:halo, :] = modulate(xp_ref[...])
    hext_ref[halo:halo + tm, :] = modulate(xc_ref[...])
    hext_ref[halo + tm:, :] = modulate(xn_ref[...])

    pqkv_ref[...] = _dot(hext_ref[...], wqkv_ref[...])
    seg_first = jnp.logical_or(t == 0, t == n_ctx_tiles)
    seg_last = jnp.logical_or(t == n_ctx_tiles - 1, t == n_tiles - 1)

    @pl.when(seg_first)
    def _():
        pqkv_ref[0:halo, :] = jnp.zeros((halo, 3 * GDN_WIDTH), F32)

    @pl.when(seg_last)
    def _():
        pqkv_ref[halo + tm:, :] = jnp.zeros((halo, 3 * GDN_WIDTH), F32)

    pm = _dot(hext_ref[halo:halo + tm, :], wmain_ref[...])
    o_kva = MLA_Q_RANK
    o_kr = o_kva + MLA_KV_RANK
    o_z = o_kr + LANES
    o_ab = o_z + GDN_WIDTH
    qa = pm[:, 0:o_kva]
    kva = pm[:, o_kva:o_kr]
    kr = pm[:, o_kr:o_z]
    z_ref[...] = pm[:, o_z:o_ab]
    ab = pm[:, o_ab:o_ab + LANES]

    ctab = ctab_ref[...]
    sa = sa_ref[...]
    sb = sb_ref[...]

    qn = (qa * lax.rsqrt(jnp.mean(qa * qa, axis=-1, keepdims=True) + NORM_EPS) * qag_ref[...]).astype(BF16)
    q = _dot(qn, wq_ref[...])
    scale = MLA_QK_DIM ** -0.5 * LOG2_E
    for h in range(MLA_HEADS):
        qh = q[:, h * LANES:(h + 1) * LANES]
        q_ref[:, h * LANES:(h + 1) * LANES] = (_rope(qh, ctab, sa, sb) * scale).astype(BF16)

    kvn = (kva * lax.rsqrt(jnp.mean(kva * kva, axis=-1, keepdims=True) + NORM_EPS) * kvag_ref[...]).astype(BF16)
    kk = _dot(kvn, wk_ref[...])
    kpe = _rope(kr, ctab, sa, sb)
    for h in range(MLA_HEADS):
        kh = kk[:, h * LANES:(h + 1) * LANES] + kpe
        kt_ref[h] = kh.T.astype(BF16)
    vv = _dot(kvn, wv_ref[...])
    lane = lax.broadcasted_iota(jnp.int32, vv.shape, 1)
    v_ref[...] = jnp.where(lane % LANES == MLA_V_DIM, 1.0, vv).astype(BF16)

    cw = convw_ref[...]
    base = halo - GDN_CONV // 2
    for cb in range(3 * GDN_HEADS):
        cs = slice(cb * LANES, (cb + 1) * LANES)
        acc = pqkv_ref[base:base + tm, cs] * cw[0:1, cs]
        for j in range(1, GDN_CONV):
            acc = acc + pqkv_ref[base + j:base + j + tm, cs] * cw[j:j + 1, cs]
        act = _silu(acc)
        grp, hh = divmod(cb, GDN_HEADS)
        hs = slice(hh * LANES, (hh + 1) * LANES)
        if grp < 2:
            act = act * lax.rsqrt(jnp.sum(act * act, axis=-1, keepdims=True) + NORM_EPS)
            (gq_ref if grp == 0 else gk_ref)[:, hs] = act
        else:
            gv_ref[:, hs] = act

    x = ab + dtb_ref[...]
    softplus = jnp.maximum(x, 0.0) + jnp.log(1.0 + jnp.exp(-jnp.abs(x)))
    gval = -jnp.exp(alog_ref[...]) * softplus
    lane_ab = lax.broadcasted_iota(jnp.int32, ab.shape, 1)
    gb_ref[...] = jnp.where(lane_ab < N_DIR * GDN_HEADS, gval, _sigmoid(ab))


def _proj_call(xs, mod, layer, wts, tabs, n_ctx_tiles):
    b, ta, d = xs.shape
    tm = TOKEN_TILE
    n_tiles = ta // tm
    halo = BF16_SUBLANES
    hb = tm // halo
    n_halo_blocks = ta // halo
    d6 = 6 * d

    tok = lambda w: pl.BlockSpec((None, tm, w), lambda bi, t: (bi, t, 0))
    in_specs = [
        tok(d),
        pl.BlockSpec((None, halo, d), lambda bi, t: (bi, jnp.maximum(t * hb - 1, 0), 0)),
        pl.BlockSpec((None, halo, d), lambda bi, t: (bi, jnp.minimum((t + 1) * hb, n_halo_blocks - 1), 0)),
        pl.BlockSpec((None, None, None, 1, d6),
                     lambda bi, t: (layer, bi, jnp.where(t < n_ctx_tiles, 1, 0), 0, 0)),
        _const_spec((1, d)),
        _const_spec((d, 3 * GDN_WIDTH)),
        _const_spec(wts["w_main"].shape),
        _const_spec((1, MLA_Q_RANK)),
        _const_spec((1, MLA_KV_RANK)),
        _const_spec((MLA_Q_RANK, MLA_HEADS * LANES)),
        _const_spec((MLA_KV_RANK, MLA_HEADS * LANES)),
        _const_spec((MLA_KV_RANK, MLA_HEADS * LANES)),
        pl.BlockSpec((tm, LANES), lambda bi, t: (t, 0)),
        pl.BlockSpec((tm, LANES), lambda bi, t: (t, 0)),
        pl.BlockSpec((tm, LANES), lambda bi, t: (t, 0)),
        _const_spec((8, 3 * GDN_WIDTH)),
        _const_spec((1, LANES)),
        _const_spec((1, LANES)),
    ]
    out_specs = [
        tok(MLA_HEADS * LANES),
        pl.BlockSpec((None, MLA_HEADS, LANES, tm), lambda bi, t: (bi, 0, 0, t)),
        tok(MLA_HEADS * LANES),
        tok(GDN_WIDTH), tok(GDN_WIDTH), tok(GDN_WIDTH),
        tok(LANES),
        tok(GDN_WIDTH),
    ]
    out_shape = [
        jax.ShapeDtypeStruct((b, ta, MLA_HEADS * LANES), BF16),
        jax.ShapeDtypeStruct((b, MLA_HEADS, LANES, ta), BF16),
        jax.ShapeDtypeStruct((b, ta, MLA_HEADS * LANES), BF16),
        jax.ShapeDtypeStruct((b, ta, GDN_WIDTH), F32),
        jax.ShapeDtypeStruct((b, ta, GDN_WIDTH), F32),
        jax.ShapeDtypeStruct((b, ta, GDN_WIDTH), F32),
        jax.ShapeDtypeStruct((b, ta, LANES), F32),
        jax.ShapeDtypeStruct((b, ta, GDN_WIDTH), F32),
    ]
    kern = functools.partial(_proj_kernel, tm=tm, n_ctx_tiles=n_ctx_tiles, n_tiles=n_tiles, d_model=d)
    return pl.pallas_call(
        kern,
        grid=(b, n_tiles),
        in_specs=in_specs,
        out_specs=out_specs,
        out_shape=out_shape,
        scratch_shapes=[pltpu.VMEM((tm + 2 * halo, d), BF16),
                        pltpu.VMEM((tm + 2 * halo, 3 * GDN_WIDTH), F32)],
        compiler_params=_params("parallel", "parallel"),
        name="proj",
    )(xs, xs, xs, mod, wts["norm1_g"], wts["w_qkv"], wts["w_main"], wts["q_a_g"], wts["kv_a_g"],
      wts["w_q"], wts["w_k"], wts["w_v"], tabs[0], tabs[1], tabs[2], wts["conv_w"], wts["a_log"], wts["dt_bias"])


def _attn_kernel(q_ref, kt_ref, v_ref, o_ref, *, n_ctx_q, tc, ta, first_q):
    qi = pl.program_id(1) + first_q

    def scores(h, nk):
        return _dot(q_ref[:, h * LANES:(h + 1) * LANES], kt_ref[h, :, 0:nk])

    def run(nk):
        s_next = scores(0, nk)
        for h in range(MLA_HEADS):
            s = s_next
            if h + 1 < MLA_HEADS:
                s_next = scores(h + 1, nk)
            m = jnp.max(s, axis=-1, keepdims=True)
            p = jnp.exp2(s - m).astype(BF16)
            o = _dot(p, v_ref[0:nk, h * LANES:(h + 1) * LANES])
            o_ref[:, h * MLA_V_DIM:(h + 1) * MLA_V_DIM] = (
                o[:, 0:MLA_V_DIM] / o[:, MLA_V_DIM:MLA_V_DIM + 1]).astype(BF16)

    if first_q < n_ctx_q:
        @pl.when(qi < n_ctx_q)
        def _():
            run(tc)

    @pl.when(qi >= n_ctx_q)
    def _():
        run(ta)


def _attn_call(q, kt, v, tc, last):
    b, ta, _ = q.shape
    tq = TOKEN_TILE
    n_ctx_q = tc // tq
    first_q = n_ctx_q if last else 0
    nq = ta // tq - first_q
    kern = functools.partial(_attn_kernel, n_ctx_q=n_ctx_q, tc=tc, ta=ta, first_q=first_q)
    return pl.pallas_call(
        kern,
        grid=(b, nq),
        in_specs=[
            pl.BlockSpec((None, tq, MLA_HEADS * LANES), lambda bi, qi: (bi, qi + first_q, 0)),
            pl.BlockSpec((None, MLA_HEADS, LANES, ta), lambda bi, qi: (bi, 0, 0, 0)),
            pl.BlockSpec((None, ta, MLA_HEADS * LANES), lambda bi, qi: (bi, 0, 0)),
        ],
        out_specs=pl.BlockSpec((None, tq, MLA_HEADS * MLA_V_DIM), lambda bi, qi: (bi, qi + first_q, 0)),
        out_shape=jax.ShapeDtypeStruct((b, ta, MLA_HEADS * MLA_V_DIM), BF16),
        compiler_params=_params("parallel", "arbitrary"),
        name="attn",
    )(q, kt, v)


def _gdn_kernel(qf_ref, kf_ref, vf_ref, gf_ref, qb_ref, kb_ref, vb_ref, gbk_ref, cm_ref,
                of_ref, ob_ref, s_ref):
    j = pl.program_id(1)
    blk = GDN_BLOCK
    c = GDN_CHUNK
    n_chunks = blk // c

    @pl.when(j == 0)
    def _():
        s_ref[...] = jnp.zeros(s_ref.shape, F32)

    row = lax.broadcasted_iota(jnp.int32, (blk, blk), 0)
    col = lax.broadcasted_iota(jnp.int32, (blk, blk), 1)
    same = (row // c) == (col // c)
    eye = jnp.where(row == col, 1.0, 0.0)
    pair_masks = []
    size = 1
    while size < c:
        pair_masks.append(jnp.logical_and((row // (2 * size)) == (col // (2 * size)),
                                          (row // size) != (col // size)))
        size *= 2
    scale = GDN_HEAD_DIM ** -0.5

    chains = []
    for d, (q_ref, k_ref, v_ref, g_ref, o_ref) in enumerate(
            ((qf_ref, kf_ref, vf_ref, gf_ref, of_ref), (qb_ref, kb_ref, vb_ref, gbk_ref, ob_ref))):
        incl = jnp.logical_and(same, (row >= col) if d == 0 else (row <= col))
        strict = jnp.logical_and(same, (row > col) if d == 0 else (row < col))
        gb = g_ref[...]
        g1 = gb.astype(BF16)
        r1 = gb - g1.astype(F32)
        g2 = r1.astype(BF16)
        g3 = (r1 - g2.astype(F32)).astype(BF16)
        cmd = cm_ref[d]
        gc = _dot(cmd, g1) + _dot(cmd, g2) + _dot(cmd, g3)
        last = c - 1 if d == 0 else 0
        gtot = jnp.concatenate(
            [jnp.broadcast_to(gc[ci * c + last:ci * c + last + 1, :], (c, LANES)) for ci in range(n_chunks)],
            axis=0)
        gct = gc.T
        for h in range(GDN_HEADS):
            ln = d * GDN_HEADS + h
            hs = slice(h * LANES, (h + 1) * LANES)
            gcol = gc[:, ln:ln + 1]
            gl = gtot[:, ln:ln + 1]
            beta = gb[:, N_DIR * GDN_HEADS + ln:N_DIR * GDN_HEADS + ln + 1]
            decay = jnp.exp(jnp.where(incl, gcol - gct[ln:ln + 1, :], NEG_BIG))
            k = k_ref[:, hs]
            q = q_ref[:, hs] * scale
            kbeta = k * beta
            egc = jnp.exp(gcol)
            chains.append(dict(
                d=d, hs=hs, o_ref=o_ref, idx=ln, strict=strict, decay=decay, kb16=k.astype(BF16),
                kbeta16=kbeta.astype(BF16), q16=q.astype(BF16),
                rhs16=jnp.concatenate([v_ref[:, hs] * beta, kbeta * egc], axis=1).astype(BF16),
                qd=q * egc, kt=k * jnp.exp(gl - gcol), glast=jnp.exp(gl)))

    for ch in chains:
        gram = _dot_nt(ch["kbeta16"], ch["kb16"])
        ch["lower"] = jnp.where(ch["strict"], gram * ch["decay"], 0.0)
        ch["t"] = eye - jnp.where(pair_masks[0], ch["lower"], 0.0)
    for pm in pair_masks[1:]:
        for ch in chains:
            ch["t16"] = ch["t"].astype(BF16)
            ch["p16"] = _dot(jnp.where(pm, ch["lower"], 0.0).astype(BF16), ch["t16"]).astype(BF16)
        for ch in chains:
            ch["t"] = ch["t"] - _dot(ch["t16"], ch["p16"])
    for ch in chains:
        x = _dot(ch["t"].astype(BF16), ch["rhs16"])
        ch["u"] = x[:, 0:LANES]
        ch["w"] = x[:, LANES:2 * LANES]
        ch["qk"] = _dot_nt(ch["q16"], ch["kb16"]) * ch["decay"]

    for step in range(n_chunks):
        for ch in chains:
            ci = step if ch["d"] == 0 else n_chunks - 1 - step
            r = slice(ci * c, (ci + 1) * c)
            s = s_ref[ch["idx"]]
            ws = _dot(jnp.concatenate([ch["w"][r], ch["qd"][r]], axis=0).astype(BF16), s.astype(BF16))
            vnew = ch["u"][r] - ws[0:c]
            vn16 = vnew.astype(BF16)
            o = ws[c:2 * c] + _dot(ch["qk"][r, ci * c:(ci + 1) * c].astype(BF16), vn16)
            glast = ch["glast"][ci * c:ci * c + 1, :]
            s_ref[ch["idx"]] = s * glast + _dot_tn(ch["kt"][r].astype(BF16), vn16)
            ch["o_ref"][r, ch["hs"]] = o


def _gdn_call(gq, gk, gv, gb, cm, n_ctx_blocks):
    b, ta, w = gq.shape
    blk = GDN_BLOCK
    nb = ta // blk

    def bwd_block(j):
        return jnp.where(j < n_ctx_blocks, n_ctx_blocks - 1 - j, nb - 1 - (j - n_ctx_blocks))

    f_spec = lambda wd: pl.BlockSpec((None, blk, wd), lambda bi, j: (bi, j, 0))
    b_spec = lambda wd: pl.BlockSpec((None, blk, wd), lambda bi, j: (bi, bwd_block(j), 0))
    return pl.pallas_call(
        _gdn_kernel,
        grid=(b, nb),
        in_specs=[f_spec(w), f_spec(w), f_spec(w), f_spec(LANES),
                  b_spec(w), b_spec(w), b_spec(w), b_spec(LANES),
                  _const_spec(cm.shape)],
        out_specs=[f_spec(w), b_spec(w)],
        out_shape=[jax.ShapeDtypeStruct((b, ta, w), F32), jax.ShapeDtypeStruct((b, ta, w), F32)],
        scratch_shapes=[pltpu.VMEM((N_DIR * GDN_HEADS, GDN_HEAD_DIM, GDN_HEAD_DIM), F32)],
        compiler_params=_params("parallel", "arbitrary"),
        name="gdn",
    )(gq, gk, gv, gb, gq, gk, gv, gb, cm)


def _out_kernel(x_ref, attn_ref, of_ref, ob_ref, z_ref, mod_ref, gng_ref, wout_ref, n2g_ref,
                w1_ref, w2_ref, fng_ref, o_ref, *, d_model, final):
    mod = mod_ref[...]
    g1 = mod[:, 2 * d_model:3 * d_model]
    sh2 = mod[:, 3 * d_model:4 * d_model]
    sc2 = mod[:, 4 * d_model:5 * d_model]
    g2 = mod[:, 5 * d_model:6 * d_model]

    o = of_ref[...] + ob_ref[...]
    z = z_ref[...]
    gng = gng_ref[...]
    parts = [attn_ref[...]]
    for h in range(GDN_HEADS):
        hs = slice(h * LANES, (h + 1) * LANES)
        oh = o[:, hs]
        y = oh * lax.rsqrt(jnp.mean(oh * oh, axis=-1, keepdims=True) + NORM_EPS) * gng
        parts.append((y * _silu(z[:, hs])).astype(BF16))
    mix = jnp.concatenate(parts, axis=1)
    x1 = x_ref[...] + g1 * _dot(mix, wout_ref[...])

    y2 = x1 * lax.rsqrt(jnp.mean(x1 * x1, axis=-1, keepdims=True) + NORM_EPS)
    h2 = ((y2 * n2g_ref[...]) * (1.0 + sc2) + sh2).astype(BF16)
    ff = jnp.maximum(_dot(h2, w1_ref[...]), 0.0)
    ff = (ff * ff).astype(BF16)
    x2 = x1 + g2 * _dot(ff, w2_ref[...])
    if final:
        x2 = x2 * lax.rsqrt(jnp.mean(x2 * x2, axis=-1, keepdims=True) + NORM_EPS) * fng_ref[...]
    o_ref[...] = x2


def _out_call(xs, attn, o_f, o_b, z, mod, layer, wts, fng, n_ctx_tiles, final):
    b, ta, d = xs.shape
    tm = TOKEN_TILE
    d6 = 6 * d
    first = n_ctx_tiles if final else 0
    n_tiles = ta // tm - first
    tok = lambda w: pl.BlockSpec((None, tm, w), lambda bi, t: (bi, t + first, 0))
    kern = functools.partial(_out_kernel, d_model=d, final=final)
    out_rows = n_tiles * tm
    return pl.pallas_call(
        kern,
        grid=(b, n_tiles),
        in_specs=[
            tok(d), tok(MLA_HEADS * MLA_V_DIM), tok(GDN_WIDTH), tok(GDN_WIDTH), tok(GDN_WIDTH),
            pl.BlockSpec((None, None, None, 1, d6),
                         lambda bi, t: (layer, bi, jnp.where(t + first < n_ctx_tiles, 1, 0), 0, 0)),
            _const_spec((1, GDN_HEAD_DIM)),
            _const_spec(wts["w_out"].shape),
            _const_spec((1, d)),
            _const_spec(wts["w_ff1"].shape),
            _const_spec(wts["w_ff2"].shape),
            _const_spec((1, d)),
        ],
        out_specs=pl.BlockSpec((None, tm, d), lambda bi, t: (bi, t, 0)),
        out_shape=jax.ShapeDtypeStruct((b, out_rows, d), F32),
        compiler_params=_params("parallel", "parallel"),
        name="out_final" if final else "out",
    )(xs, attn, o_f, o_b, z, mod, wts["gdn_norm_g"], wts["w_out"], wts["norm2_g"],
      wts["w_ff1"], wts["w_ff2"], fng)


def _rope_tables(t_lat, t_ctx):
    rows = t_lat // GRID_W
    row = jnp.broadcast_to(jnp.arange(rows)[:, None], (rows, GRID_W)).reshape(-1).astype(F32)
    col = jnp.broadcast_to(jnp.arange(GRID_W)[None, :], (rows, GRID_W)).reshape(-1).astype(F32)
    axis_pairs = MLA_ROPE_DIM // 4
    inv_freq = ROPE_THETA ** (-jnp.arange(axis_pairs, dtype=F32) / axis_pairs)
    ang = jnp.concatenate([row[:, None] * inv_freq, col[:, None] * inv_freq], axis=-1)
    cos, sin = jnp.cos(ang), jnp.sin(ang)
    half = MLA_ROPE_DIM // 2
    lo, mid, hi = MLA_NOPE_DIM, MLA_NOPE_DIM + half, MLA_NOPE_DIM + 2 * half
    ta = t_ctx + t_lat
    ctab = jnp.ones((ta, LANES), F32).at[t_ctx:, lo:mid].set(cos).at[t_ctx:, mid:hi].set(cos)
    sa = jnp.zeros((ta, LANES), F32).at[t_ctx:, lo:mid].set(-sin)
    sb = jnp.zeros((ta, LANES), F32).at[t_ctx:, mid:hi].set(sin)
    return ctab, sa, sb


def _chunk_sum_matrices():
    i = jnp.arange(GDN_BLOCK)
    same = (i[:, None] // GDN_CHUNK) == (i[None, :] // GDN_CHUNK)
    lower = same & (i[:, None] >= i[None, :])
    upper = same & (i[:, None] <= i[None, :])
    return jnp.stack([lower, upper], axis=0).astype(BF16)


def _pad_heads(w, used):
    k = w.shape[0]
    w = w.reshape(k, MLA_HEADS, used)
    return jnp.pad(w, ((0, 0), (0, 0), (0, LANES - used))).reshape(k, MLA_HEADS * LANES)


def _layer_weights(i, norm1_g, w_in, q_a_g, w_q_b, kv_a_g, w_kv_b, conv_w, a_log, dt_bias, gdn_norm_g,
                   w_out, norm2_g, w_ff1, w_ff2):
    d = w_in.shape[1]
    wi = w_in[i]
    o = 0
    cols = []
    for sz in (MLA_Q_RANK, MLA_KV_RANK, MLA_ROPE_DIM, 3 * GDN_WIDTH, GDN_WIDTH,
               N_DIR * GDN_HEADS, N_DIR * GDN_HEADS):
        cols.append(wi[:, o:o + sz])
        o += sz
    w_qa, w_kva, w_kr, w_qkv, w_z, w_a, w_b = cols
    w_kr_pad = jnp.zeros((d, LANES), F32).at[:, MLA_NOPE_DIM:MLA_QK_DIM].set(w_kr)
    nab = N_DIR * GDN_HEADS
    w_ab_pad = jnp.zeros((d, LANES), F32).at[:, 0:nab].set(w_a).at[:, nab:2 * nab].set(w_b)
    w_main = jnp.concatenate([w_qa, w_kva, w_kr_pad, w_z, w_ab_pad], axis=1)
    kv = w_kv_b[i].reshape(MLA_KV_RANK, MLA_HEADS, MLA_NOPE_DIM + MLA_V_DIM)
    w_k = _pad_heads(kv[:, :, :MLA_NOPE_DIM].reshape(MLA_KV_RANK, -1), MLA_NOPE_DIM)
    w_v = _pad_heads(kv[:, :, MLA_NOPE_DIM:].reshape(MLA_KV_RANK, -1), MLA_V_DIM)
    row = lambda v: jnp.zeros((1, LANES), F32).at[0, 0:nab].set(v.reshape(-1))
    return dict(
        norm1_g=norm1_g[i][None, :],
        w_qkv=w_qkv.astype(BF16),
        w_main=w_main.astype(BF16),
        q_a_g=q_a_g[i][None, :],
        kv_a_g=kv_a_g[i][None, :],
        w_q=_pad_heads(w_q_b[i], MLA_QK_DIM).astype(BF16),
        w_k=w_k.astype(BF16),
        w_v=w_v.astype(BF16),
        conv_w=jnp.pad(conv_w[i], ((0, 8 - GDN_CONV), (0, 0))),
        a_log=row(a_log[i]),
        dt_bias=row(dt_bias[i]),
        gdn_norm_g=gdn_norm_g[i][None, :],
        w_out=w_out[i].astype(BF16),
        norm2_g=norm2_g[i][None, :],
        w_ff1=w_ff1[i].astype(BF16),
        w_ff2=w_ff2[i].astype(BF16),
    )


def kernel(x, c, ctx, c_ctx, w_ada, b_ada, norm1_g, w_in, q_a_g, w_q_b, kv_a_g, w_kv_b, conv_w, a_log,
           dt_bias, gdn_norm_g, w_out, norm2_g, w_ff1, w_ff2, final_norm_g):
    b, t_lat, d = x.shape
    t_ctx = ctx.shape[1]
    depth = w_ada.shape[0]
    assert t_ctx % TOKEN_TILE == 0 and t_lat % TOKEN_TILE == 0 and t_lat % GRID_W == 0
    assert b + 1 <= PAD_ROWS
    n_ctx_tiles = t_ctx // TOKEN_TILE

    xs = jnp.concatenate([ctx, x], axis=1)
    cc = jnp.zeros((PAD_ROWS, d), F32).at[0:b].set(c).at[b].set(c_ctx)
    mods = _ada_call(cc, w_ada, b_ada)
    mod = jnp.stack([mods[:, 0:b], jnp.broadcast_to(mods[:, b:b + 1], (depth, b, 6 * d))], axis=2)
    mod = mod[:, :, :, None, :]
    tabs = _rope_tables(t_lat, t_ctx)
    cm = _chunk_sum_matrices()
    fng = final_norm_g[None, :]

    for i in range(depth):
        last = i == depth - 1
        wts = _layer_weights(i, norm1_g, w_in, q_a_g, w_q_b, kv_a_g, w_kv_b, conv_w, a_log, dt_bias,
                             gdn_norm_g, w_out, norm2_g, w_ff1, w_ff2)
        q, kt, v, gq, gk, gv, gb, z = _proj_call(xs, mod, i, wts, tabs, n_ctx_tiles)
        attn = _attn_call(q, kt, v, t_ctx, last)
        o_f, o_b = _gdn_call(gq, gk, gv, gb, cm, t_ctx // GDN_BLOCK)
        xs = _out_call(xs, attn, o_f, o_b, z, mod, i, wts, fng, n_ctx_tiles, last)
    return xs
```

```python
import functools

import jax
import jax.numpy as jnp
from jax import lax
from jax.experimental import pallas as pl
from jax.experimental.pallas import tpu as pltpu

F32 = jnp.float32
BF16 = jnp.bfloat16

GRID_W = 64
MLA_HEADS = 8
MLA_NOPE_DIM = 64
MLA_ROPE_DIM = 32
MLA_V_DIM = 64
MLA_Q_RANK = 256
MLA_KV_RANK = 128
MLA_QK_DIM = MLA_NOPE_DIM + MLA_ROPE_DIM
GDN_HEADS = 4
GDN_HEAD_DIM = 128
GDN_WIDTH = GDN_HEADS * GDN_HEAD_DIM
GDN_CONV = 5
GDN_CHUNK = 64
N_DIR = 2
ROPE_THETA = 10000.0
NORM_EPS = 1e-6
LOG2_E = 1.4426950408889634

LANES = 128
BF16_SUBLANES = 16
VMEM_LIMIT_BYTES = 56 * 1024 * 1024

TOKEN_TILE = 256
GDN_BLOCK = 256
ATTN_KEY_TILE = 256
NEG_BIG = -1e30
PAD_ROWS = 16


def _sigmoid(x):
    return 1.0 / (1.0 + jnp.exp(-x))


def _silu(x):
    return x * _sigmoid(x)


def _dot(a, b):
    return jnp.dot(a, b, preferred_element_type=F32)


def _dot_nt(a, b):
    return lax.dot_general(a, b, (((1,), (1,)), ((), ())), preferred_element_type=F32)


def _dot_tn(a, b):
    return lax.dot_general(a, b, (((0,), (0,)), ((), ())), preferred_element_type=F32)


def _const_spec(shape):
    nd = len(shape)
    return pl.BlockSpec(shape, lambda *_: (0,) * nd, pipeline_mode=pl.Buffered(1))


def _params(*sem):
    return pltpu.CompilerParams(dimension_semantics=sem, vmem_limit_bytes=VMEM_LIMIT_BYTES)


def _ada_kernel(c_ref, w_ref, b_ref, o_ref):
    s = _silu(c_ref[...]).astype(BF16)
    o_ref[0] = _dot(s, w_ref[0].astype(BF16)) + b_ref[0]


def _ada_call(cc, w_ada, b_ada):
    n_layers, d, d6 = w_ada.shape
    tn = 1536
    return pl.pallas_call(
        _ada_kernel,
        grid=(n_layers, d6 // tn),
        in_specs=[
            pl.BlockSpec((PAD_ROWS, d), lambda l, n: (0, 0)),
            pl.BlockSpec((1, d, tn), lambda l, n: (l, 0, n)),
            pl.BlockSpec((1, 1, tn), lambda l, n: (l, 0, n)),
        ],
        out_specs=pl.BlockSpec((1, PAD_ROWS, tn), lambda l, n: (l, 0, n)),
        out_shape=jax.ShapeDtypeStruct((n_layers, PAD_ROWS, d6), F32),
        compiler_params=_params("parallel", "parallel"),
        name="ada",
    )(cc, w_ada, b_ada.reshape(n_layers, 1, d6))


def _rope(x, c, sa, sb):
    return x * c + pltpu.roll(x, LANES - 16, 1) * sa + pltpu.roll(x, 16, 1) * sb


def _proj_kernel(xc_ref, xp_ref, xn_ref, mod_ref, n1g_ref, wqkv_ref, wmain_ref, qag_ref, kvag_ref,
                 wq_ref, wk_ref, wv_ref, ctab_ref, sa_ref, sb_ref, convw_ref, alog_ref, dtb_ref,
                 qt_ref, k_ref, vt_ref, gq_ref, gk_ref, gv_ref, gb_ref, z_ref,
                 hext_ref, pqkv_ref, *, tm, n_ctx_tiles, n_tiles, d_model):
    t = pl.program_id(1)
    halo = BF16_SUBLANES
    mod = mod_ref[...]
    sh1 = mod[:, 0:d_model]
    sc1 = mod[:, d_model:2 * d_model]
    n1g = n1g_ref[...]

    def modulate(x):
        y = x * lax.rsqrt(jnp.mean(x * x, axis=-1, keepdims=True) + NORM_EPS)
        return ((y * n1g) * (1.0 + sc1) + sh1).astype(BF16)

    hext_ref[0:halo, :] = modulate(xp_ref[...])
    hext_ref[halo:halo + tm, :] = modulate(xc_ref[...])
    hext_ref[halo + tm:, :] = modulate(xn_ref[...])

    pqkv_ref[...] = _dot(hext_ref[...], wqkv_ref[...])
    seg_first = jnp.logical_or(t == 0, t == n_ctx_tiles)
    seg_last = jnp.logical_or(t == n_ctx_tiles - 1, t == n_tiles - 1)

    @pl.when(seg_first)
    def _():
        pqkv_ref[0:halo, :] = jnp.zeros((halo, 3 * GDN_WIDTH), F32)

    @pl.when(seg_last)
    def _():
        pqkv_ref[halo + tm:, :] = jnp.zeros((halo, 3 * GDN_WIDTH), F32)

    pm = _dot(hext_ref[halo:halo + tm, :], wmain_ref[...])
    o_kva = MLA_Q_RANK
    o_kr = o_kva + MLA_KV_RANK
    o_z = o_kr + LANES
    o_ab = o_z + GDN_WIDTH
    qa = pm[:, 0:o_kva]
    kva = pm[:, o_kva:o_kr]
    kr = pm[:, o_kr:o_z]
    z_ref[...] = pm[:, o_z:o_ab]
    ab = pm[:, o_ab:o_ab + LANES]

    ctab = ctab_ref[...]
    sa = sa_ref[...]
    sb = sb_ref[...]

    qn = (qa * lax.rsqrt(jnp.mean(qa * qa, axis=-1, keepdims=True) + NORM_EPS) * qag_ref[...]).astype(BF16)
    q = _dot(qn, wq_ref[...])
    scale = MLA_QK_DIM ** -0.5 * LOG2_E
    for h in range(MLA_HEADS):
        qh = q[:, h * LANES:(h + 1) * LANES]
        qt_ref[h] = (_rope(qh, ctab, sa, sb) * scale).T.astype(BF16)

    kvn = (kva * lax.rsqrt(jnp.mean(kva * kva, axis=-1, keepdims=True) + NORM_EPS) * kvag_ref[...]).astype(BF16)
    kk = _dot(kvn, wk_ref[...])
    kpe = _rope(kr, ctab, sa, sb)
    for h in range(MLA_HEADS):
        k_ref[:, h * LANES:(h + 1) * LANES] = (kk[:, h * LANES:(h + 1) * LANES] + kpe).astype(BF16)
    vv = _dot(kvn, wv_ref[...])
    lane = lax.broadcasted_iota(jnp.int32, (tm, LANES), 1)
    for h in range(MLA_HEADS):
        vh = jnp.where(lane == MLA_V_DIM, 1.0, vv[:, h * LANES:(h + 1) * LANES])
        vt_ref[h] = vh.T.astype(BF16)

    cw = convw_ref[...]
    base = halo - GDN_CONV // 2
    for cb in range(3 * GDN_HEADS):
        cs = slice(cb * LANES, (cb + 1) * LANES)
        acc = pqkv_ref[base:base + tm, cs] * cw[0:1, cs]
        for j in range(1, GDN_CONV):
            acc = acc + pqkv_ref[base + j:base + j + tm, cs] * cw[j:j + 1, cs]
        act = _silu(acc)
        grp, hh = divmod(cb, GDN_HEADS)
        hs = slice(hh * LANES, (hh + 1) * LANES)
        if grp < 2:
            act = act * lax.rsqrt(jnp.sum(act * act, axis=-1, keepdims=True) + NORM_EPS)
            (gq_ref if grp == 0 else gk_ref)[:, hs] = act
        else:
            gv_ref[:, hs] = act

    x = ab + dtb_ref[...]
    softplus = jnp.maximum(x, 0.0) + jnp.log(1.0 + jnp.exp(-jnp.abs(x)))
    gval = -jnp.exp(alog_ref[...]) * softplus
    lane_ab = lax.broadcasted_iota(jnp.int32, ab.shape, 1)
    gb_ref[...] = jnp.where(lane_ab < N_DIR * GDN_HEADS, gval, _sigmoid(ab))


def _proj_call(xs, mod, layer, wts, tabs, n_ctx_tiles):
    b, ta, d = xs.shape
    tm = TOKEN_TILE
    n_tiles = ta // tm
    halo = BF16_SUBLANES
    hb = tm // halo
    n_halo_blocks = ta // halo
    d6 = 6 * d

    tok = lambda w: pl.BlockSpec((None, tm, w), lambda bi, t: (bi, t, 0))
    in_specs = [
        tok(d),
        pl.BlockSpec((None, halo, d), lambda bi, t: (bi, jnp.maximum(t * hb - 1, 0), 0)),
        pl.BlockSpec((None, halo, d), lambda bi, t: (bi, jnp.minimum((t + 1) * hb, n_halo_blocks - 1), 0)),
        pl.BlockSpec((None, None, None, 1, d6),
                     lambda bi, t: (layer, bi, jnp.where(t < n_ctx_tiles, 1, 0), 0, 0)),
        _const_spec((1, d)),
        _const_spec((d, 3 * GDN_WIDTH)),
        _const_spec(wts["w_main"].shape),
        _const_spec((1, MLA_Q_RANK)),
        _const_spec((1, MLA_KV_RANK)),
        _const_spec((MLA_Q_RANK, MLA_HEADS * LANES)),
        _const_spec((MLA_KV_RANK, MLA_HEADS * LANES)),
        _const_spec((MLA_KV_RANK, MLA_HEADS * LANES)),
        pl.BlockSpec((tm, LANES), lambda bi, t: (t, 0)),
        pl.BlockSpec((tm, LANES), lambda bi, t: (t, 0)),
        pl.BlockSpec((tm, LANES), lambda bi, t: (t, 0)),
        _const_spec((8, 3 * GDN_WIDTH)),
        _const_spec((1, LANES)),
        _const_spec((1, LANES)),
    ]
    head_t = pl.BlockSpec((None, MLA_HEADS, LANES, tm), lambda bi, t: (bi, 0, 0, t))
    out_specs = [
        head_t,
        tok(MLA_HEADS * LANES),
        head_t,
        tok(GDN_WIDTH), tok(GDN_WIDTH), tok(GDN_WIDTH),
        tok(LANES),
        tok(GDN_WIDTH),
    ]
    out_shape = [
        jax.ShapeDtypeStruct((b, MLA_HEADS, LANES, ta), BF16),
        jax.ShapeDtypeStruct((b, ta, MLA_HEADS * LANES), BF16),
        jax.ShapeDtypeStruct((b, MLA_HEADS, LANES, ta), BF16),
        jax.ShapeDtypeStruct((b, ta, GDN_WIDTH), F32),
        jax.ShapeDtypeStruct((b, ta, GDN_WIDTH), F32),
        jax.ShapeDtypeStruct((b, ta, GDN_WIDTH), F32),
        jax.ShapeDtypeStruct((b, ta, LANES), F32),
        jax.ShapeDtypeStruct((b, ta, GDN_WIDTH), F32),
    ]
    kern = functools.partial(_proj_kernel, tm=tm, n_ctx_tiles=n_ctx_tiles, n_tiles=n_tiles, d_model=d)
    return pl.pallas_call(
        kern,
        grid=(b, n_tiles),
        in_specs=in_specs,
        out_specs=out_specs,
        out_shape=out_shape,
        scratch_shapes=[pltpu.VMEM((tm + 2 * halo, d), BF16),
                        pltpu.VMEM((tm + 2 * halo, 3 * GDN_WIDTH), F32)],
        compiler_params=_params("parallel", "parallel"),
        name="proj",
    )(xs, xs, xs, mod, wts["norm1_g"], wts["w_qkv"], wts["w_main"], wts["q_a_g"], wts["kv_a_g"],
      wts["w_q"], wts["w_k"], wts["w_v"], tabs[0], tabs[1], tabs[2], wts["conv_w"], wts["a_log"], wts["dt_bias"])


def _attn_kernel(qt_ref, k_ref, vt_ref, o_ref, s_ref, *, n_ctx_q, tc, ta, first_q):
    qi = pl.program_id(1) + first_q
    kt = ATTN_KEY_TILE

    def score_tile(h, t, slot):
        s = _dot(k_ref[t * kt:(t + 1) * kt, h * LANES:(h + 1) * LANES], qt_ref[h])
        s_ref[slot, t * kt:(t + 1) * kt, :] = s
        return jnp.max(s, axis=0, keepdims=True)

    def pv_tile(h, t, slot, m):
        p = jnp.exp2(s_ref[slot, t * kt:(t + 1) * kt, :] - m).astype(BF16)
        return _dot(vt_ref[h, :, t * kt:(t + 1) * kt], p)

    def run(nk):
        nt = nk // kt
        outs = []
        m = functools.reduce(jnp.maximum, [score_tile(0, t, 0) for t in range(nt)])
        for h in range(MLA_HEADS):
            slot = h % 2
            acc = None
            m_parts = []
            for t in range(nt):
                if h + 1 < MLA_HEADS:
                    m_parts.append(score_tile(h + 1, t, 1 - slot))
                o = pv_tile(h, t, slot, m)
                acc = o if acc is None else acc + o
            outs.append(acc[0:MLA_V_DIM] / acc[MLA_V_DIM:MLA_V_DIM + 1])
            if m_parts:
                m = functools.reduce(jnp.maximum, m_parts)
        o_ref[...] = jnp.concatenate(outs, axis=0).T.astype(BF16)

    if first_q < n_ctx_q:
        @pl.when(qi < n_ctx_q)
        def _():
            run(tc)

    @pl.when(qi >= n_ctx_q)
    def _():
        run(ta)


def _attn_call(qt, k, vt, tc, last):
    b, ta, _ = k.shape
    tq = TOKEN_TILE
    n_ctx_q = tc // tq
    first_q = n_ctx_q if last else 0
    nq = ta // tq - first_q
    kern = functools.partial(_attn_kernel, n_ctx_q=n_ctx_q, tc=tc, ta=ta, first_q=first_q)
    return pl.pallas_call(
        kern,
        grid=(b, nq),
        in_specs=[
            pl.BlockSpec((None, MLA_HEADS, LANES, tq), lambda bi, qi: (bi, 0, 0, qi + first_q)),
            pl.BlockSpec((None, ta, MLA_HEADS * LANES), lambda bi, qi: (bi, 0, 0)),
            pl.BlockSpec((None, MLA_HEADS, LANES, ta), lambda bi, qi: (bi, 0, 0, 0)),
        ],
        out_specs=pl.BlockSpec((None, tq, MLA_HEADS * MLA_V_DIM), lambda bi, qi: (bi, qi, 0)),
        out_shape=jax.ShapeDtypeStruct((b, nq * tq, MLA_HEADS * MLA_V_DIM), BF16),
        scratch_shapes=[pltpu.VMEM((2, ta, tq), F32)],
        compiler_params=_params("parallel", "arbitrary"),
        name="attn",
    )(qt, k, vt)


def _gdn_kernel(qf_ref, kf_ref, vf_ref, gf_ref, qb_ref, kb_ref, vb_ref, gbk_ref, cm_ref,
                of_ref, ob_ref, s_ref):
    j = pl.program_id(1)
    blk = GDN_BLOCK
    c = GDN_CHUNK
    n_chunks = blk // c

    @pl.when(j == 0)
    def _():
        s_ref[...] = jnp.zeros(s_ref.shape, F32)

    row = lax.broadcasted_iota(jnp.int32, (blk, blk), 0)
    col = lax.broadcasted_iota(jnp.int32, (blk, blk), 1)
    same = (row // c) == (col // c)
    eye = jnp.where(row == col, 1.0, 0.0)
    pair_masks = []
    size = 1
    while size < c:
        pair_masks.append(jnp.logical_and((row // (2 * size)) == (col // (2 * size)),
                                          (row // size) != (col // size)))
        size *= 2
    scale = GDN_HEAD_DIM ** -0.5

    chains = []
    for d, (q_ref, k_ref, v_ref, g_ref, o_ref) in enumerate(
            ((qf_ref, kf_ref, vf_ref, gf_ref, of_ref), (qb_ref, kb_ref, vb_ref, gbk_ref, ob_ref))):
        incl = jnp.logical_and(same, (row >= col) if d == 0 else (row <= col))
        strict = jnp.logical_and(same, (row > col) if d == 0 else (row < col))
        gb = g_ref[...]
        g1 = gb.astype(BF16)
        r1 = gb - g1.astype(F32)
        g2 = r1.astype(BF16)
        g3 = (r1 - g2.astype(F32)).astype(BF16)
        cmd = cm_ref[d]
        gc = _dot(cmd, g1) + _dot(cmd, g2) + _dot(cmd, g3)
        last = c - 1 if d == 0 else 0
        gtot = jnp.concatenate(
            [jnp.broadcast_to(gc[ci * c + last:ci * c + last + 1, :], (c, LANES)) for ci in range(n_chunks)],
            axis=0)
        gct = gc.T
        for h in range(GDN_HEADS):
            ln = d * GDN_HEADS + h
            hs = slice(h * LANES, (h + 1) * LANES)
            gcol = gc[:, ln:ln + 1]
            gl = gtot[:, ln:ln + 1]
            beta = gb[:, N_DIR * GDN_HEADS + ln:N_DIR * GDN_HEADS + ln + 1]
            decay = jnp.exp(jnp.where(incl, gcol - gct[ln:ln + 1, :], NEG_BIG))
            k = k_ref[:, hs]
            q = q_ref[:, hs] * scale
            kbeta = k * beta
            egc = jnp.exp(gcol)
            chains.append(dict(
                d=d, hs=hs, o_ref=o_ref, idx=ln, strict=strict, decay=decay, kb16=k.astype(BF16),
                kbeta16=kbeta.astype(BF16), q16=q.astype(BF16),
                rhs16=jnp.concatenate([v_ref[:, hs] * beta, kbeta * egc], axis=1).astype(BF16),
                qd=q * egc, kt=k * jnp.exp(gl - gcol), glast=jnp.exp(gl)))

    for ch in chains:
        gram = _dot_nt(ch["kbeta16"], ch["kb16"])
        ch["lower"] = jnp.where(ch["strict"], gram * ch["decay"], 0.0)
        ch["t"] = eye - jnp.where(pair_masks[0], ch["lower"], 0.0)
    for pm in pair_masks[1:]:
        for ch in chains:
            ch["t16"] = ch["t"].astype(BF16)
            ch["p16"] = _dot(jnp.where(pm, ch["lower"], 0.0).astype(BF16), ch["t16"]).astype(BF16)
        for ch in chains:
            ch["t"] = ch["t"] - _dot(ch["t16"], ch["p16"])
    for ch in chains:
        x = _dot(ch["t"].astype(BF16), ch["rhs16"])
        ch["u"] = x[:, 0:LANES]
        ch["w"] = x[:, LANES:2 * LANES]
        ch["qk"] = _dot_nt(ch["q16"], ch["kb16"]) * ch["decay"]

    for step in range(n_chunks):
        for ch in chains:
            ci = step if ch["d"] == 0 else n_chunks - 1 - step
            r = slice(ci * c, (ci + 1) * c)
            s = s_ref[ch["idx"]]
            ws = _dot(jnp.concatenate([ch["w"][r], ch["qd"][r]], axis=0).astype(BF16), s.astype(BF16))
            vnew = ch["u"][r] - ws[0:c]
            vn16 = vnew.astype(BF16)
            o = ws[c:2 * c] + _dot(ch["qk"][r, ci * c:(ci + 1) * c].astype(BF16), vn16)
            glast = ch["glast"][ci * c:ci * c + 1, :]
            s_ref[ch["idx"]] = s * glast + _dot_tn(ch["kt"][r].astype(BF16), vn16)
            ch["o_ref"][r, ch["hs"]] = o


def _gdn_call(gq, gk, gv, gb, cm, n_ctx_blocks):
    b, ta, w = gq.shape
    blk = GDN_BLOCK
    nb = ta // blk

    def bwd_block(j):
        return jnp.where(j < n_ctx_blocks, n_ctx_blocks - 1 - j, nb - 1 - (j - n_ctx_blocks))

    f_spec = lambda wd: pl.BlockSpec((None, blk, wd), lambda bi, j: (bi, j, 0))
    b_spec = lambda wd: pl.BlockSpec((None, blk, wd), lambda bi, j: (bi, bwd_block(j), 0))
    return pl.pallas_call(
        _gdn_kernel,
        grid=(b, nb),
        in_specs=[f_spec(w), f_spec(w), f_spec(w), f_spec(LANES),
                  b_spec(w), b_spec(w), b_spec(w), b_spec(LANES),
                  _const_spec(cm.shape)],
        out_specs=[f_spec(w), b_spec(w)],
        out_shape=[jax.ShapeDtypeStruct((b, ta, w), F32), jax.ShapeDtypeStruct((b, ta, w), F32)],
        scratch_shapes=[pltpu.VMEM((N_DIR * GDN_HEADS, GDN_HEAD_DIM, GDN_HEAD_DIM), F32)],
        compiler_params=_params("parallel", "arbitrary"),
        name="gdn",
    )(gq, gk, gv, gb, gq, gk, gv, gb, cm)


def _out_kernel(x_ref, attn_ref, of_ref, ob_ref, z_ref, mod_ref, gng_ref, wout_ref, n2g_ref,
                w1_ref, w2_ref, fng_ref, o_ref, *, d_model, final):
    mod = mod_ref[...]
    g1 = mod[:, 2 * d_model:3 * d_model]
    sh2 = mod[:, 3 * d_model:4 * d_model]
    sc2 = mod[:, 4 * d_model:5 * d_model]
    g2 = mod[:, 5 * d_model:6 * d_model]

    o = of_ref[...] + ob_ref[...]
    z = z_ref[...]
    gng = gng_ref[...]
    parts = [attn_ref[...]]
    for h in range(GDN_HEADS):
        hs = slice(h * LANES, (h + 1) * LANES)
        oh = o[:, hs]
        y = oh * lax.rsqrt(jnp.mean(oh * oh, axis=-1, keepdims=True) + NORM_EPS) * gng
        parts.append((y * _silu(z[:, hs])).astype(BF16))
    mix = jnp.concatenate(parts, axis=1)
    x1 = x_ref[...] + g1 * _dot(mix, wout_ref[...])

    y2 = x1 * lax.rsqrt(jnp.mean(x1 * x1, axis=-1, keepdims=True) + NORM_EPS)
    h2 = ((y2 * n2g_ref[...]) * (1.0 + sc2) + sh2).astype(BF16)
    ff = jnp.maximum(_dot(h2, w1_ref[...]), 0.0)
    ff = (ff * ff).astype(BF16)
    x2 = x1 + g2 * _dot(ff, w2_ref[...])
    if final:
        x2 = x2 * lax.rsqrt(jnp.mean(x2 * x2, axis=-1, keepdims=True) + NORM_EPS) * fng_ref[...]
    o_ref[...] = x2


def _out_call(xs, attn, o_f, o_b, z, mod, layer, wts, fng, n_ctx_tiles, final):
    b, ta, d = xs.shape
    tm = TOKEN_TILE
    d6 = 6 * d
    first = n_ctx_tiles if final else 0
    n_tiles = ta // tm - first
    tok = lambda w: pl.BlockSpec((None, tm, w), lambda bi, t: (bi, t + first, 0))
    kern = functools.partial(_out_kernel, d_model=d, final=final)
    out_rows = n_tiles * tm
    return pl.pallas_call(
        kern,
        grid=(b, n_tiles),
        in_specs=[
            tok(d),
            pl.BlockSpec((None, tm, MLA_HEADS * MLA_V_DIM), lambda bi, t: (bi, t, 0)),
            tok(GDN_WIDTH), tok(GDN_WIDTH), tok(GDN_WIDTH),
            pl.BlockSpec((None, None, None, 1, d6),
                         lambda bi, t: (layer, bi, jnp.where(t + first < n_ctx_tiles, 1, 0), 0, 0)),
            _const_spec((1, GDN_HEAD_DIM)),
            _const_spec(wts["w_out"].shape),
            _const_spec((1, d)),
            _const_spec(wts["w_ff1"].shape),
            _const_spec(wts["w_ff2"].shape),
            _const_spec((1, d)),
        ],
        out_specs=pl.BlockSpec((None, tm, d), lambda bi, t: (bi, t, 0)),
        out_shape=jax.ShapeDtypeStruct((b, out_rows, d), F32),
        compiler_params=_params("parallel", "parallel"),
        name="out_final" if final else "out",
    )(xs, attn, o_f, o_b, z, mod, wts["gdn_norm_g"], wts["w_out"], wts["norm2_g"],
      wts["w_ff1"], wts["w_ff2"], fng)


def _rope_tables(t_lat, t_ctx):
    rows = t_lat // GRID_W
    row = jnp.broadcast_to(jnp.arange(rows)[:, None], (rows, GRID_W)).reshape(-1).astype(F32)
    col = jnp.broadcast_to(jnp.arange(GRID_W)[None, :], (rows, GRID_W)).reshape(-1).astype(F32)
    axis_pairs = MLA_ROPE_DIM // 4
    inv_freq = ROPE_THETA ** (-jnp.arange(axis_pairs, dtype=F32) / axis_pairs)
    ang = jnp.concatenate([row[:, None] * inv_freq, col[:, None] * inv_freq], axis=-1)
    cos, sin = jnp.cos(ang), jnp.sin(ang)
    half = MLA_ROPE_DIM // 2
    lo, mid, hi = MLA_NOPE_DIM, MLA_NOPE_DIM + half, MLA_NOPE_DIM + 2 * half
    ta = t_ctx + t_lat
    ctab = jnp.ones((ta, LANES), F32).at[t_ctx:, lo:mid].set(cos).at[t_ctx:, mid:hi].set(cos)
    sa = jnp.zeros((ta, LANES), F32).at[t_ctx:, lo:mid].set(-sin)
    sb = jnp.zeros((ta, LANES), F32).at[t_ctx:, mid:hi].set(sin)
    return ctab, sa, sb


def _chunk_sum_matrices():
    i = jnp.arange(GDN_BLOCK)
    same = (i[:, None] // GDN_CHUNK) == (i[None, :] // GDN_CHUNK)
    lower = same & (i[:, None] >= i[None, :])
    upper = same & (i[:, None] <= i[None, :])
    return jnp.stack([lower, upper], axis=0).astype(BF16)


def _pad_heads(w, used):
    k = w.shape[0]
    w = w.reshape(k, MLA_HEADS, used)
    return jnp.pad(w, ((0, 0), (0, 0), (0, LANES - used))).reshape(k, MLA_HEADS * LANES)


def _layer_weights(i, norm1_g, w_in, q_a_g, w_q_b, kv_a_g, w_kv_b, conv_w, a_log, dt_bias, gdn_norm_g,
                   w_out, norm2_g, w_ff1, w_ff2):
    d = w_in.shape[1]
    wi = w_in[i]
    o = 0
    cols = []
    for sz in (MLA_Q_RANK, MLA_KV_RANK, MLA_ROPE_DIM, 3 * GDN_WIDTH, GDN_WIDTH,
               N_DIR * GDN_HEADS, N_DIR * GDN_HEADS):
        cols.append(wi[:, o:o + sz])
        o += sz
    w_qa, w_kva, w_kr, w_qkv, w_z, w_a, w_b = cols
    w_kr_pad = jnp.zeros((d, LANES), F32).at[:, MLA_NOPE_DIM:MLA_QK_DIM].set(w_kr)
    nab = N_DIR * GDN_HEADS
    w_ab_pad = jnp.zeros((d, LANES), F32).at[:, 0:nab].set(w_a).at[:, nab:2 * nab].set(w_b)
    w_main = jnp.concatenate([w_qa, w_kva, w_kr_pad, w_z, w_ab_pad], axis=1)
    kv = w_kv_b[i].reshape(MLA_KV_RANK, MLA_HEADS, MLA_NOPE_DIM + MLA_V_DIM)
    w_k = _pad_heads(kv[:, :, :MLA_NOPE_DIM].reshape(MLA_KV_RANK, -1), MLA_NOPE_DIM)
    w_v = _pad_heads(kv[:, :, MLA_NOPE_DIM:].reshape(MLA_KV_RANK, -1), MLA_V_DIM)
    row = lambda v: jnp.zeros((1, LANES), F32).at[0, 0:nab].set(v.reshape(-1))
    return dict(
        norm1_g=norm1_g[i][None, :],
        w_qkv=w_qkv.astype(BF16),
        w_main=w_main.astype(BF16),
        q_a_g=q_a_g[i][None, :],
        kv_a_g=kv_a_g[i][None, :],
        w_q=_pad_heads(w_q_b[i], MLA_QK_DIM).astype(BF16),
        w_k=w_k.astype(BF16),
        w_v=w_v.astype(BF16),
        conv_w=jnp.pad(conv_w[i], ((0, 8 - GDN_CONV), (0, 0))),
        a_log=row(a_log[i]),
        dt_bias=row(dt_bias[i]),
        gdn_norm_g=gdn_norm_g[i][None, :],
        w_out=w_out[i].astype(BF16),
        norm2_g=norm2_g[i][None, :],
        w_ff1=w_ff1[i].astype(BF16),
        w_ff2=w_ff2[i].astype(BF16),
    )


def kernel(x, c, ctx, c_ctx, w_ada, b_ada, norm1_g, w_in, q_a_g, w_q_b, kv_a_g, w_kv_b, conv_w, a_log,
           dt_bias, gdn_norm_g, w_out, norm2_g, w_ff1, w_ff2, final_norm_g):
    b, t_lat, d = x.shape
    t_ctx = ctx.shape[1]
    depth = w_ada.shape[0]
    assert t_ctx % TOKEN_TILE == 0 and t_lat % TOKEN_TILE == 0 and t_lat % GRID_W == 0
    assert b + 1 <= PAD_ROWS
    n_ctx_tiles = t_ctx // TOKEN_TILE

    xs = jnp.concatenate([ctx, x], axis=1)
    cc = jnp.zeros((PAD_ROWS, d), F32).at[0:b].set(c).at[b].set(c_ctx)
    mods = _ada_call(cc, w_ada, b_ada)
    mod = jnp.stack([mods[:, 0:b], jnp.broadcast_to(mods[:, b:b + 1], (depth, b, 6 * d))], axis=2)
    mod = mod[:, :, :, None, :]
    tabs = _rope_tables(t_lat, t_ctx)
    cm = _chunk_sum_matrices()
    fng = final_norm_g[None, :]

    for i in range(depth):
        last = i == depth - 1
        wts = _layer_weights(i, norm1_g, w_in, q_a_g, w_q_b, kv_a_g, w_kv_b, conv_w, a_log, dt_bias,
                             gdn_norm_g, w_out, norm2_g, w_ff1, w_ff2)
        qt, k, vt, gq, gk, gv, gb, z = _proj_call(xs, mod, i, wts, tabs, n_ctx_tiles)
        attn = _attn_call(qt, k, vt, t_ctx, last)
        o_f, o_b = _gdn_call(gq, gk, gv, gb, cm, t_ctx // GDN_BLOCK)
        xs = _out_call(xs, attn, o_f, o_b, z, mod, i, wts, fng, n_ctx_tiles, last)
    return xs
```

```python
import functools

import jax
import jax.numpy as jnp
from jax import lax
from jax.experimental import pallas as pl
from jax.experimental.pallas import tpu as pltpu

F32 = jnp.float32
BF16 = jnp.bfloat16

GRID_W = 64
MLA_HEADS = 8
MLA_NOPE_DIM = 64
MLA_ROPE_DIM = 32
MLA_V_DIM = 64
MLA_Q_RANK = 256
MLA_KV_RANK = 128
MLA_QK_DIM = MLA_NOPE_DIM + MLA_ROPE_DIM
GDN_HEADS = 4
GDN_HEAD_DIM = 128
GDN_WIDTH = GDN_HEADS * GDN_HEAD_DIM
GDN_CONV = 5
GDN_CHUNK = 64
N_DIR = 2
ROPE_THETA = 10000.0
NORM_EPS = 1e-6
LOG2_E = 1.4426950408889634

LANES = 128
BF16_SUBLANES = 16
VMEM_LIMIT_BYTES = 56 * 1024 * 1024

TOKEN_TILE = 256
GDN_BLOCK = 256
ATTN_KEY_TILE = 256
NEG_BIG = -1e30
PAD_ROWS = 16


def _sigmoid(x):
    return 1.0 / (1.0 + jnp.exp(-x))


def _silu(x):
    return x * _sigmoid(x)


def _dot(a, b):
    return jnp.dot(a, b, preferred_element_type=F32)


def _dot_nt(a, b):
    return lax.dot_general(a, b, (((1,), (1,)), ((), ())), preferred_element_type=F32)


def _dot_tn(a, b):
    return lax.dot_general(a, b, (((0,), (0,)), ((), ())), preferred_element_type=F32)


def _const_spec(shape):
    nd = len(shape)
    return pl.BlockSpec(shape, lambda *_: (0,) * nd, pipeline_mode=pl.Buffered(1))


def _params(*sem):
    return pltpu.CompilerParams(dimension_semantics=sem, vmem_limit_bytes=VMEM_LIMIT_BYTES)


def _ada_kernel(c_ref, w_ref, b_ref, o_ref):
    s = _silu(c_ref[...]).astype(BF16)
    o_ref[0] = _dot(s, w_ref[0].astype(BF16)) + b_ref[0]


def _ada_call(cc, w_ada, b_ada):
    n_layers, d, d6 = w_ada.shape
    tn = 1536
    return pl.pallas_call(
        _ada_kernel,
        grid=(n_layers, d6 // tn),
        in_specs=[
            pl.BlockSpec((PAD_ROWS, d), lambda l, n: (0, 0)),
            pl.BlockSpec((1, d, tn), lambda l, n: (l, 0, n)),
            pl.BlockSpec((1, 1, tn), lambda l, n: (l, 0, n)),
        ],
        out_specs=pl.BlockSpec((1, PAD_ROWS, tn), lambda l, n: (l, 0, n)),
        out_shape=jax.ShapeDtypeStruct((n_layers, PAD_ROWS, d6), F32),
        compiler_params=_params("parallel", "parallel"),
        name="ada",
    )(cc, w_ada, b_ada.reshape(n_layers, 1, d6))


def _rope(x, c, sa, sb):
    return x * c + pltpu.roll(x, LANES - 16, 1) * sa + pltpu.roll(x, 16, 1) * sb


def _proj_kernel(xc_ref, xp_ref, xn_ref, mod_ref, n1g_ref, wqkv_ref, wmain_ref, qag_ref, kvag_ref,
                 wq_ref, wk_ref, wv_ref, ctab_ref, sa_ref, sb_ref, convw_ref, alog_ref, dtb_ref,
                 qt_ref, k_ref, vt_ref, gq_ref, gk_ref, gv_ref, gb_ref, z_ref,
                 hext_ref, pqkv_ref, *, tm, n_ctx_tiles, n_tiles, d_model):
    t = pl.program_id(1)
    halo = BF16_SUBLANES
    mod = mod_ref[...]
    sh1 = mod[:, 0:d_model]
    sc1 = mod[:, d_model:2 * d_model]
    n1g = n1g_ref[...]

    def modulate(x):
        y = x * lax.rsqrt(jnp.mean(x * x, axis=-1, keepdims=True) + NORM_EPS)
        return ((y * n1g) * (1.0 + sc1) + sh1).astype(BF16)

    hext_ref[0:halo, :] = modulate(xp_ref[...])
    hext_ref[halo:halo + tm, :] = modulate(xc_ref[...])
    hext_ref[halo + tm:, :] = modulate(xn_ref[...])

    pqkv_ref[...] = _dot(hext_ref[...], wqkv_ref[...])
    seg_first = jnp.logical_or(t == 0, t == n_ctx_tiles)
    seg_last = jnp.logical_or(t == n_ctx_tiles - 1, t == n_tiles - 1)

    @pl.when(seg_first)
    def _():
        pqkv_ref[0:halo, :] = jnp.zeros((halo, 3 * GDN_WIDTH), F32)

    @pl.when(seg_last)
    def _():
        pqkv_ref[halo + tm:, :] = jnp.zeros((halo, 3 * GDN_WIDTH), F32)

    pm = _dot(hext_ref[halo:halo + tm, :], wmain_ref[...])
    o_kva = MLA_Q_RANK
    o_kr = o_kva + MLA_KV_RANK
    o_z = o_kr + LANES
    o_ab = o_z + GDN_WIDTH
    qa = pm[:, 0:o_kva]
    kva = pm[:, o_kva:o_kr]
    kr = pm[:, o_kr:o_z]
    z_ref[...] = pm[:, o_z:o_ab]
    ab = pm[:, o_ab:o_ab + LANES]

    ctab = ctab_ref[...]
    sa = sa_ref[...]
    sb = sb_ref[...]

    qn = (qa * lax.rsqrt(jnp.mean(qa * qa, axis=-1, keepdims=True) + NORM_EPS) * qag_ref[...]).astype(BF16)
    q = _dot(qn, wq_ref[...])
    scale = MLA_QK_DIM ** -0.5 * LOG2_E
    for h in range(MLA_HEADS):
        qh = q[:, h * LANES:(h + 1) * LANES]
        qt_ref[h] = (_rope(qh, ctab, sa, sb) * scale).T.astype(BF16)

    kvn = (kva * lax.rsqrt(jnp.mean(kva * kva, axis=-1, keepdims=True) + NORM_EPS) * kvag_ref[...]).astype(BF16)
    kk = _dot(kvn, wk_ref[...])
    kpe = _rope(kr, ctab, sa, sb)
    for h in range(MLA_HEADS):
        k_ref[:, h * LANES:(h + 1) * LANES] = (kk[:, h * LANES:(h + 1) * LANES] + kpe).astype(BF16)
    vv = _dot(kvn, wv_ref[...])
    lane = lax.broadcasted_iota(jnp.int32, (tm, LANES), 1)
    for h in range(MLA_HEADS):
        vh = jnp.where(lane == MLA_V_DIM, 1.0, vv[:, h * LANES:(h + 1) * LANES])
        vt_ref[h] = vh.T.astype(BF16)

    cw = convw_ref[...]
    base = halo - GDN_CONV // 2
    for cb in range(3 * GDN_HEADS):
        cs = slice(cb * LANES, (cb + 1) * LANES)
        acc = pqkv_ref[base:base + tm, cs] * cw[0:1, cs]
        for j in range(1, GDN_CONV):
            acc = acc + pqkv_ref[base + j:base + j + tm, cs] * cw[j:j + 1, cs]
        act = _silu(acc)
        grp, hh = divmod(cb, GDN_HEADS)
        hs = slice(hh * LANES, (hh + 1) * LANES)
        if grp < 2:
            act = act * lax.rsqrt(jnp.sum(act * act, axis=-1, keepdims=True) + NORM_EPS)
            (gq_ref if grp == 0 else gk_ref)[:, hs] = act
        else:
            gv_ref[:, hs] = act

    x = ab + dtb_ref[...]
    softplus = jnp.maximum(x, 0.0) + jnp.log(1.0 + jnp.exp(-jnp.abs(x)))
    gval = -jnp.exp(alog_ref[...]) * softplus
    lane_ab = lax.broadcasted_iota(jnp.int32, ab.shape, 1)
    gb_ref[...] = jnp.where(lane_ab < N_DIR * GDN_HEADS, gval, _sigmoid(ab))


def _proj_call(xs, mod, layer, wts, tabs, n_ctx_tiles):
    b, ta, d = xs.shape
    tm = TOKEN_TILE
    n_tiles = ta // tm
    halo = BF16_SUBLANES
    hb = tm // halo
    n_halo_blocks = ta // halo
    d6 = 6 * d

    tok = lambda w: pl.BlockSpec((None, tm, w), lambda bi, t: (bi, t, 0))
    in_specs = [
        tok(d),
        pl.BlockSpec((None, halo, d), lambda bi, t: (bi, jnp.maximum(t * hb - 1, 0), 0)),
        pl.BlockSpec((None, halo, d), lambda bi, t: (bi, jnp.minimum((t + 1) * hb, n_halo_blocks - 1), 0)),
        pl.BlockSpec((None, None, None, 1, d6),
                     lambda bi, t: (layer, bi, jnp.where(t < n_ctx_tiles, 1, 0), 0, 0)),
        _const_spec((1, d)),
        _const_spec((d, 3 * GDN_WIDTH)),
        _const_spec(wts["w_main"].shape),
        _const_spec((1, MLA_Q_RANK)),
        _const_spec((1, MLA_KV_RANK)),
        _const_spec((MLA_Q_RANK, MLA_HEADS * LANES)),
        _const_spec((MLA_KV_RANK, MLA_HEADS * LANES)),
        _const_spec((MLA_KV_RANK, MLA_HEADS * LANES)),
        pl.BlockSpec((tm, LANES), lambda bi, t: (t, 0)),
        pl.BlockSpec((tm, LANES), lambda bi, t: (t, 0)),
        pl.BlockSpec((tm, LANES), lambda bi, t: (t, 0)),
        _const_spec((8, 3 * GDN_WIDTH)),
        _const_spec((1, LANES)),
        _const_spec((1, LANES)),
    ]
    head_t = pl.BlockSpec((None, MLA_HEADS, LANES, tm), lambda bi, t: (bi, 0, 0, t))
    out_specs = [
        head_t,
        tok(MLA_HEADS * LANES),
        head_t,
        tok(GDN_WIDTH), tok(GDN_WIDTH), tok(GDN_WIDTH),
        tok(LANES),
        tok(GDN_WIDTH),
    ]
    out_shape = [
        jax.ShapeDtypeStruct((b, MLA_HEADS, LANES, ta), BF16),
        jax.ShapeDtypeStruct((b, ta, MLA_HEADS * LANES), BF16),
        jax.ShapeDtypeStruct((b, MLA_HEADS, LANES, ta), BF16),
        jax.ShapeDtypeStruct((b, ta, GDN_WIDTH), F32),
        jax.ShapeDtypeStruct((b, ta, GDN_WIDTH), F32),
        jax.ShapeDtypeStruct((b, ta, GDN_WIDTH), F32),
        jax.ShapeDtypeStruct((b, ta, LANES), F32),
        jax.ShapeDtypeStruct((b, ta, GDN_WIDTH), F32),
    ]
    kern = functools.partial(_proj_kernel, tm=tm, n_ctx_tiles=n_ctx_tiles, n_tiles=n_tiles, d_model=d)
    return pl.pallas_call(
        kern,
        grid=(b, n_tiles),
        in_specs=in_specs,
        out_specs=out_specs,
        out_shape=out_shape,
        scratch_shapes=[pltpu.VMEM((tm + 2 * halo, d), BF16),
                        pltpu.VMEM((tm + 2 * halo, 3 * GDN_WIDTH), F32)],
        compiler_params=_params("parallel", "parallel"),
        name="proj",
    )(xs, xs, xs, mod, wts["norm1_g"], wts["w_qkv"], wts["w_main"], wts["q_a_g"], wts["kv_a_g"],
      wts["w_q"], wts["w_k"], wts["w_v"], tabs[0], tabs[1], tabs[2], wts["conv_w"], wts["a_log"], wts["dt_bias"])


def _attn_kernel(qt_ref, k_ref, vt_ref, o_ref, s_ref, *, n_ctx_q, tc, ta):
    qi = pl.program_id(1)
    kt = ATTN_KEY_TILE

    def score_tile(h, t, slot):
        s = _dot(k_ref[t * kt:(t + 1) * kt, h * LANES:(h + 1) * LANES], qt_ref[h])
        s_ref[slot, t * kt:(t + 1) * kt, :] = s
        return jnp.max(s, axis=0, keepdims=True)

    def pv_tile(h, t, slot, m):
        p = jnp.exp2(s_ref[slot, t * kt:(t + 1) * kt, :] - m).astype(BF16)
        return _dot(vt_ref[h, :, t * kt:(t + 1) * kt], p)

    def run(nk):
        nt = nk // kt
        outs = []
        m = functools.reduce(jnp.maximum, [score_tile(0, t, 0) for t in range(nt)])
        for h in range(MLA_HEADS):
            slot = h % 2
            acc = None
            m_parts = []
            for t in range(nt):
                if h + 1 < MLA_HEADS:
                    m_parts.append(score_tile(h + 1, t, 1 - slot))
                o = pv_tile(h, t, slot, m)
                acc = o if acc is None else acc + o
            outs.append(acc[0:MLA_V_DIM] / acc[MLA_V_DIM:MLA_V_DIM + 1])
            if m_parts:
                m = functools.reduce(jnp.maximum, m_parts)
        o_ref[...] = jnp.concatenate(outs, axis=0).T.astype(BF16)

    @pl.when(qi < n_ctx_q)
    def _():
        run(tc)

    @pl.when(qi >= n_ctx_q)
    def _():
        run(ta)


def _attn_call(qt, k, vt, tc):
    b, ta, _ = k.shape
    tq = TOKEN_TILE
    n_ctx_q = tc // tq
    nq = ta // tq
    kern = functools.partial(_attn_kernel, n_ctx_q=n_ctx_q, tc=tc, ta=ta)
    return pl.pallas_call(
        kern,
        grid=(b, nq),
        in_specs=[
            pl.BlockSpec((None, MLA_HEADS, LANES, tq), lambda bi, qi: (bi, 0, 0, qi)),
            pl.BlockSpec((None, ta, MLA_HEADS * LANES), lambda bi, qi: (bi, 0, 0)),
            pl.BlockSpec((None, MLA_HEADS, LANES, ta), lambda bi, qi: (bi, 0, 0, 0)),
        ],
        out_specs=pl.BlockSpec((None, tq, MLA_HEADS * MLA_V_DIM), lambda bi, qi: (bi, qi, 0)),
        out_shape=jax.ShapeDtypeStruct((b, nq * tq, MLA_HEADS * MLA_V_DIM), BF16),
        scratch_shapes=[pltpu.VMEM((2, ta, tq), F32)],
        compiler_params=_params("parallel", "arbitrary"),
        name="attn",
    )(qt, k, vt)


def _gdn_kernel(qf_ref, kf_ref, vf_ref, gf_ref, qb_ref, kb_ref, vb_ref, gbk_ref, cm_ref,
                of_ref, ob_ref, s_ref):
    j = pl.program_id(1)
    blk = GDN_BLOCK
    c = GDN_CHUNK
    n_chunks = blk // c

    @pl.when(j == 0)
    def _():
        s_ref[...] = jnp.zeros(s_ref.shape, F32)

    row = lax.broadcasted_iota(jnp.int32, (blk, blk), 0)
    col = lax.broadcasted_iota(jnp.int32, (blk, blk), 1)
    same = (row // c) == (col // c)
    eye = jnp.where(row == col, 1.0, 0.0)
    pair_masks = []
    size = 1
    while size < c:
        pair_masks.append(jnp.logical_and((row // (2 * size)) == (col // (2 * size)),
                                          (row // size) != (col // size)))
        size *= 2
    scale = GDN_HEAD_DIM ** -0.5

    chains = []
    for d, (q_ref, k_ref, v_ref, g_ref, o_ref) in enumerate(
            ((qf_ref, kf_ref, vf_ref, gf_ref, of_ref), (qb_ref, kb_ref, vb_ref, gbk_ref, ob_ref))):
        incl = jnp.logical_and(same, (row >= col) if d == 0 else (row <= col))
        strict = jnp.logical_and(same, (row > col) if d == 0 else (row < col))
        gb = g_ref[...]
        g1 = gb.astype(BF16)
        r1 = gb - g1.astype(F32)
        g2 = r1.astype(BF16)
        g3 = (r1 - g2.astype(F32)).astype(BF16)
        cmd = cm_ref[d]
        gc = _dot(cmd, g1) + _dot(cmd, g2) + _dot(cmd, g3)
        last = c - 1 if d == 0 else 0
        gtot = jnp.concatenate(
            [jnp.broadcast_to(gc[ci * c + last:ci * c + last + 1, :], (c, LANES)) for ci in range(n_chunks)],
            axis=0)
        gct = gc.T
        for h in range(GDN_HEADS):
            ln = d * GDN_HEADS + h
            hs = slice(h * LANES, (h + 1) * LANES)
            gcol = gc[:, ln:ln + 1]
            gl = gtot[:, ln:ln + 1]
            beta = gb[:, N_DIR * GDN_HEADS + ln:N_DIR * GDN_HEADS + ln + 1]
            decay = jnp.exp(jnp.where(incl, gcol - gct[ln:ln + 1, :], NEG_BIG))
            k = k_ref[:, hs]
            q = q_ref[:, hs] * scale
            kbeta = k * beta
            egc = jnp.exp(gcol)
            chains.append(dict(
                d=d, hs=hs, o_ref=o_ref, idx=ln, strict=strict, decay=decay, kb16=k.astype(BF16),
                kbeta16=kbeta.astype(BF16), q16=q.astype(BF16),
                rhs16=jnp.concatenate([v_ref[:, hs] * beta, kbeta * egc], axis=1).astype(BF16),
                qd=q * egc, kt=k * jnp.exp(gl - gcol), glast=jnp.exp(gl)))

    for ch in chains:
        gram = _dot_nt(ch["kbeta16"], ch["kb16"])
        ch["lower"] = jnp.where(ch["strict"], gram * ch["decay"], 0.0)
        ch["t"] = eye - jnp.where(pair_masks[0], ch["lower"], 0.0)
    for pm in pair_masks[1:]:
        for ch in chains:
            ch["t16"] = ch["t"].astype(BF16)
            ch["p16"] = _dot(jnp.where(pm, ch["lower"], 0.0).astype(BF16), ch["t16"]).astype(BF16)
        for ch in chains:
            ch["t"] = ch["t"] - _dot(ch["t16"], ch["p16"])
    for ch in chains:
        x = _dot(ch["t"].astype(BF16), ch["rhs16"])
        ch["u"] = x[:, 0:LANES]
        ch["w"] = x[:, LANES:2 * LANES]
        ch["qk"] = _dot_nt(ch["q16"], ch["kb16"]) * ch["decay"]

    for ch in chains:
        ch["s"] = s_ref[ch["idx"]]
        ch["wq16"] = [jnp.concatenate([ch["w"][ci * c:(ci + 1) * c], ch["qd"][ci * c:(ci + 1) * c]],
                                      axis=0).astype(BF16) for ci in range(n_chunks)]
        ch["ktt16"] = [ch["kt"][ci * c:(ci + 1) * c].T.astype(BF16) for ci in range(n_chunks)]
        ch["qk16"] = [ch["qk"][ci * c:(ci + 1) * c, ci * c:(ci + 1) * c].astype(BF16) for ci in range(n_chunks)]
    for step in range(n_chunks):
        cis = [step if ch["d"] == 0 else n_chunks - 1 - step for ch in chains]
        wss = [_dot(ch["wq16"][ci], ch["s"].astype(BF16)) for ch, ci in zip(chains, cis)]
        for ch, ci, ws in zip(chains, cis, wss):
            r = slice(ci * c, (ci + 1) * c)
            vn16 = (ch["u"][r] - ws[0:c]).astype(BF16)
            o = ws[c:2 * c] + _dot(ch["qk16"][ci], vn16)
            glast = ch["glast"][ci * c:ci * c + 1, :]
            ch["s"] = ch["s"] * glast + _dot(ch["ktt16"][ci], vn16)
            ch["o_ref"][r, ch["hs"]] = o
    for ch in chains:
        s_ref[ch["idx"]] = ch["s"]


def _gdn_call(gq, gk, gv, gb, cm, n_ctx_blocks):
    b, ta, w = gq.shape
    blk = GDN_BLOCK
    nb = ta // blk

    def bwd_block(j):
        return jnp.where(j < n_ctx_blocks, n_ctx_blocks - 1 - j, nb - 1 - (j - n_ctx_blocks))

    f_spec = lambda wd: pl.BlockSpec((None, blk, wd), lambda bi, j: (bi, j, 0))
    b_spec = lambda wd: pl.BlockSpec((None, blk, wd), lambda bi, j: (bi, bwd_block(j), 0))
    return pl.pallas_call(
        _gdn_kernel,
        grid=(b, nb),
        in_specs=[f_spec(w), f_spec(w), f_spec(w), f_spec(LANES),
                  b_spec(w), b_spec(w), b_spec(w), b_spec(LANES),
                  _const_spec(cm.shape)],
        out_specs=[f_spec(w), b_spec(w)],
        out_shape=[jax.ShapeDtypeStruct((b, ta, w), F32), jax.ShapeDtypeStruct((b, ta, w), F32)],
        scratch_shapes=[pltpu.VMEM((N_DIR * GDN_HEADS, GDN_HEAD_DIM, GDN_HEAD_DIM), F32)],
        compiler_params=_params("parallel", "arbitrary"),
        name="gdn",
    )(gq, gk, gv, gb, gq, gk, gv, gb, cm)


def _out_kernel(x_ref, attn_ref, of_ref, ob_ref, z_ref, mod_ref, gng_ref, wout_ref, n2g_ref,
                w1_ref, w2_ref, fng_ref, o_ref, *, d_model, final):
    mod = mod_ref[...]
    g1 = mod[:, 2 * d_model:3 * d_model]
    sh2 = mod[:, 3 * d_model:4 * d_model]
    sc2 = mod[:, 4 * d_model:5 * d_model]
    g2 = mod[:, 5 * d_model:6 * d_model]

    o = of_ref[...] + ob_ref[...]
    z = z_ref[...]
    gng = gng_ref[...]
    parts = [attn_ref[...]]
    for h in range(GDN_HEADS):
        hs = slice(h * LANES, (h + 1) * LANES)
        oh = o[:, hs]
        y = oh * lax.rsqrt(jnp.mean(oh * oh, axis=-1, keepdims=True) + NORM_EPS) * gng
        parts.append((y * _silu(z[:, hs])).astype(BF16))
    mix = jnp.concatenate(parts, axis=1)
    x1 = x_ref[...] + g1 * _dot(mix, wout_ref[...])

    y2 = x1 * lax.rsqrt(jnp.mean(x1 * x1, axis=-1, keepdims=True) + NORM_EPS)
    h2 = ((y2 * n2g_ref[...]) * (1.0 + sc2) + sh2).astype(BF16)
    ff = jnp.maximum(_dot(h2, w1_ref[...]), 0.0)
    ff = (ff * ff).astype(BF16)
    x2 = x1 + g2 * _dot(ff, w2_ref[...])
    if final:
        x2 = x2 * lax.rsqrt(jnp.mean(x2 * x2, axis=-1, keepdims=True) + NORM_EPS) * fng_ref[...]
    o_ref[...] = x2


def _out_call(xs, attn, o_f, o_b, z, mod, layer, wts, fng, n_ctx_tiles, final):
    b, ta, d = xs.shape
    tm = TOKEN_TILE
    d6 = 6 * d
    first = n_ctx_tiles if final else 0
    n_tiles = ta // tm - first
    tok = lambda w: pl.BlockSpec((None, tm, w), lambda bi, t: (bi, t + first, 0))
    kern = functools.partial(_out_kernel, d_model=d, final=final)
    out_rows = n_tiles * tm
    return pl.pallas_call(
        kern,
        grid=(b, n_tiles),
        in_specs=[
            tok(d),
            tok(MLA_HEADS * MLA_V_DIM),
            tok(GDN_WIDTH), tok(GDN_WIDTH), tok(GDN_WIDTH),
            pl.BlockSpec((None, None, None, 1, d6),
                         lambda bi, t: (layer, bi, jnp.where(t + first < n_ctx_tiles, 1, 0), 0, 0)),
            _const_spec((1, GDN_HEAD_DIM)),
            _const_spec(wts["w_out"].shape),
            _const_spec((1, d)),
            _const_spec(wts["w_ff1"].shape),
            _const_spec(wts["w_ff2"].shape),
            _const_spec((1, d)),
        ],
        out_specs=pl.BlockSpec((None, tm, d), lambda bi, t: (bi, t, 0)),
        out_shape=jax.ShapeDtypeStruct((b, out_rows, d), F32),
        compiler_params=_params("parallel", "parallel"),
        name="out_final" if final else "out",
    )(xs, attn, o_f, o_b, z, mod, wts["gdn_norm_g"], wts["w_out"], wts["norm2_g"],
      wts["w_ff1"], wts["w_ff2"], fng)


def _rope_tables(t_lat, t_ctx):
    rows = t_lat // GRID_W
    row = jnp.broadcast_to(jnp.arange(rows)[:, None], (rows, GRID_W)).reshape(-1).astype(F32)
    col = jnp.broadcast_to(jnp.arange(GRID_W)[None, :], (rows, GRID_W)).reshape(-1).astype(F32)
    axis_pairs = MLA_ROPE_DIM // 4
    inv_freq = ROPE_THETA ** (-jnp.arange(axis_pairs, dtype=F32) / axis_pairs)
    ang = jnp.concatenate([row[:, None] * inv_freq, col[:, None] * inv_freq], axis=-1)
    cos, sin = jnp.cos(ang), jnp.sin(ang)
    half = MLA_ROPE_DIM // 2
    lo, mid, hi = MLA_NOPE_DIM, MLA_NOPE_DIM + half, MLA_NOPE_DIM + 2 * half
    ta = t_ctx + t_lat
    ctab = jnp.ones((ta, LANES), F32).at[t_ctx:, lo:mid].set(cos).at[t_ctx:, mid:hi].set(cos)
    sa = jnp.zeros((ta, LANES), F32).at[t_ctx:, lo:mid].set(-sin)
    sb = jnp.zeros((ta, LANES), F32).at[t_ctx:, mid:hi].set(sin)
    return ctab, sa, sb


def _chunk_sum_matrices():
    i = jnp.arange(GDN_BLOCK)
    same = (i[:, None] // GDN_CHUNK) == (i[None, :] // GDN_CHUNK)
    lower = same & (i[:, None] >= i[None, :])
    upper = same & (i[:, None] <= i[None, :])
    return jnp.stack([lower, upper], axis=0).astype(BF16)


def _pad_heads(w, used):
    k = w.shape[0]
    w = w.reshape(k, MLA_HEADS, used)
    return jnp.pad(w, ((0, 0), (0, 0), (0, LANES - used))).reshape(k, MLA_HEADS * LANES)


def _layer_weights(i, norm1_g, w_in, q_a_g, w_q_b, kv_a_g, w_kv_b, conv_w, a_log, dt_bias, gdn_norm_g,
                   w_out, norm2_g, w_ff1, w_ff2):
    d = w_in.shape[1]
    wi = w_in[i]
    o = 0
    cols = []
    for sz in (MLA_Q_RANK, MLA_KV_RANK, MLA_ROPE_DIM, 3 * GDN_WIDTH, GDN_WIDTH,
               N_DIR * GDN_HEADS, N_DIR * GDN_HEADS):
        cols.append(wi[:, o:o + sz])
        o += sz
    w_qa, w_kva, w_kr, w_qkv, w_z, w_a, w_b = cols
    w_kr_pad = jnp.zeros((d, LANES), F32).at[:, MLA_NOPE_DIM:MLA_QK_DIM].set(w_kr)
    nab = N_DIR * GDN_HEADS
    w_ab_pad = jnp.zeros((d, LANES), F32).at[:, 0:nab].set(w_a).at[:, nab:2 * nab].set(w_b)
    w_main = jnp.concatenate([w_qa, w_kva, w_kr_pad, w_z, w_ab_pad], axis=1)
    kv = w_kv_b[i].reshape(MLA_KV_RANK, MLA_HEADS, MLA_NOPE_DIM + MLA_V_DIM)
    w_k = _pad_heads(kv[:, :, :MLA_NOPE_DIM].reshape(MLA_KV_RANK, -1), MLA_NOPE_DIM)
    w_v = _pad_heads(kv[:, :, MLA_NOPE_DIM:].reshape(MLA_KV_RANK, -1), MLA_V_DIM)
    row = lambda v: jnp.zeros((1, LANES), F32).at[0, 0:nab].set(v.reshape(-1))
    return dict(
        norm1_g=norm1_g[i][None, :],
        w_qkv=w_qkv.astype(BF16),
        w_main=w_main.astype(BF16),
        q_a_g=q_a_g[i][None, :],
        kv_a_g=kv_a_g[i][None, :],
        w_q=_pad_heads(w_q_b[i], MLA_QK_DIM).astype(BF16),
        w_k=w_k.astype(BF16),
        w_v=w_v.astype(BF16),
        conv_w=jnp.pad(conv_w[i], ((0, 8 - GDN_CONV), (0, 0))),
        a_log=row(a_log[i]),
        dt_bias=row(dt_bias[i]),
        gdn_norm_g=gdn_norm_g[i][None, :],
        w_out=w_out[i].astype(BF16),
        norm2_g=norm2_g[i][None, :],
        w_ff1=w_ff1[i].astype(BF16),
        w_ff2=w_ff2[i].astype(BF16),
    )


def kernel(x, c, ctx, c_ctx, w_ada, b_ada, norm1_g, w_in, q_a_g, w_q_b, kv_a_g, w_kv_b, conv_w, a_log,
           dt_bias, gdn_norm_g, w_out, norm2_g, w_ff1, w_ff2, final_norm_g):
    b, t_lat, d = x.shape
    t_ctx = ctx.shape[1]
    depth = w_ada.shape[0]
    assert t_ctx % TOKEN_TILE == 0 and t_lat % TOKEN_TILE == 0 and t_lat % GRID_W == 0
    assert b + 1 <= PAD_ROWS
    n_ctx_tiles = t_ctx // TOKEN_TILE

    xs = jnp.concatenate([ctx, x], axis=1)
    cc = jnp.zeros((PAD_ROWS, d), F32).at[0:b].set(c).at[b].set(c_ctx)
    mods = _ada_call(cc, w_ada, b_ada)
    mod = jnp.stack([mods[:, 0:b], jnp.broadcast_to(mods[:, b:b + 1], (depth, b, 6 * d))], axis=2)
    mod = mod[:, :, :, None, :]
    tabs = _rope_tables(t_lat, t_ctx)
    cm = _chunk_sum_matrices()
    fng = final_norm_g[None, :]

    for i in range(depth):
        last = i == depth - 1
        wts = _layer_weights(i, norm1_g, w_in, q_a_g, w_q_b, kv_a_g, w_kv_b, conv_w, a_log, dt_bias,
                             gdn_norm_g, w_out, norm2_g, w_ff1, w_ff2)
        qt, k, vt, gq, gk, gv, gb, z = _proj_call(xs, mod, i, wts, tabs, n_ctx_tiles)
        attn = _attn_call(qt, k, vt, t_ctx)
        o_f, o_b = _gdn_call(gq, gk, gv, gb, cm, t_ctx // GDN_BLOCK)
        xs = _out_call(xs, attn, o_f, o_b, z, mod, i, wts, fng, n_ctx_tiles, last)
    return xs
```

```python
import functools

import jax
import jax.numpy as jnp
from jax import lax
from jax.experimental import pallas as pl
from jax.experimental.pallas import tpu as pltpu

F32 = jnp.float32
BF16 = jnp.bfloat16

GRID_W = 64
MLA_HEADS = 8
MLA_NOPE_DIM = 64
MLA_ROPE_DIM = 32
MLA_V_DIM = 64
MLA_Q_RANK = 256
MLA_KV_RANK = 128
MLA_QK_DIM = MLA_NOPE_DIM + MLA_ROPE_DIM
GDN_HEADS = 4
GDN_HEAD_DIM = 128
GDN_WIDTH = GDN_HEADS * GDN_HEAD_DIM
GDN_CONV = 5
GDN_CHUNK = 64
N_DIR = 2
ROPE_THETA = 10000.0
NORM_EPS = 1e-6
LOG2_E = 1.4426950408889634

LANES = 128
BF16_SUBLANES = 16
VMEM_LIMIT_BYTES = 56 * 1024 * 1024

TOKEN_TILE = 256
GDN_BLOCK = 256
ATTN_KEY_TILE = 256
NEG_BIG = -1e30
PAD_ROWS = 16


def _sigmoid(x):
    return 1.0 / (1.0 + jnp.exp(-x))


def _silu(x):
    return x * _sigmoid(x)


def _dot(a, b):
    return jnp.dot(a, b, preferred_element_type=F32)


def _dot_nt(a, b):
    return lax.dot_general(a, b, (((1,), (1,)), ((), ())), preferred_element_type=F32)


def _dot_tn(a, b):
    return lax.dot_general(a, b, (((0,), (0,)), ((), ())), preferred_element_type=F32)


def _const_spec(shape):
    nd = len(shape)
    return pl.BlockSpec(shape, lambda *_: (0,) * nd, pipeline_mode=pl.Buffered(1))


def _params(*sem):
    return pltpu.CompilerParams(dimension_semantics=sem, vmem_limit_bytes=VMEM_LIMIT_BYTES)


def _ada_kernel(c_ref, w_ref, b_ref, o_ref):
    s = _silu(c_ref[...]).astype(BF16)
    o_ref[0] = _dot(s, w_ref[0].astype(BF16)) + b_ref[0]


def _ada_call(cc, w_ada, b_ada):
    n_layers, d, d6 = w_ada.shape
    tn = 1536
    return pl.pallas_call(
        _ada_kernel,
        grid=(n_layers, d6 // tn),
        in_specs=[
            pl.BlockSpec((PAD_ROWS, d), lambda l, n: (0, 0)),
            pl.BlockSpec((1, d, tn), lambda l, n: (l, 0, n)),
            pl.BlockSpec((1, 1, tn), lambda l, n: (l, 0, n)),
        ],
        out_specs=pl.BlockSpec((1, PAD_ROWS, tn), lambda l, n: (l, 0, n)),
        out_shape=jax.ShapeDtypeStruct((n_layers, PAD_ROWS, d6), F32),
        compiler_params=_params("parallel", "parallel"),
        name="ada",
    )(cc, w_ada, b_ada.reshape(n_layers, 1, d6))


def _rope(x, c, sa, sb):
    return x * c + pltpu.roll(x, LANES - 16, 1) * sa + pltpu.roll(x, 16, 1) * sb


def _proj_kernel(xc_ref, xp_ref, xn_ref, mod_ref, n1g_ref, wqkv_ref, wmain_ref, qag_ref, kvag_ref,
                 wq_ref, wk_ref, wv_ref, ctab_ref, sa_ref, sb_ref, convw_ref, alog_ref, dtb_ref,
                 qt_ref, k_ref, vt_ref, gq_ref, gk_ref, gv_ref, gb_ref, zs_ref,
                 hext_ref, pqkv_ref, *, tm, n_ctx_tiles, n_tiles, d_model):
    t = pl.program_id(1)
    halo = BF16_SUBLANES
    mod = mod_ref[...]
    sh1 = mod[:, 0:d_model]
    sc1 = mod[:, d_model:2 * d_model]
    n1g = n1g_ref[...]

    def modulate(x):
        y = x * lax.rsqrt(jnp.mean(x * x, axis=-1, keepdims=True) + NORM_EPS)
        return ((y * n1g) * (1.0 + sc1) + sh1).astype(BF16)

    hext_ref[0:halo, :] = modulate(xp_ref[...])
    hext_ref[halo:halo + tm, :] = modulate(xc_ref[...])
    hext_ref[halo + tm:, :] = modulate(xn_ref[...])

    pqkv_ref[...] = _dot(hext_ref[...], wqkv_ref[...])
    seg_first = jnp.logical_or(t == 0, t == n_ctx_tiles)
    seg_last = jnp.logical_or(t == n_ctx_tiles - 1, t == n_tiles - 1)

    @pl.when(seg_first)
    def _():
        pqkv_ref[0:halo, :] = jnp.zeros((halo, 3 * GDN_WIDTH), F32)

    @pl.when(seg_last)
    def _():
        pqkv_ref[halo + tm:, :] = jnp.zeros((halo, 3 * GDN_WIDTH), F32)

    pm = _dot(hext_ref[halo:halo + tm, :], wmain_ref[...])
    o_kva = MLA_Q_RANK
    o_kr = o_kva + MLA_KV_RANK
    o_z = o_kr + LANES
    o_ab = o_z + GDN_WIDTH
    qa = pm[:, 0:o_kva]
    kva = pm[:, o_kva:o_kr]
    kr = pm[:, o_kr:o_z]
    zs_ref[...] = _silu(pm[:, o_z:o_ab]).astype(BF16)
    ab = pm[:, o_ab:o_ab + LANES]

    ctab = ctab_ref[...]
    sa = sa_ref[...]
    sb = sb_ref[...]

    qn = (qa * lax.rsqrt(jnp.mean(qa * qa, axis=-1, keepdims=True) + NORM_EPS) * qag_ref[...]).astype(BF16)
    q = _dot(qn, wq_ref[...])
    scale = MLA_QK_DIM ** -0.5 * LOG2_E
    for h in range(MLA_HEADS):
        qh = q[:, h * LANES:(h + 1) * LANES]
        qt_ref[h] = (_rope(qh, ctab, sa, sb) * scale).T.astype(BF16)

    kvn = (kva * lax.rsqrt(jnp.mean(kva * kva, axis=-1, keepdims=True) + NORM_EPS) * kvag_ref[...]).astype(BF16)
    kk = _dot(kvn, wk_ref[...])
    kpe = _rope(kr, ctab, sa, sb)
    for h in range(MLA_HEADS):
        k_ref[:, h * LANES:(h + 1) * LANES] = (kk[:, h * LANES:(h + 1) * LANES] + kpe).astype(BF16)
    vv = _dot(kvn, wv_ref[...])
    lane = lax.broadcasted_iota(jnp.int32, (tm, LANES), 1)
    for h in range(MLA_HEADS):
        vh = jnp.where(lane == MLA_V_DIM, 1.0, vv[:, h * LANES:(h + 1) * LANES])
        vt_ref[h] = vh.T.astype(BF16)

    cw = convw_ref[...]
    base = halo - GDN_CONV // 2
    for cb in range(3 * GDN_HEADS):
        cs = slice(cb * LANES, (cb + 1) * LANES)
        acc = pqkv_ref[base:base + tm, cs] * cw[0:1, cs]
        for j in range(1, GDN_CONV):
            acc = acc + pqkv_ref[base + j:base + j + tm, cs] * cw[j:j + 1, cs]
        act = _silu(acc)
        grp, hh = divmod(cb, GDN_HEADS)
        hs = slice(hh * LANES, (hh + 1) * LANES)
        if grp < 2:
            act = act * lax.rsqrt(jnp.sum(act * act, axis=-1, keepdims=True) + NORM_EPS)
            (gq_ref if grp == 0 else gk_ref)[:, hs] = act.astype(BF16)
        else:
            gv_ref[:, hs] = act.astype(BF16)

    x = ab + dtb_ref[...]
    softplus = jnp.maximum(x, 0.0) + jnp.log(1.0 + jnp.exp(-jnp.abs(x)))
    gval = -jnp.exp(alog_ref[...]) * softplus
    lane_ab = lax.broadcasted_iota(jnp.int32, ab.shape, 1)
    gb_ref[...] = jnp.where(lane_ab < N_DIR * GDN_HEADS, gval, _sigmoid(ab))


def _proj_call(xs, mod, layer, wts, tabs, n_ctx_tiles):
    b, ta, d = xs.shape
    tm = TOKEN_TILE
    n_tiles = ta // tm
    halo = BF16_SUBLANES
    hb = tm // halo
    n_halo_blocks = ta // halo
    d6 = 6 * d

    tok = lambda w: pl.BlockSpec((None, tm, w), lambda bi, t: (bi, t, 0))
    in_specs = [
        tok(d),
        pl.BlockSpec((None, halo, d), lambda bi, t: (bi, jnp.maximum(t * hb - 1, 0), 0)),
        pl.BlockSpec((None, halo, d), lambda bi, t: (bi, jnp.minimum((t + 1) * hb, n_halo_blocks - 1), 0)),
        pl.BlockSpec((None, None, None, 1, d6),
                     lambda bi, t: (layer, bi, jnp.where(t < n_ctx_tiles, 1, 0), 0, 0)),
        _const_spec((1, d)),
        _const_spec((d, 3 * GDN_WIDTH)),
        _const_spec(wts["w_main"].shape),
        _const_spec((1, MLA_Q_RANK)),
        _const_spec((1, MLA_KV_RANK)),
        _const_spec((MLA_Q_RANK, MLA_HEADS * LANES)),
        _const_spec((MLA_KV_RANK, MLA_HEADS * LANES)),
        _const_spec((MLA_KV_RANK, MLA_HEADS * LANES)),
        pl.BlockSpec((tm, LANES), lambda bi, t: (t, 0)),
        pl.BlockSpec((tm, LANES), lambda bi, t: (t, 0)),
        pl.BlockSpec((tm, LANES), lambda bi, t: (t, 0)),
        _const_spec((8, 3 * GDN_WIDTH)),
        _const_spec((1, LANES)),
        _const_spec((1, LANES)),
    ]
    head_t = pl.BlockSpec((None, MLA_HEADS, LANES, tm), lambda bi, t: (bi, 0, 0, t))
    out_specs = [
        head_t,
        tok(MLA_HEADS * LANES),
        head_t,
        tok(GDN_WIDTH), tok(GDN_WIDTH), tok(GDN_WIDTH),
        tok(LANES),
        tok(GDN_WIDTH),
    ]
    out_shape = [
        jax.ShapeDtypeStruct((b, MLA_HEADS, LANES, ta), BF16),
        jax.ShapeDtypeStruct((b, ta, MLA_HEADS * LANES), BF16),
        jax.ShapeDtypeStruct((b, MLA_HEADS, LANES, ta), BF16),
        jax.ShapeDtypeStruct((b, ta, GDN_WIDTH), BF16),
        jax.ShapeDtypeStruct((b, ta, GDN_WIDTH), BF16),
        jax.ShapeDtypeStruct((b, ta, GDN_WIDTH), BF16),
        jax.ShapeDtypeStruct((b, ta, LANES), F32),
        jax.ShapeDtypeStruct((b, ta, GDN_WIDTH), BF16),
    ]
    kern = functools.partial(_proj_kernel, tm=tm, n_ctx_tiles=n_ctx_tiles, n_tiles=n_tiles, d_model=d)
    return pl.pallas_call(
        kern,
        grid=(b, n_tiles),
        in_specs=in_specs,
        out_specs=out_specs,
        out_shape=out_shape,
        scratch_shapes=[pltpu.VMEM((tm + 2 * halo, d), BF16),
                        pltpu.VMEM((tm + 2 * halo, 3 * GDN_WIDTH), F32)],
        compiler_params=_params("parallel", "parallel"),
        name="proj",
    )(xs, xs, xs, mod, wts["norm1_g"], wts["w_qkv"], wts["w_main"], wts["q_a_g"], wts["kv_a_g"],
      wts["w_q"], wts["w_k"], wts["w_v"], tabs[0], tabs[1], tabs[2], wts["conv_w"], wts["a_log"], wts["dt_bias"])


def _attn_kernel(qt_ref, k_ref, vt_ref, o_ref, s_ref, *, n_ctx_q, tc, ta):
    qi = pl.program_id(1)
    kt = ATTN_KEY_TILE

    def score_tile(h, t, slot):
        s = _dot(k_ref[t * kt:(t + 1) * kt, h * LANES:(h + 1) * LANES], qt_ref[h])
        s_ref[slot, t * kt:(t + 1) * kt, :] = s
        return jnp.max(s, axis=0, keepdims=True)

    def pv_tile(h, t, slot, m):
        p = jnp.exp2(s_ref[slot, t * kt:(t + 1) * kt, :] - m).astype(BF16)
        return _dot(vt_ref[h, :, t * kt:(t + 1) * kt], p)

    def run(nk):
        nt = nk // kt
        outs = []
        m = functools.reduce(jnp.maximum, [score_tile(0, t, 0) for t in range(nt)])
        for h in range(MLA_HEADS):
            slot = h % 2
            acc = None
            m_parts = []
            for t in range(nt):
                if h + 1 < MLA_HEADS:
                    m_parts.append(score_tile(h + 1, t, 1 - slot))
                o = pv_tile(h, t, slot, m)
                acc = o if acc is None else acc + o
            outs.append(acc[0:MLA_V_DIM] / acc[MLA_V_DIM:MLA_V_DIM + 1])
            if m_parts:
                m = functools.reduce(jnp.maximum, m_parts)
        o_ref[...] = jnp.concatenate(outs, axis=0).T.astype(BF16)

    @pl.when(qi < n_ctx_q)
    def _():
        run(tc)

    @pl.when(qi >= n_ctx_q)
    def _():
        run(ta)


def _attn_call(qt, k, vt, tc):
    b, ta, _ = k.shape
    tq = TOKEN_TILE
    n_ctx_q = tc // tq
    nq = ta // tq
    kern = functools.partial(_attn_kernel, n_ctx_q=n_ctx_q, tc=tc, ta=ta)
    return pl.pallas_call(
        kern,
        grid=(b, nq),
        in_specs=[
            pl.BlockSpec((None, MLA_HEADS, LANES, tq), lambda bi, qi: (bi, 0, 0, qi)),
            pl.BlockSpec((None, ta, MLA_HEADS * LANES), lambda bi, qi: (bi, 0, 0)),
            pl.BlockSpec((None, MLA_HEADS, LANES, ta), lambda bi, qi: (bi, 0, 0, 0)),
        ],
        out_specs=pl.BlockSpec((None, tq, MLA_HEADS * MLA_V_DIM), lambda bi, qi: (bi, qi, 0)),
        out_shape=jax.ShapeDtypeStruct((b, nq * tq, MLA_HEADS * MLA_V_DIM), BF16),
        scratch_shapes=[pltpu.VMEM((2, ta, tq), F32)],
        compiler_params=_params("parallel", "arbitrary"),
        name="attn",
    )(qt, k, vt)


def _gdn_kernel(qf_ref, kf_ref, vf_ref, gf_ref, qb_ref, kb_ref, vb_ref, gbk_ref, cm_ref,
                of_ref, ob_ref, s_ref):
    j = pl.program_id(1)
    blk = GDN_BLOCK
    c = GDN_CHUNK
    n_chunks = blk // c

    @pl.when(j == 0)
    def _():
        s_ref[...] = jnp.zeros(s_ref.shape, F32)

    row = lax.broadcasted_iota(jnp.int32, (blk, blk), 0)
    col = lax.broadcasted_iota(jnp.int32, (blk, blk), 1)
    same = (row // c) == (col // c)
    eye = jnp.where(row == col, 1.0, 0.0)
    pair_masks = []
    size = 1
    while size < c:
        pair_masks.append(jnp.logical_and((row // (2 * size)) == (col // (2 * size)),
                                          (row // size) != (col // size)))
        size *= 2
    scale = GDN_HEAD_DIM ** -0.5

    chains = []
    for d, (q_ref, k_ref, v_ref, g_ref, o_ref) in enumerate(
            ((qf_ref, kf_ref, vf_ref, gf_ref, of_ref), (qb_ref, kb_ref, vb_ref, gbk_ref, ob_ref))):
        incl = jnp.logical_and(same, (row >= col) if d == 0 else (row <= col))
        strict = jnp.logical_and(same, (row > col) if d == 0 else (row < col))
        gb = g_ref[...]
        g1 = gb.astype(BF16)
        r1 = gb - g1.astype(F32)
        g2 = r1.astype(BF16)
        g3 = (r1 - g2.astype(F32)).astype(BF16)
        cmd = cm_ref[d]
        gc = _dot(cmd, g1) + _dot(cmd, g2) + _dot(cmd, g3)
        last = c - 1 if d == 0 else 0
        gtot = jnp.concatenate(
            [jnp.broadcast_to(gc[ci * c + last:ci * c + last + 1, :], (c, LANES)) for ci in range(n_chunks)],
            axis=0)
        gct = gc.T
        for h in range(GDN_HEADS):
            ln = d * GDN_HEADS + h
            hs = slice(h * LANES, (h + 1) * LANES)
            gcol = gc[:, ln:ln + 1]
            gl = gtot[:, ln:ln + 1]
            beta = gb[:, N_DIR * GDN_HEADS + ln:N_DIR * GDN_HEADS + ln + 1]
            decay = jnp.exp(jnp.where(incl, gcol - gct[ln:ln + 1, :], NEG_BIG))
            k16 = k_ref[:, hs]
            q16 = q_ref[:, hs]
            k = k16.astype(F32)
            kbeta = k * beta
            egc = jnp.exp(gcol)
            chains.append(dict(
                d=d, hs=hs, o_ref=o_ref, idx=ln, strict=strict, decay=decay, kb16=k16,
                kbeta16=kbeta.astype(BF16), q16=q16,
                rhs16=jnp.concatenate([v_ref[:, hs].astype(F32) * beta, kbeta * egc], axis=1).astype(BF16),
                qd=q16.astype(F32) * (egc * scale), kt=k * jnp.exp(gl - gcol), glast=jnp.exp(gl)))

    for ch in chains:
        gram = _dot_nt(ch["kbeta16"], ch["kb16"])
        ch["lower"] = jnp.where(ch["strict"], gram * ch["decay"], 0.0)
        ch["t"] = eye - jnp.where(pair_masks[0], ch["lower"], 0.0)
    for pm in pair_masks[1:]:
        for ch in chains:
            ch["t16"] = ch["t"].astype(BF16)
            ch["p16"] = _dot(jnp.where(pm, ch["lower"], 0.0).astype(BF16), ch["t16"]).astype(BF16)
        for ch in chains:
            ch["t"] = ch["t"] - _dot(ch["t16"], ch["p16"])
    for ch in chains:
        x = _dot(ch["t"].astype(BF16), ch["rhs16"])
        ch["u"] = x[:, 0:LANES]
        ch["w"] = x[:, LANES:2 * LANES]
        ch["qk"] = _dot_nt(ch["q16"], ch["kb16"]) * (ch["decay"] * scale)

    for ch in chains:
        ch["s"] = s_ref[ch["idx"]]
        ch["wq16"] = [jnp.concatenate([ch["w"][ci * c:(ci + 1) * c], ch["qd"][ci * c:(ci + 1) * c]],
                                      axis=0).astype(BF16) for ci in range(n_chunks)]
        ch["ktt16"] = [ch["kt"][ci * c:(ci + 1) * c].T.astype(BF16) for ci in range(n_chunks)]
        ch["qk16"] = [ch["qk"][ci * c:(ci + 1) * c, ci * c:(ci + 1) * c].astype(BF16) for ci in range(n_chunks)]
    for step in range(n_chunks):
        cis = [step if ch["d"] == 0 else n_chunks - 1 - step for ch in chains]
        wss = [_dot(ch["wq16"][ci], ch["s"].astype(BF16)) for ch, ci in zip(chains, cis)]
        for ch, ci, ws in zip(chains, cis, wss):
            r = slice(ci * c, (ci + 1) * c)
            vn16 = (ch["u"][r] - ws[0:c]).astype(BF16)
            o = ws[c:2 * c] + _dot(ch["qk16"][ci], vn16)
            glast = ch["glast"][ci * c:ci * c + 1, :]
            ch["s"] = ch["s"] * glast + _dot(ch["ktt16"][ci], vn16)
            ch["o_ref"][r, ch["hs"]] = o.astype(BF16)
    for ch in chains:
        s_ref[ch["idx"]] = ch["s"]


def _gdn_call(gq, gk, gv, gb, cm, n_ctx_blocks):
    b, ta, w = gq.shape
    blk = GDN_BLOCK
    nb = ta // blk

    def bwd_block(j):
        return jnp.where(j < n_ctx_blocks, n_ctx_blocks - 1 - j, nb - 1 - (j - n_ctx_blocks))

    f_spec = lambda wd: pl.BlockSpec((None, blk, wd), lambda bi, j: (bi, j, 0))
    b_spec = lambda wd: pl.BlockSpec((None, blk, wd), lambda bi, j: (bi, bwd_block(j), 0))
    return pl.pallas_call(
        _gdn_kernel,
        grid=(b, nb),
        in_specs=[f_spec(w), f_spec(w), f_spec(w), f_spec(LANES),
                  b_spec(w), b_spec(w), b_spec(w), b_spec(LANES),
                  _const_spec(cm.shape)],
        out_specs=[f_spec(w), b_spec(w)],
        out_shape=[jax.ShapeDtypeStruct((b, ta, w), BF16), jax.ShapeDtypeStruct((b, ta, w), BF16)],
        scratch_shapes=[pltpu.VMEM((N_DIR * GDN_HEADS, GDN_HEAD_DIM, GDN_HEAD_DIM), F32)],
        compiler_params=_params("parallel", "arbitrary"),
        name="gdn",
    )(gq, gk, gv, gb, gq, gk, gv, gb, cm)


def _out_kernel(x_ref, attn_ref, of_ref, ob_ref, zs_ref, mod_ref, gng_ref, wout_ref, n2g_ref,
                w1_ref, w2_ref, fng_ref, o_ref, *, d_model, final):
    mod = mod_ref[...]
    g1 = mod[:, 2 * d_model:3 * d_model]
    sh2 = mod[:, 3 * d_model:4 * d_model]
    sc2 = mod[:, 4 * d_model:5 * d_model]
    g2 = mod[:, 5 * d_model:6 * d_model]

    o = of_ref[...].astype(F32) + ob_ref[...].astype(F32)
    zs = zs_ref[...].astype(F32)
    gng = gng_ref[...]
    parts = [attn_ref[...]]
    for h in range(GDN_HEADS):
        hs = slice(h * LANES, (h + 1) * LANES)
        oh = o[:, hs]
        y = oh * lax.rsqrt(jnp.mean(oh * oh, axis=-1, keepdims=True) + NORM_EPS) * gng
        parts.append((y * zs[:, hs]).astype(BF16))
    mix = jnp.concatenate(parts, axis=1)
    x1 = x_ref[...] + g1 * _dot(mix, wout_ref[...])

    y2 = x1 * lax.rsqrt(jnp.mean(x1 * x1, axis=-1, keepdims=True) + NORM_EPS)
    h2 = ((y2 * n2g_ref[...]) * (1.0 + sc2) + sh2).astype(BF16)
    ff = jnp.maximum(_dot(h2, w1_ref[...]), 0.0)
    ff = (ff * ff).astype(BF16)
    x2 = x1 + g2 * _dot(ff, w2_ref[...])
    if final:
        x2 = x2 * lax.rsqrt(jnp.mean(x2 * x2, axis=-1, keepdims=True) + NORM_EPS) * fng_ref[...]
    o_ref[...] = x2


def _out_call(xs, attn, o_f, o_b, z, mod, layer, wts, fng, n_ctx_tiles, final):
    b, ta, d = xs.shape
    tm = TOKEN_TILE
    d6 = 6 * d
    first = n_ctx_tiles if final else 0
    n_tiles = ta // tm - first
    tok = lambda w: pl.BlockSpec((None, tm, w), lambda bi, t: (bi, t + first, 0))
    kern = functools.partial(_out_kernel, d_model=d, final=final)
    out_rows = n_tiles * tm
    return pl.pallas_call(
        kern,
        grid=(b, n_tiles),
        in_specs=[
            tok(d),
            tok(MLA_HEADS * MLA_V_DIM),
            tok(GDN_WIDTH), tok(GDN_WIDTH), tok(GDN_WIDTH),
            pl.BlockSpec((None, None, None, 1, d6),
                         lambda bi, t: (layer, bi, jnp.where(t + first < n_ctx_tiles, 1, 0), 0, 0)),
            _const_spec((1, GDN_HEAD_DIM)),
            _const_spec(wts["w_out"].shape),
            _const_spec((1, d)),
            _const_spec(wts["w_ff1"].shape),
            _const_spec(wts["w_ff2"].shape),
            _const_spec((1, d)),
        ],
        out_specs=pl.BlockSpec((None, tm, d), lambda bi, t: (bi, t, 0)),
        out_shape=jax.ShapeDtypeStruct((b, out_rows, d), F32),
        compiler_params=_params("parallel", "parallel"),
        name="out_final" if final else "out",
    )(xs, attn, o_f, o_b, z, mod, wts["gdn_norm_g"], wts["w_out"], wts["norm2_g"],
      wts["w_ff1"], wts["w_ff2"], fng)


def _rope_tables(t_lat, t_ctx):
    rows = t_lat // GRID_W
    row = jnp.broadcast_to(jnp.arange(rows)[:, None], (rows, GRID_W)).reshape(-1).astype(F32)
    col = jnp.broadcast_to(jnp.arange(GRID_W)[None, :], (rows, GRID_W)).reshape(-1).astype(F32)
    axis_pairs = MLA_ROPE_DIM // 4
    inv_freq = ROPE_THETA ** (-jnp.arange(axis_pairs, dtype=F32) / axis_pairs)
    ang = jnp.concatenate([row[:, None] * inv_freq, col[:, None] * inv_freq], axis=-1)
    cos, sin = jnp.cos(ang), jnp.sin(ang)
    half = MLA_ROPE_DIM // 2
    lo, mid, hi = MLA_NOPE_DIM, MLA_NOPE_DIM + half, MLA_NOPE_DIM + 2 * half
    ta = t_ctx + t_lat
    ctab = jnp.ones((ta, LANES), F32).at[t_ctx:, lo:mid].set(cos).at[t_ctx:, mid:hi].set(cos)
    sa = jnp.zeros((ta, LANES), F32).at[t_ctx:, lo:mid].set(-sin)
    sb = jnp.zeros((ta, LANES), F32).at[t_ctx:, mid:hi].set(sin)
    return ctab, sa, sb


def _chunk_sum_matrices():
    i = jnp.arange(GDN_BLOCK)
    same = (i[:, None] // GDN_CHUNK) == (i[None, :] // GDN_CHUNK)
    lower = same & (i[:, None] >= i[None, :])
    upper = same & (i[:, None] <= i[None, :])
    return jnp.stack([lower, upper], axis=0).astype(BF16)


def _pad_heads(w, used):
    k = w.shape[0]
    w = w.reshape(k, MLA_HEADS, used)
    return jnp.pad(w, ((0, 0), (0, 0), (0, LANES - used))).reshape(k, MLA_HEADS * LANES)


def _layer_weights(i, norm1_g, w_in, q_a_g, w_q_b, kv_a_g, w_kv_b, conv_w, a_log, dt_bias, gdn_norm_g,
                   w_out, norm2_g, w_ff1, w_ff2):
    d = w_in.shape[1]
    wi = w_in[i]
    o = 0
    cols = []
    for sz in (MLA_Q_RANK, MLA_KV_RANK, MLA_ROPE_DIM, 3 * GDN_WIDTH, GDN_WIDTH,
               N_DIR * GDN_HEADS, N_DIR * GDN_HEADS):
        cols.append(wi[:, o:o + sz])
        o += sz
    w_qa, w_kva, w_kr, w_qkv, w_z, w_a, w_b = cols
    w_kr_pad = jnp.zeros((d, LANES), F32).at[:, MLA_NOPE_DIM:MLA_QK_DIM].set(w_kr)
    nab = N_DIR * GDN_HEADS
    w_ab_pad = jnp.zeros((d, LANES), F32).at[:, 0:nab].set(w_a).at[:, nab:2 * nab].set(w_b)
    w_main = jnp.concatenate([w_qa, w_kva, w_kr_pad, w_z, w_ab_pad], axis=1)
    kv = w_kv_b[i].reshape(MLA_KV_RANK, MLA_HEADS, MLA_NOPE_DIM + MLA_V_DIM)
    w_k = _pad_heads(kv[:, :, :MLA_NOPE_DIM].reshape(MLA_KV_RANK, -1), MLA_NOPE_DIM)
    w_v = _pad_heads(kv[:, :, MLA_NOPE_DIM:].reshape(MLA_KV_RANK, -1), MLA_V_DIM)
    row = lambda v: jnp.zeros((1, LANES), F32).at[0, 0:nab].set(v.reshape(-1))
    return dict(
        norm1_g=norm1_g[i][None, :],
        w_qkv=w_qkv.astype(BF16),
        w_main=w_main.astype(BF16),
        q_a_g=q_a_g[i][None, :],
        kv_a_g=kv_a_g[i][None, :],
        w_q=_pad_heads(w_q_b[i], MLA_QK_DIM).astype(BF16),
        w_k=w_k.astype(BF16),
        w_v=w_v.astype(BF16),
        conv_w=jnp.pad(conv_w[i], ((0, 8 - GDN_CONV), (0, 0))),
        a_log=row(a_log[i]),
        dt_bias=row(dt_bias[i]),
        gdn_norm_g=gdn_norm_g[i][None, :],
        w_out=w_out[i].astype(BF16),
        norm2_g=norm2_g[i][None, :],
        w_ff1=w_ff1[i].astype(BF16),
        w_ff2=w_ff2[i].astype(BF16),
    )


def kernel(x, c, ctx, c_ctx, w_ada, b_ada, norm1_g, w_in, q_a_g, w_q_b, kv_a_g, w_kv_b, conv_w, a_log,
           dt_bias, gdn_norm_g, w_out, norm2_g, w_ff1, w_ff2, final_norm_g):
    b, t_lat, d = x.shape
    t_ctx = ctx.shape[1]
    depth = w_ada.shape[0]
    assert t_ctx % TOKEN_TILE == 0 and t_lat % TOKEN_TILE == 0 and t_lat % GRID_W == 0
    assert b + 1 <= PAD_ROWS
    n_ctx_tiles = t_ctx // TOKEN_TILE

    xs = jnp.concatenate([ctx, x], axis=1)
    cc = jnp.zeros((PAD_ROWS, d), F32).at[0:b].set(c).at[b].set(c_ctx)
    mods = _ada_call(cc, w_ada, b_ada)
    mod = jnp.stack([mods[:, 0:b], jnp.broadcast_to(mods[:, b:b + 1], (depth, b, 6 * d))], axis=2)
    mod = mod[:, :, :, None, :]
    tabs = _rope_tables(t_lat, t_ctx)
    cm = _chunk_sum_matrices()
    fng = final_norm_g[None, :]

    for i in range(depth):
        last = i == depth - 1
        wts = _layer_weights(i, norm1_g, w_in, q_a_g, w_q_b, kv_a_g, w_kv_b, conv_w, a_log, dt_bias,
                             gdn_norm_g, w_out, norm2_g, w_ff1, w_ff2)
        qt, k, vt, gq, gk, gv, gb, z = _proj_call(xs, mod, i, wts, tabs, n_ctx_tiles)
        attn = _attn_call(qt, k, vt, t_ctx)
        o_f, o_b = _gdn_call(gq, gk, gv, gb, cm, t_ctx // GDN_BLOCK)
        xs = _out_call(xs, attn, o_f, o_b, z, mod, i, wts, fng, n_ctx_tiles, last)
    return xs
```

```python
import functools

import jax
import jax.numpy as jnp
from jax import lax
from jax.experimental import pallas as pl
from jax.experimental.pallas import tpu as pltpu

F32 = jnp.float32
BF16 = jnp.bfloat16

GRID_W = 64
MLA_HEADS = 8
MLA_NOPE_DIM = 64
MLA_ROPE_DIM = 32
MLA_V_DIM = 64
MLA_Q_RANK = 256
MLA_KV_RANK = 128
MLA_QK_DIM = MLA_NOPE_DIM + MLA_ROPE_DIM
GDN_HEADS = 4
GDN_HEAD_DIM = 128
GDN_WIDTH = GDN_HEADS * GDN_HEAD_DIM
GDN_CONV = 5
GDN_CHUNK = 64
N_DIR = 2
ROPE_THETA = 10000.0
NORM_EPS = 1e-6
LOG2_E = 1.4426950408889634

LANES = 128
BF16_SUBLANES = 16
VMEM_LIMIT_BYTES = 56 * 1024 * 1024

TOKEN_TILE = 256
GDN_BLOCK = 256
GDN_BATCH = 2
ATTN_KEY_TILE = 256
NEG_BIG = -1e30
PAD_ROWS = 16


def _sigmoid(x):
    return 1.0 / (1.0 + jnp.exp(-x))


def _silu(x):
    return x * _sigmoid(x)


def _dot(a, b):
    return jnp.dot(a, b, preferred_element_type=F32)


def _dot_nt(a, b):
    return lax.dot_general(a, b, (((1,), (1,)), ((), ())), preferred_element_type=F32)


def _dot_tn(a, b):
    return lax.dot_general(a, b, (((0,), (0,)), ((), ())), preferred_element_type=F32)


def _const_spec(shape):
    nd = len(shape)
    return pl.BlockSpec(shape, lambda *_: (0,) * nd, pipeline_mode=pl.Buffered(1))


def _params(*sem):
    return pltpu.CompilerParams(dimension_semantics=sem, vmem_limit_bytes=VMEM_LIMIT_BYTES)


def _ada_kernel(c_ref, w_ref, b_ref, o_ref):
    s = _silu(c_ref[...]).astype(BF16)
    o_ref[0] = _dot(s, w_ref[0].astype(BF16)) + b_ref[0]


def _ada_call(cc, w_ada, b_ada):
    n_layers, d, d6 = w_ada.shape
    tn = 1536
    return pl.pallas_call(
        _ada_kernel,
        grid=(n_layers, d6 // tn),
        in_specs=[
            pl.BlockSpec((PAD_ROWS, d), lambda l, n: (0, 0)),
            pl.BlockSpec((1, d, tn), lambda l, n: (l, 0, n)),
            pl.BlockSpec((1, 1, tn), lambda l, n: (l, 0, n)),
        ],
        out_specs=pl.BlockSpec((1, PAD_ROWS, tn), lambda l, n: (l, 0, n)),
        out_shape=jax.ShapeDtypeStruct((n_layers, PAD_ROWS, d6), F32),
        compiler_params=_params("parallel", "parallel"),
        name="ada",
    )(cc, w_ada, b_ada.reshape(n_layers, 1, d6))


def _rope(x, c, sa, sb):
    return x * c + pltpu.roll(x, LANES - 16, 1) * sa + pltpu.roll(x, 16, 1) * sb


def _proj_kernel(xc_ref, xp_ref, xn_ref, mod_ref, n1g_ref, wqkv_ref, wmain_ref, qag_ref, kvag_ref,
                 wq_ref, wk_ref, wv_ref, ctab_ref, sa_ref, sb_ref, convw_ref, alog_ref, dtb_ref,
                 qt_ref, k_ref, vt_ref, gq_ref, gk_ref, gv_ref, gb_ref, zs_ref,
                 hext_ref, pqkv_ref, *, tm, n_ctx_tiles, n_tiles, d_model):
    t = pl.program_id(1)
    halo = BF16_SUBLANES
    mod = mod_ref[...]
    sh1 = mod[:, 0:d_model]
    sc1 = mod[:, d_model:2 * d_model]
    n1g = n1g_ref[...]

    def modulate(x):
        y = x * lax.rsqrt(jnp.mean(x * x, axis=-1, keepdims=True) + NORM_EPS)
        return ((y * n1g) * (1.0 + sc1) + sh1).astype(BF16)

    hext_ref[0:halo, :] = modulate(xp_ref[...])
    hext_ref[halo:halo + tm, :] = modulate(xc_ref[...])
    hext_ref[halo + tm:, :] = modulate(xn_ref[...])

    pqkv_ref[...] = _dot(hext_ref[...], wqkv_ref[...])
    seg_first = jnp.logical_or(t == 0, t == n_ctx_tiles)
    seg_last = jnp.logical_or(t == n_ctx_tiles - 1, t == n_tiles - 1)

    @pl.when(seg_first)
    def _():
        pqkv_ref[0:halo, :] = jnp.zeros((halo, 3 * GDN_WIDTH), F32)

    @pl.when(seg_last)
    def _():
        pqkv_ref[halo + tm:, :] = jnp.zeros((halo, 3 * GDN_WIDTH), F32)

    pm = _dot(hext_ref[halo:halo + tm, :], wmain_ref[...])
    o_kva = MLA_Q_RANK
    o_kr = o_kva + MLA_KV_RANK
    o_z = o_kr + LANES
    o_ab = o_z + GDN_WIDTH
    qa = pm[:, 0:o_kva]
    kva = pm[:, o_kva:o_kr]
    kr = pm[:, o_kr:o_z]
    zs_ref[...] = _silu(pm[:, o_z:o_ab]).astype(BF16)
    ab = pm[:, o_ab:o_ab + LANES]

    ctab = ctab_ref[...]
    sa = sa_ref[...]
    sb = sb_ref[...]

    qn = (qa * lax.rsqrt(jnp.mean(qa * qa, axis=-1, keepdims=True) + NORM_EPS) * qag_ref[...]).astype(BF16)
    q = _dot(qn, wq_ref[...])
    scale = MLA_QK_DIM ** -0.5 * LOG2_E
    for h in range(MLA_HEADS):
        qh = q[:, h * LANES:(h + 1) * LANES]
        qt_ref[h] = (_rope(qh, ctab, sa, sb) * scale).T.astype(BF16)

    kvn = (kva * lax.rsqrt(jnp.mean(kva * kva, axis=-1, keepdims=True) + NORM_EPS) * kvag_ref[...]).astype(BF16)
    kk = _dot(kvn, wk_ref[...])
    kpe = _rope(kr, ctab, sa, sb)
    for h in range(MLA_HEADS):
        k_ref[:, h * LANES:(h + 1) * LANES] = (kk[:, h * LANES:(h + 1) * LANES] + kpe).astype(BF16)
    vv = _dot(kvn, wv_ref[...])
    lane = lax.broadcasted_iota(jnp.int32, (tm, LANES), 1)
    for h in range(MLA_HEADS):
        vh = jnp.where(lane == MLA_V_DIM, 1.0, vv[:, h * LANES:(h + 1) * LANES])
        vt_ref[h] = vh.T.astype(BF16)

    cw = convw_ref[...]
    base = halo - GDN_CONV // 2
    for cb in range(3 * GDN_HEADS):
        cs = slice(cb * LANES, (cb + 1) * LANES)
        acc = pqkv_ref[base:base + tm, cs] * cw[0:1, cs]
        for j in range(1, GDN_CONV):
            acc = acc + pqkv_ref[base + j:base + j + tm, cs] * cw[j:j + 1, cs]
        act = _silu(acc)
        grp, hh = divmod(cb, GDN_HEADS)
        hs = slice(hh * LANES, (hh + 1) * LANES)
        if grp < 2:
            act = act * lax.rsqrt(jnp.sum(act * act, axis=-1, keepdims=True) + NORM_EPS)
            (gq_ref if grp == 0 else gk_ref)[:, hs] = act.astype(BF16)
        else:
            gv_ref[:, hs] = act.astype(BF16)

    x = ab + dtb_ref[...]
    softplus = jnp.maximum(x, 0.0) + jnp.log(1.0 + jnp.exp(-jnp.abs(x)))
    gval = -jnp.exp(alog_ref[...]) * softplus
    lane_ab = lax.broadcasted_iota(jnp.int32, ab.shape, 1)
    gb_ref[...] = jnp.where(lane_ab < N_DIR * GDN_HEADS, gval, _sigmoid(ab))


def _proj_call(xs, mod, layer, wts, tabs, n_ctx_tiles):
    b, ta, d = xs.shape
    tm = TOKEN_TILE
    n_tiles = ta // tm
    halo = BF16_SUBLANES
    hb = tm // halo
    n_halo_blocks = ta // halo
    d6 = 6 * d

    tok = lambda w: pl.BlockSpec((None, tm, w), lambda bi, t: (bi, t, 0))
    in_specs = [
        tok(d),
        pl.BlockSpec((None, halo, d), lambda bi, t: (bi, jnp.maximum(t * hb - 1, 0), 0)),
        pl.BlockSpec((None, halo, d), lambda bi, t: (bi, jnp.minimum((t + 1) * hb, n_halo_blocks - 1), 0)),
        pl.BlockSpec((None, None, None, 1, d6),
                     lambda bi, t: (layer, bi, jnp.where(t < n_ctx_tiles, 1, 0), 0, 0)),
        _const_spec((1, d)),
        _const_spec((d, 3 * GDN_WIDTH)),
        _const_spec(wts["w_main"].shape),
        _const_spec((1, MLA_Q_RANK)),
        _const_spec((1, MLA_KV_RANK)),
        _const_spec((MLA_Q_RANK, MLA_HEADS * LANES)),
        _const_spec((MLA_KV_RANK, MLA_HEADS * LANES)),
        _const_spec((MLA_KV_RANK, MLA_HEADS * LANES)),
        pl.BlockSpec((tm, LANES), lambda bi, t: (t, 0)),
        pl.BlockSpec((tm, LANES), lambda bi, t: (t, 0)),
        pl.BlockSpec((tm, LANES), lambda bi, t: (t, 0)),
        _const_spec((8, 3 * GDN_WIDTH)),
        _const_spec((1, LANES)),
        _const_spec((1, LANES)),
    ]
    head_t = pl.BlockSpec((None, MLA_HEADS, LANES, tm), lambda bi, t: (bi, 0, 0, t))
    out_specs = [
        head_t,
        tok(MLA_HEADS * LANES),
        head_t,
        tok(GDN_WIDTH), tok(GDN_WIDTH), tok(GDN_WIDTH),
        tok(LANES),
        tok(GDN_WIDTH),
    ]
    out_shape = [
        jax.ShapeDtypeStruct((b, MLA_HEADS, LANES, ta), BF16),
        jax.ShapeDtypeStruct((b, ta, MLA_HEADS * LANES), BF16),
        jax.ShapeDtypeStruct((b, MLA_HEADS, LANES, ta), BF16),
        jax.ShapeDtypeStruct((b, ta, GDN_WIDTH), BF16),
        jax.ShapeDtypeStruct((b, ta, GDN_WIDTH), BF16),
        jax.ShapeDtypeStruct((b, ta, GDN_WIDTH), BF16),
        jax.ShapeDtypeStruct((b, ta, LANES), F32),
        jax.ShapeDtypeStruct((b, ta, GDN_WIDTH), BF16),
    ]
    kern = functools.partial(_proj_kernel, tm=tm, n_ctx_tiles=n_ctx_tiles, n_tiles=n_tiles, d_model=d)
    return pl.pallas_call(
        kern,
        grid=(b, n_tiles),
        in_specs=in_specs,
        out_specs=out_specs,
        out_shape=out_shape,
        scratch_shapes=[pltpu.VMEM((tm + 2 * halo, d), BF16),
                        pltpu.VMEM((tm + 2 * halo, 3 * GDN_WIDTH), F32)],
        compiler_params=_params("parallel", "parallel"),
        name="proj",
    )(xs, xs, xs, mod, wts["norm1_g"], wts["w_qkv"], wts["w_main"], wts["q_a_g"], wts["kv_a_g"],
      wts["w_q"], wts["w_k"], wts["w_v"], tabs[0], tabs[1], tabs[2], wts["conv_w"], wts["a_log"], wts["dt_bias"])


def _attn_kernel(qt_ref, k_ref, vt_ref, o_ref, s_ref, *, n_ctx_q, tc, ta):
    qi = pl.program_id(1)
    kt = ATTN_KEY_TILE

    def score_tile(h, t, slot):
        s = _dot(k_ref[t * kt:(t + 1) * kt, h * LANES:(h + 1) * LANES], qt_ref[h])
        s_ref[slot, t * kt:(t + 1) * kt, :] = s
        return jnp.max(s, axis=0, keepdims=True)

    def pv_tile(h, t, slot, m):
        p = jnp.exp2(s_ref[slot, t * kt:(t + 1) * kt, :] - m).astype(BF16)
        return _dot(vt_ref[h, :, t * kt:(t + 1) * kt], p)

    def run(nk):
        nt = nk // kt
        outs = []
        m = functools.reduce(jnp.maximum, [score_tile(0, t, 0) for t in range(nt)])
        for h in range(MLA_HEADS):
            slot = h % 2
            acc = None
            m_parts = []
            for t in range(nt):
                if h + 1 < MLA_HEADS:
                    m_parts.append(score_tile(h + 1, t, 1 - slot))
                o = pv_tile(h, t, slot, m)
                acc = o if acc is None else acc + o
            outs.append(acc[0:MLA_V_DIM] / acc[MLA_V_DIM:MLA_V_DIM + 1])
            if m_parts:
                m = functools.reduce(jnp.maximum, m_parts)
        o_ref[...] = jnp.concatenate(outs, axis=0).T.astype(BF16)

    @pl.when(qi < n_ctx_q)
    def _():
        run(tc)

    @pl.when(qi >= n_ctx_q)
    def _():
        run(ta)


def _attn_call(qt, k, vt, tc):
    b, ta, _ = k.shape
    tq = TOKEN_TILE
    n_ctx_q = tc // tq
    nq = ta // tq
    kern = functools.partial(_attn_kernel, n_ctx_q=n_ctx_q, tc=tc, ta=ta)
    return pl.pallas_call(
        kern,
        grid=(b, nq),
        in_specs=[
            pl.BlockSpec((None, MLA_HEADS, LANES, tq), lambda bi, qi: (bi, 0, 0, qi)),
            pl.BlockSpec((None, ta, MLA_HEADS * LANES), lambda bi, qi: (bi, 0, 0)),
            pl.BlockSpec((None, MLA_HEADS, LANES, ta), lambda bi, qi: (bi, 0, 0, 0)),
        ],
        out_specs=pl.BlockSpec((None, tq, MLA_HEADS * MLA_V_DIM), lambda bi, qi: (bi, qi, 0)),
        out_shape=jax.ShapeDtypeStruct((b, nq * tq, MLA_HEADS * MLA_V_DIM), BF16),
        scratch_shapes=[pltpu.VMEM((2, ta, tq), F32)],
        compiler_params=_params("parallel", "arbitrary"),
        name="attn",
    )(qt, k, vt)


def _gdn_kernel(qf_ref, kf_ref, vf_ref, gf_ref, qb_ref, kb_ref, vb_ref, gbk_ref, cm_ref,
                of_ref, ob_ref, s_ref):
    j = pl.program_id(1)
    blk = GDN_BLOCK
    c = GDN_CHUNK
    n_chunks = blk // c

    @pl.when(j == 0)
    def _():
        s_ref[...] = jnp.zeros(s_ref.shape, F32)

    row = lax.broadcasted_iota(jnp.int32, (blk, blk), 0)
    col = lax.broadcasted_iota(jnp.int32, (blk, blk), 1)
    same = (row // c) == (col // c)
    eye = jnp.where(row == col, 1.0, 0.0)
    pair_masks = []
    size = 1
    while size < c:
        pair_masks.append(jnp.logical_and((row // (2 * size)) == (col // (2 * size)),
                                          (row // size) != (col // size)))
        size *= 2
    scale = GDN_HEAD_DIM ** -0.5

    chains = []
    dir_refs = ((qf_ref, kf_ref, vf_ref, gf_ref, of_ref), (qb_ref, kb_ref, vb_ref, gbk_ref, ob_ref))
    for bb, d in [(bb, d) for bb in range(GDN_BATCH) for d in range(N_DIR)]:
        q_ref, k_ref, v_ref, g_ref, o_ref = (r.at[bb] for r in dir_refs[d])
        incl = jnp.logical_and(same, (row >= col) if d == 0 else (row <= col))
        strict = jnp.logical_and(same, (row > col) if d == 0 else (row < col))
        gb = g_ref[...]
        g1 = gb.astype(BF16)
        r1 = gb - g1.astype(F32)
        g2 = r1.astype(BF16)
        g3 = (r1 - g2.astype(F32)).astype(BF16)
        cmd = cm_ref[d]
        gc = _dot(cmd, g1) + _dot(cmd, g2) + _dot(cmd, g3)
        last = c - 1 if d == 0 else 0
        gtot = jnp.concatenate(
            [jnp.broadcast_to(gc[ci * c + last:ci * c + last + 1, :], (c, LANES)) for ci in range(n_chunks)],
            axis=0)
        gct = gc.T
        for h in range(GDN_HEADS):
            ln = d * GDN_HEADS + h
            hs = slice(h * LANES, (h + 1) * LANES)
            gcol = gc[:, ln:ln + 1]
            gl = gtot[:, ln:ln + 1]
            beta = gb[:, N_DIR * GDN_HEADS + ln:N_DIR * GDN_HEADS + ln + 1]
            decay = jnp.exp(jnp.where(incl, gcol - gct[ln:ln + 1, :], NEG_BIG))
            k16 = k_ref[:, hs]
            q16 = q_ref[:, hs]
            k = k16.astype(F32)
            kbeta = k * beta
            egc = jnp.exp(gcol)
            chains.append(dict(
                d=d, hs=hs, o_ref=o_ref, idx=bb * N_DIR * GDN_HEADS + ln, strict=strict, decay=decay, kb16=k16,
                kbeta16=kbeta.astype(BF16), q16=q16,
                rhs16=jnp.concatenate([v_ref[:, hs].astype(F32) * beta, kbeta * egc], axis=1).astype(BF16),
                qd=q16.astype(F32) * (egc * scale), kt=k * jnp.exp(gl - gcol), glast=jnp.exp(gl)))

    for ch in chains:
        gram = _dot_nt(ch["kbeta16"], ch["kb16"])
        ch["lower"] = jnp.where(ch["strict"], gram * ch["decay"], 0.0)
        ch["t"] = eye - jnp.where(pair_masks[0], ch["lower"], 0.0)
    for pm in pair_masks[1:]:
        for ch in chains:
            ch["t16"] = ch["t"].astype(BF16)
            ch["p16"] = _dot(jnp.where(pm, ch["lower"], 0.0).astype(BF16), ch["t16"]).astype(BF16)
        for ch in chains:
            ch["t"] = ch["t"] - _dot(ch["t16"], ch["p16"])
    for ch in chains:
        x = _dot(ch["t"].astype(BF16), ch["rhs16"])
        ch["u"] = x[:, 0:LANES]
        ch["w"] = x[:, LANES:2 * LANES]
        ch["qk"] = _dot_nt(ch["q16"], ch["kb16"]) * (ch["decay"] * scale)

    for ch in chains:
        ch["s"] = s_ref[ch["idx"]]
        ch["wq16"] = [jnp.concatenate([ch["w"][ci * c:(ci + 1) * c], ch["qd"][ci * c:(ci + 1) * c]],
                                      axis=0).astype(BF16) for ci in range(n_chunks)]
        ch["ktt16"] = [ch["kt"][ci * c:(ci + 1) * c].T.astype(BF16) for ci in range(n_chunks)]
        ch["qk16"] = [ch["qk"][ci * c:(ci + 1) * c, ci * c:(ci + 1) * c].astype(BF16) for ci in range(n_chunks)]
    for step in range(n_chunks):
        cis = [step if ch["d"] == 0 else n_chunks - 1 - step for ch in chains]
        wss = [_dot(ch["wq16"][ci], ch["s"].astype(BF16)) for ch, ci in zip(chains, cis)]
        for ch, ci, ws in zip(chains, cis, wss):
            r = slice(ci * c, (ci + 1) * c)
            vn16 = (ch["u"][r] - ws[0:c]).astype(BF16)
            o = ws[c:2 * c] + _dot(ch["qk16"][ci], vn16)
            glast = ch["glast"][ci * c:ci * c + 1, :]
            ch["s"] = ch["s"] * glast + _dot(ch["ktt16"][ci], vn16)
            ch["o_ref"][r, ch["hs"]] = o.astype(BF16)
    for ch in chains:
        s_ref[ch["idx"]] = ch["s"]


def _gdn_call(gq, gk, gv, gb, cm, n_ctx_blocks):
    b, ta, w = gq.shape
    blk = GDN_BLOCK
    nb = ta // blk

    def bwd_block(j):
        return jnp.where(j < n_ctx_blocks, n_ctx_blocks - 1 - j, nb - 1 - (j - n_ctx_blocks))

    gbt = GDN_BATCH
    assert b % gbt == 0
    f_spec = lambda wd: pl.BlockSpec((gbt, blk, wd), lambda bi, j: (bi, j, 0))
    b_spec = lambda wd: pl.BlockSpec((gbt, blk, wd), lambda bi, j: (bi, bwd_block(j), 0))
    return pl.pallas_call(
        _gdn_kernel,
        grid=(b // gbt, nb),
        in_specs=[f_spec(w), f_spec(w), f_spec(w), f_spec(LANES),
                  b_spec(w), b_spec(w), b_spec(w), b_spec(LANES),
                  _const_spec(cm.shape)],
        out_specs=[f_spec(w), b_spec(w)],
        out_shape=[jax.ShapeDtypeStruct((b, ta, w), BF16), jax.ShapeDtypeStruct((b, ta, w), BF16)],
        scratch_shapes=[pltpu.VMEM((gbt * N_DIR * GDN_HEADS, GDN_HEAD_DIM, GDN_HEAD_DIM), F32)],
        compiler_params=_params("parallel", "arbitrary"),
        name="gdn",
    )(gq, gk, gv, gb, gq, gk, gv, gb, cm)


def _out_kernel(x_ref, attn_ref, of_ref, ob_ref, zs_ref, mod_ref, gng_ref, wout_ref, n2g_ref,
                w1_ref, w2_ref, fng_ref, o_ref, *, d_model, final):
    mod = mod_ref[...]
    g1 = mod[:, 2 * d_model:3 * d_model]
    sh2 = mod[:, 3 * d_model:4 * d_model]
    sc2 = mod[:, 4 * d_model:5 * d_model]
    g2 = mod[:, 5 * d_model:6 * d_model]

    o = of_ref[...].astype(F32) + ob_ref[...].astype(F32)
    zs = zs_ref[...].astype(F32)
    gng = gng_ref[...]
    parts = [attn_ref[...]]
    for h in range(GDN_HEADS):
        hs = slice(h * LANES, (h + 1) * LANES)
        oh = o[:, hs]
        y = oh * lax.rsqrt(jnp.mean(oh * oh, axis=-1, keepdims=True) + NORM_EPS) * gng
        parts.append((y * zs[:, hs]).astype(BF16))
    mix = jnp.concatenate(parts, axis=1)
    x1 = x_ref[...] + g1 * _dot(mix, wout_ref[...])

    y2 = x1 * lax.rsqrt(jnp.mean(x1 * x1, axis=-1, keepdims=True) + NORM_EPS)
    h2 = ((y2 * n2g_ref[...]) * (1.0 + sc2) + sh2).astype(BF16)
    ff = jnp.maximum(_dot(h2, w1_ref[...]), 0.0)
    ff = (ff * ff).astype(BF16)
    x2 = x1 + g2 * _dot(ff, w2_ref[...])
    if final:
        x2 = x2 * lax.rsqrt(jnp.mean(x2 * x2, axis=-1, keepdims=True) + NORM_EPS) * fng_ref[...]
    o_ref[...] = x2


def _out_call(xs, attn, o_f, o_b, z, mod, layer, wts, fng, n_ctx_tiles, final):
    b, ta, d = xs.shape
    tm = TOKEN_TILE
    d6 = 6 * d
    first = n_ctx_tiles if final else 0
    n_tiles = ta // tm - first
    tok = lambda w: pl.BlockSpec((None, tm, w), lambda bi, t: (bi, t + first, 0))
    kern = functools.partial(_out_kernel, d_model=d, final=final)
    out_rows = n_tiles * tm
    return pl.pallas_call(
        kern,
        grid=(b, n_tiles),
        in_specs=[
            tok(d),
            tok(MLA_HEADS * MLA_V_DIM),
            tok(GDN_WIDTH), tok(GDN_WIDTH), tok(GDN_WIDTH),
            pl.BlockSpec((None, None, None, 1, d6),
                         lambda bi, t: (layer, bi, jnp.where(t + first < n_ctx_tiles, 1, 0), 0, 0)),
            _const_spec((1, GDN_HEAD_DIM)),
            _const_spec(wts["w_out"].shape),
            _const_spec((1, d)),
            _const_spec(wts["w_ff1"].shape),
            _const_spec(wts["w_ff2"].shape),
            _const_spec((1, d)),
        ],
        out_specs=pl.BlockSpec((None, tm, d), lambda bi, t: (bi, t, 0)),
        out_shape=jax.ShapeDtypeStruct((b, out_rows, d), F32),
        compiler_params=_params("parallel", "parallel"),
        name="out_final" if final else "out",
    )(xs, attn, o_f, o_b, z, mod, wts["gdn_norm_g"], wts["w_out"], wts["norm2_g"],
      wts["w_ff1"], wts["w_ff2"], fng)


def _rope_tables(t_lat, t_ctx):
    rows = t_lat // GRID_W
    row = jnp.broadcast_to(jnp.arange(rows)[:, None], (rows, GRID_W)).reshape(-1).astype(F32)
    col = jnp.broadcast_to(jnp.arange(GRID_W)[None, :], (rows, GRID_W)).reshape(-1).astype(F32)
    axis_pairs = MLA_ROPE_DIM // 4
    inv_freq = ROPE_THETA ** (-jnp.arange(axis_pairs, dtype=F32) / axis_pairs)
    ang = jnp.concatenate([row[:, None] * inv_freq, col[:, None] * inv_freq], axis=-1)
    cos, sin = jnp.cos(ang), jnp.sin(ang)
    half = MLA_ROPE_DIM // 2
    lo, mid, hi = MLA_NOPE_DIM, MLA_NOPE_DIM + half, MLA_NOPE_DIM + 2 * half
    ta = t_ctx + t_lat
    ctab = jnp.ones((ta, LANES), F32).at[t_ctx:, lo:mid].set(cos).at[t_ctx:, mid:hi].set(cos)
    sa = jnp.zeros((ta, LANES), F32).at[t_ctx:, lo:mid].set(-sin)
    sb = jnp.zeros((ta, LANES), F32).at[t_ctx:, mid:hi].set(sin)
    return ctab, sa, sb


def _chunk_sum_matrices():
    i = jnp.arange(GDN_BLOCK)
    same = (i[:, None] // GDN_CHUNK) == (i[None, :] // GDN_CHUNK)
    lower = same & (i[:, None] >= i[None, :])
    upper = same & (i[:, None] <= i[None, :])
    return jnp.stack([lower, upper], axis=0).astype(BF16)


def _pad_heads(w, used):
    k = w.shape[0]
    w = w.reshape(k, MLA_HEADS, used)
    return jnp.pad(w, ((0, 0), (0, 0), (0, LANES - used))).reshape(k, MLA_HEADS * LANES)


def _layer_weights(i, norm1_g, w_in, q_a_g, w_q_b, kv_a_g, w_kv_b, conv_w, a_log, dt_bias, gdn_norm_g,
                   w_out, norm2_g, w_ff1, w_ff2):
    d = w_in.shape[1]
    wi = w_in[i]
    o = 0
    cols = []
    for sz in (MLA_Q_RANK, MLA_KV_RANK, MLA_ROPE_DIM, 3 * GDN_WIDTH, GDN_WIDTH,
               N_DIR * GDN_HEADS, N_DIR * GDN_HEADS):
        cols.append(wi[:, o:o + sz])
        o += sz
    w_qa, w_kva, w_kr, w_qkv, w_z, w_a, w_b = cols
    w_kr_pad = jnp.zeros((d, LANES), F32).at[:, MLA_NOPE_DIM:MLA_QK_DIM].set(w_kr)
    nab = N_DIR * GDN_HEADS
    w_ab_pad = jnp.zeros((d, LANES), F32).at[:, 0:nab].set(w_a).at[:, nab:2 * nab].set(w_b)
    w_main = jnp.concatenate([w_qa, w_kva, w_kr_pad, w_z, w_ab_pad], axis=1)
    kv = w_kv_b[i].reshape(MLA_KV_RANK, MLA_HEADS, MLA_NOPE_DIM + MLA_V_DIM)
    w_k = _pad_heads(kv[:, :, :MLA_NOPE_DIM].reshape(MLA_KV_RANK, -1), MLA_NOPE_DIM)
    w_v = _pad_heads(kv[:, :, MLA_NOPE_DIM:].reshape(MLA_KV_RANK, -1), MLA_V_DIM)
    row = lambda v: jnp.zeros((1, LANES), F32).at[0, 0:nab].set(v.reshape(-1))
    return dict(
        norm1_g=norm1_g[i][None, :],
        w_qkv=w_qkv.astype(BF16),
        w_main=w_main.astype(BF16),
        q_a_g=q_a_g[i][None, :],
        kv_a_g=kv_a_g[i][None, :],
        w_q=_pad_heads(w_q_b[i], MLA_QK_DIM).astype(BF16),
        w_k=w_k.astype(BF16),
        w_v=w_v.astype(BF16),
        conv_w=jnp.pad(conv_w[i], ((0, 8 - GDN_CONV), (0, 0))),
        a_log=row(a_log[i]),
        dt_bias=row(dt_bias[i]),
        gdn_norm_g=gdn_norm_g[i][None, :],
        w_out=w_out[i].astype(BF16),
        norm2_g=norm2_g[i][None, :],
        w_ff1=w_ff1[i].astype(BF16),
        w_ff2=w_ff2[i].astype(BF16),
    )


def kernel(x, c, ctx, c_ctx, w_ada, b_ada, norm1_g, w_in, q_a_g, w_q_b, kv_a_g, w_kv_b, conv_w, a_log,
           dt_bias, gdn_norm_g, w_out, norm2_g, w_ff1, w_ff2, final_norm_g):
    b, t_lat, d = x.shape
    t_ctx = ctx.shape[1]
    depth = w_ada.shape[0]
    assert t_ctx % TOKEN_TILE == 0 and t_lat % TOKEN_TILE == 0 and t_lat % GRID_W == 0
    assert b + 1 <= PAD_ROWS
    n_ctx_tiles = t_ctx // TOKEN_TILE

    xs = jnp.concatenate([ctx, x], axis=1)
    cc = jnp.zeros((PAD_ROWS, d), F32).at[0:b].set(c).at[b].set(c_ctx)
    mods = _ada_call(cc, w_ada, b_ada)
    mod = jnp.stack([mods[:, 0:b], jnp.broadcast_to(mods[:, b:b + 1], (depth, b, 6 * d))], axis=2)
    mod = mod[:, :, :, None, :]
    tabs = _rope_tables(t_lat, t_ctx)
    cm = _chunk_sum_matrices()
    fng = final_norm_g[None, :]

    for i in range(depth):
        last = i == depth - 1
        wts = _layer_weights(i, norm1_g, w_in, q_a_g, w_q_b, kv_a_g, w_kv_b, conv_w, a_log, dt_bias,
                             gdn_norm_g, w_out, norm2_g, w_ff1, w_ff2)
        qt, k, vt, gq, gk, gv, gb, z = _proj_call(xs, mod, i, wts, tabs, n_ctx_tiles)
        attn = _attn_call(qt, k, vt, t_ctx)
        o_f, o_b = _gdn_call(gq, gk, gv, gb, cm, t_ctx // GDN_BLOCK)
        xs = _out_call(xs, attn, o_f, o_b, z, mod, i, wts, fng, n_ctx_tiles, last)
    return xs
```

```python
import functools

import jax
import jax.numpy as jnp
from jax import lax
from jax.experimental import pallas as pl
from jax.experimental.pallas import tpu as pltpu

F32 = jnp.float32
BF16 = jnp.bfloat16

GRID_W = 64
MLA_HEADS = 8
MLA_NOPE_DIM = 64
MLA_ROPE_DIM = 32
MLA_V_DIM = 64
MLA_Q_RANK = 256
MLA_KV_RANK = 128
MLA_QK_DIM = MLA_NOPE_DIM + MLA_ROPE_DIM
GDN_HEADS = 4
GDN_HEAD_DIM = 128
GDN_WIDTH = GDN_HEADS * GDN_HEAD_DIM
GDN_CONV = 5
GDN_CHUNK = 64
N_DIR = 2
ROPE_THETA = 10000.0
NORM_EPS = 1e-6
LOG2_E = 1.4426950408889634

LANES = 128
BF16_SUBLANES = 16
VMEM_LIMIT_BYTES = 56 * 1024 * 1024

TOKEN_TILE = 256
GDN_BLOCK = 256
GDN_BATCH = 2
GDN_WAVE_LAG = 0
ATTN_KEY_TILE = 256
NEG_BIG = -1e30
PAD_ROWS = 16


def _sigmoid(x):
    return 1.0 / (1.0 + jnp.exp(-x))


def _silu(x):
    return x * _sigmoid(x)


def _dot(a, b):
    return jnp.dot(a, b, preferred_element_type=F32)


def _dot_nt(a, b):
    return lax.dot_general(a, b, (((1,), (1,)), ((), ())), preferred_element_type=F32)


def _dot_tn(a, b):
    return lax.dot_general(a, b, (((0,), (0,)), ((), ())), preferred_element_type=F32)


def _const_spec(shape):
    nd = len(shape)
    return pl.BlockSpec(shape, lambda *_: (0,) * nd, pipeline_mode=pl.Buffered(1))


def _params(*sem):
    return pltpu.CompilerParams(dimension_semantics=sem, vmem_limit_bytes=VMEM_LIMIT_BYTES)


def _ada_kernel(c_ref, w_ref, b_ref, o_ref):
    s = _silu(c_ref[...]).astype(BF16)
    o_ref[0] = _dot(s, w_ref[0].astype(BF16)) + b_ref[0]


def _ada_call(cc, w_ada, b_ada):
    n_layers, d, d6 = w_ada.shape
    tn = 1536
    return pl.pallas_call(
        _ada_kernel,
        grid=(n_layers, d6 // tn),
        in_specs=[
            pl.BlockSpec((PAD_ROWS, d), lambda l, n: (0, 0)),
            pl.BlockSpec((1, d, tn), lambda l, n: (l, 0, n)),
            pl.BlockSpec((1, 1, tn), lambda l, n: (l, 0, n)),
        ],
        out_specs=pl.BlockSpec((1, PAD_ROWS, tn), lambda l, n: (l, 0, n)),
        out_shape=jax.ShapeDtypeStruct((n_layers, PAD_ROWS, d6), F32),
        compiler_params=_params("parallel", "parallel"),
        name="ada",
    )(cc, w_ada, b_ada.reshape(n_layers, 1, d6))


def _rope(x, c, sa, sb):
    return x * c + pltpu.roll(x, LANES - 16, 1) * sa + pltpu.roll(x, 16, 1) * sb


def _proj_kernel(xc_ref, xp_ref, xn_ref, mod_ref, n1g_ref, wqkv_ref, wmain_ref, qag_ref, kvag_ref,
                 wq_ref, wk_ref, wv_ref, ctab_ref, sa_ref, sb_ref, convw_ref, alog_ref, dtb_ref,
                 qt_ref, k_ref, vt_ref, gq_ref, gk_ref, gv_ref, gb_ref, zs_ref,
                 hext_ref, pqkv_ref, *, tm, n_ctx_tiles, n_tiles, d_model):
    t = pl.program_id(1)
    halo = BF16_SUBLANES
    mod = mod_ref[...]
    sh1 = mod[:, 0:d_model]
    sc1 = mod[:, d_model:2 * d_model]
    n1g = n1g_ref[...]

    def modulate(x):
        y = x * lax.rsqrt(jnp.mean(x * x, axis=-1, keepdims=True) + NORM_EPS)
        return ((y * n1g) * (1.0 + sc1) + sh1).astype(BF16)

    hext_ref[0:halo, :] = modulate(xp_ref[...])
    hext_ref[halo:halo + tm, :] = modulate(xc_ref[...])
    hext_ref[halo + tm:, :] = modulate(xn_ref[...])

    pqkv_ref[...] = _dot(hext_ref[...], wqkv_ref[...])
    seg_first = jnp.logical_or(t == 0, t == n_ctx_tiles)
    seg_last = jnp.logical_or(t == n_ctx_tiles - 1, t == n_tiles - 1)

    @pl.when(seg_first)
    def _():
        pqkv_ref[0:halo, :] = jnp.zeros((halo, 3 * GDN_WIDTH), F32)

    @pl.when(seg_last)
    def _():
        pqkv_ref[halo + tm:, :] = jnp.zeros((halo, 3 * GDN_WIDTH), F32)

    pm = _dot(hext_ref[halo:halo + tm, :], wmain_ref[...])
    o_kva = MLA_Q_RANK
    o_kr = o_kva + MLA_KV_RANK
    o_z = o_kr + LANES
    o_ab = o_z + GDN_WIDTH
    qa = pm[:, 0:o_kva]
    kva = pm[:, o_kva:o_kr]
    kr = pm[:, o_kr:o_z]
    zs_ref[...] = _silu(pm[:, o_z:o_ab]).astype(BF16)
    ab = pm[:, o_ab:o_ab + LANES]

    ctab = ctab_ref[...]
    sa = sa_ref[...]
    sb = sb_ref[...]

    qn = (qa * lax.rsqrt(jnp.mean(qa * qa, axis=-1, keepdims=True) + NORM_EPS) * qag_ref[...]).astype(BF16)
    q = _dot(qn, wq_ref[...])
    scale = MLA_QK_DIM ** -0.5 * LOG2_E
    for h in range(MLA_HEADS):
        qh = q[:, h * LANES:(h + 1) * LANES]
        qt_ref[h] = (_rope(qh, ctab, sa, sb) * scale).T.astype(BF16)

    kvn = (kva * lax.rsqrt(jnp.mean(kva * kva, axis=-1, keepdims=True) + NORM_EPS) * kvag_ref[...]).astype(BF16)
    kk = _dot(kvn, wk_ref[...])
    kpe = _rope(kr, ctab, sa, sb)
    for h in range(MLA_HEADS):
        k_ref[:, h * LANES:(h + 1) * LANES] = (kk[:, h * LANES:(h + 1) * LANES] + kpe).astype(BF16)
    vv = _dot(kvn, wv_ref[...])
    lane = lax.broadcasted_iota(jnp.int32, (tm, LANES), 1)
    for h in range(MLA_HEADS):
        vh = jnp.where(lane == MLA_V_DIM, 1.0, vv[:, h * LANES:(h + 1) * LANES])
        vt_ref[h] = vh.T.astype(BF16)

    cw = convw_ref[...]
    base = halo - GDN_CONV // 2
    for cb in range(3 * GDN_HEADS):
        cs = slice(cb * LANES, (cb + 1) * LANES)
        acc = pqkv_ref[base:base + tm, cs] * cw[0:1, cs]
        for j in range(1, GDN_CONV):
            acc = acc + pqkv_ref[base + j:base + j + tm, cs] * cw[j:j + 1, cs]
        act = _silu(acc)
        grp, hh = divmod(cb, GDN_HEADS)
        hs = slice(hh * LANES, (hh + 1) * LANES)
        if grp < 2:
            act = act * lax.rsqrt(jnp.sum(act * act, axis=-1, keepdims=True) + NORM_EPS)
            (gq_ref if grp == 0 else gk_ref)[:, hs] = act.astype(BF16)
        else:
            gv_ref[:, hs] = act.astype(BF16)

    x = ab + dtb_ref[...]
    softplus = jnp.maximum(x, 0.0) + jnp.log(1.0 + jnp.exp(-jnp.abs(x)))
    gval = -jnp.exp(alog_ref[...]) * softplus
    lane_ab = lax.broadcasted_iota(jnp.int32, ab.shape, 1)
    gb_ref[...] = jnp.where(lane_ab < N_DIR * GDN_HEADS, gval, _sigmoid(ab))


def _proj_call(xs, mod, layer, wts, tabs, n_ctx_tiles):
    b, ta, d = xs.shape
    tm = TOKEN_TILE
    n_tiles = ta // tm
    halo = BF16_SUBLANES
    hb = tm // halo
    n_halo_blocks = ta // halo
    d6 = 6 * d

    tok = lambda w: pl.BlockSpec((None, tm, w), lambda bi, t: (bi, t, 0))
    in_specs = [
        tok(d),
        pl.BlockSpec((None, halo, d), lambda bi, t: (bi, jnp.maximum(t * hb - 1, 0), 0)),
        pl.BlockSpec((None, halo, d), lambda bi, t: (bi, jnp.minimum((t + 1) * hb, n_halo_blocks - 1), 0)),
        pl.BlockSpec((None, None, None, 1, d6),
                     lambda bi, t: (layer, bi, jnp.where(t < n_ctx_tiles, 1, 0), 0, 0)),
        _const_spec((1, d)),
        _const_spec((d, 3 * GDN_WIDTH)),
        _const_spec(wts["w_main"].shape),
        _const_spec((1, MLA_Q_RANK)),
        _const_spec((1, MLA_KV_RANK)),
        _const_spec((MLA_Q_RANK, MLA_HEADS * LANES)),
        _const_spec((MLA_KV_RANK, MLA_HEADS * LANES)),
        _const_spec((MLA_KV_RANK, MLA_HEADS * LANES)),
        pl.BlockSpec((tm, LANES), lambda bi, t: (t, 0)),
        pl.BlockSpec((tm, LANES), lambda bi, t: (t, 0)),
        pl.BlockSpec((tm, LANES), lambda bi, t: (t, 0)),
        _const_spec((8, 3 * GDN_WIDTH)),
        _const_spec((1, LANES)),
        _const_spec((1, LANES)),
    ]
    head_t = pl.BlockSpec((None, MLA_HEADS, LANES, tm), lambda bi, t: (bi, 0, 0, t))
    out_specs = [
        head_t,
        tok(MLA_HEADS * LANES),
        head_t,
        tok(GDN_WIDTH), tok(GDN_WIDTH), tok(GDN_WIDTH),
        tok(LANES),
        tok(GDN_WIDTH),
    ]
    out_shape = [
        jax.ShapeDtypeStruct((b, MLA_HEADS, LANES, ta), BF16),
        jax.ShapeDtypeStruct((b, ta, MLA_HEADS * LANES), BF16),
        jax.ShapeDtypeStruct((b, MLA_HEADS, LANES, ta), BF16),
        jax.ShapeDtypeStruct((b, ta, GDN_WIDTH), BF16),
        jax.ShapeDtypeStruct((b, ta, GDN_WIDTH), BF16),
        jax.ShapeDtypeStruct((b, ta, GDN_WIDTH), BF16),
        jax.ShapeDtypeStruct((b, ta, LANES), F32),
        jax.ShapeDtypeStruct((b, ta, GDN_WIDTH), BF16),
    ]
    kern = functools.partial(_proj_kernel, tm=tm, n_ctx_tiles=n_ctx_tiles, n_tiles=n_tiles, d_model=d)
    return pl.pallas_call(
        kern,
        grid=(b, n_tiles),
        in_specs=in_specs,
        out_specs=out_specs,
        out_shape=out_shape,
        scratch_shapes=[pltpu.VMEM((tm + 2 * halo, d), BF16),
                        pltpu.VMEM((tm + 2 * halo, 3 * GDN_WIDTH), F32)],
        compiler_params=_params("parallel", "parallel"),
        name="proj",
    )(xs, xs, xs, mod, wts["norm1_g"], wts["w_qkv"], wts["w_main"], wts["q_a_g"], wts["kv_a_g"],
      wts["w_q"], wts["w_k"], wts["w_v"], tabs[0], tabs[1], tabs[2], wts["conv_w"], wts["a_log"], wts["dt_bias"])


def _attn_kernel(qt_ref, k_ref, vt_ref, o_ref, s_ref, *, n_ctx_q, tc, ta):
    qi = pl.program_id(1)
    kt = ATTN_KEY_TILE

    def score_tile(h, t, slot):
        s = _dot(k_ref[t * kt:(t + 1) * kt, h * LANES:(h + 1) * LANES], qt_ref[h])
        s_ref[slot, t * kt:(t + 1) * kt, :] = s
        return jnp.max(s, axis=0, keepdims=True)

    def pv_tile(h, t, slot, m):
        p = jnp.exp2(s_ref[slot, t * kt:(t + 1) * kt, :] - m).astype(BF16)
        return _dot(vt_ref[h, :, t * kt:(t + 1) * kt], p)

    def run(nk):
        nt = nk // kt
        outs = []
        m = functools.reduce(jnp.maximum, [score_tile(0, t, 0) for t in range(nt)])
        for h in range(MLA_HEADS):
            slot = h % 2
            acc = None
            m_parts = []
            for t in range(nt):
                if h + 1 < MLA_HEADS:
                    m_parts.append(score_tile(h + 1, t, 1 - slot))
                o = pv_tile(h, t, slot, m)
                acc = o if acc is None else acc + o
            outs.append(acc[0:MLA_V_DIM] / acc[MLA_V_DIM:MLA_V_DIM + 1])
            if m_parts:
                m = functools.reduce(jnp.maximum, m_parts)
        o_ref[...] = jnp.concatenate(outs, axis=0).T.astype(BF16)

    @pl.when(qi < n_ctx_q)
    def _():
        run(tc)

    @pl.when(qi >= n_ctx_q)
    def _():
        run(ta)


def _attn_call(qt, k, vt, tc):
    b, ta, _ = k.shape
    tq = TOKEN_TILE
    n_ctx_q = tc // tq
    nq = ta // tq
    kern = functools.partial(_attn_kernel, n_ctx_q=n_ctx_q, tc=tc, ta=ta)
    return pl.pallas_call(
        kern,
        grid=(b, nq),
        in_specs=[
            pl.BlockSpec((None, MLA_HEADS, LANES, tq), lambda bi, qi: (bi, 0, 0, qi)),
            pl.BlockSpec((None, ta, MLA_HEADS * LANES), lambda bi, qi: (bi, 0, 0)),
            pl.BlockSpec((None, MLA_HEADS, LANES, ta), lambda bi, qi: (bi, 0, 0, 0)),
        ],
        out_specs=pl.BlockSpec((None, tq, MLA_HEADS * MLA_V_DIM), lambda bi, qi: (bi, qi, 0)),
        out_shape=jax.ShapeDtypeStruct((b, nq * tq, MLA_HEADS * MLA_V_DIM), BF16),
        scratch_shapes=[pltpu.VMEM((2, ta, tq), F32)],
        compiler_params=_params("parallel", "arbitrary"),
        name="attn",
    )(qt, k, vt)


def _gdn_kernel(qf_ref, kf_ref, vf_ref, gf_ref, qb_ref, kb_ref, vb_ref, gbk_ref, cm_ref,
                of_ref, ob_ref, s_ref):
    j = pl.program_id(1)
    blk = GDN_BLOCK
    c = GDN_CHUNK
    n_chunks = blk // c

    @pl.when(j == 0)
    def _():
        s_ref[...] = jnp.zeros(s_ref.shape, F32)

    row = lax.broadcasted_iota(jnp.int32, (blk, blk), 0)
    col = lax.broadcasted_iota(jnp.int32, (blk, blk), 1)
    same = (row // c) == (col // c)
    eye = jnp.where(row == col, 1.0, 0.0)
    pair_masks = []
    size = 1
    while size < c:
        pair_masks.append(jnp.logical_and((row // (2 * size)) == (col // (2 * size)),
                                          (row // size) != (col // size)))
        size *= 2
    scale = GDN_HEAD_DIM ** -0.5

    dir_refs = ((qf_ref, kf_ref, vf_ref, gf_ref, of_ref), (qb_ref, kb_ref, vb_ref, gbk_ref, ob_ref))

    def st_setup(wave):
        bb = wave["bb"]
        chains = wave["chains"] = []
        for d in range(N_DIR):
            q_ref, k_ref, v_ref, g_ref, o_ref = (r.at[bb] for r in dir_refs[d])
            incl = jnp.logical_and(same, (row >= col) if d == 0 else (row <= col))
            strict = jnp.logical_and(same, (row > col) if d == 0 else (row < col))
            gb = g_ref[...]
            g1 = gb.astype(BF16)
            r1 = gb - g1.astype(F32)
            g2 = r1.astype(BF16)
            g3 = (r1 - g2.astype(F32)).astype(BF16)
            cmd = cm_ref[d]
            gc = _dot(cmd, g1) + _dot(cmd, g2) + _dot(cmd, g3)
            last = c - 1 if d == 0 else 0
            gtot = jnp.concatenate(
                [jnp.broadcast_to(gc[ci * c + last:ci * c + last + 1, :], (c, LANES)) for ci in range(n_chunks)],
                axis=0)
            gct = gc.T
            for h in range(GDN_HEADS):
                ln = d * GDN_HEADS + h
                hs = slice(h * LANES, (h + 1) * LANES)
                gcol = gc[:, ln:ln + 1]
                gl = gtot[:, ln:ln + 1]
                beta = gb[:, N_DIR * GDN_HEADS + ln:N_DIR * GDN_HEADS + ln + 1]
                decay = jnp.exp(jnp.where(incl, gcol - gct[ln:ln + 1, :], NEG_BIG))
                k16 = k_ref[:, hs]
                q16 = q_ref[:, hs]
                k = k16.astype(F32)
                kbeta = k * beta
                egc = jnp.exp(gcol)
                chains.append(dict(
                    d=d, hs=hs, o_ref=o_ref, idx=bb * N_DIR * GDN_HEADS + ln, strict=strict, decay=decay,
                    kb16=k16, kbeta16=kbeta.astype(BF16), q16=q16,
                    rhs16=jnp.concatenate([v_ref[:, hs].astype(F32) * beta, kbeta * egc], axis=1).astype(BF16),
                    qd=q16.astype(F32) * (egc * scale), kt=k * jnp.exp(gl - gcol), glast=jnp.exp(gl)))

    def st_gram(wave):
        for ch in wave["chains"]:
            gram = _dot_nt(ch["kbeta16"], ch["kb16"])
            lower = jnp.where(ch["strict"], gram * ch["decay"], 0.0)
            ch["lower16"] = lower.astype(BF16)
            ch["t16"] = (eye - jnp.where(pair_masks[0], lower, 0.0)).astype(BF16)

    def st_merge_a(pm16):
        def run(wave):
            for ch in wave["chains"]:
                ch["p16"] = _dot(ch["lower16"] * pm16, ch["t16"]).astype(BF16)
        return run

    def st_merge_b(wave):
        for ch in wave["chains"]:
            ch["t16"] = ch["t16"] - _dot(ch["t16"], ch["p16"]).astype(BF16)

    def st_apply(wave):
        for ch in wave["chains"]:
            x = _dot(ch["t16"], ch["rhs16"])
            ch["u"] = x[:, 0:LANES]
            ch["w"] = x[:, LANES:2 * LANES]
            ch["qk"] = _dot_nt(ch["q16"], ch["kb16"]) * (ch["decay"] * scale)

    def st_prep(wave):
        for ch in wave["chains"]:
            ch["s"] = s_ref[ch["idx"]]
            ch["wq16"] = [jnp.concatenate([ch["w"][ci * c:(ci + 1) * c], ch["qd"][ci * c:(ci + 1) * c]],
                                          axis=0).astype(BF16) for ci in range(n_chunks)]
            ch["ktt16"] = [ch["kt"][ci * c:(ci + 1) * c].T.astype(BF16) for ci in range(n_chunks)]
            ch["qk16"] = [ch["qk"][ci * c:(ci + 1) * c, ci * c:(ci + 1) * c].astype(BF16)
                          for ci in range(n_chunks)]

    def st_step_a(step):
        def run(wave):
            for ch in wave["chains"]:
                ci = ch["ci"] = step if ch["d"] == 0 else n_chunks - 1 - step
                ch["ws"] = _dot(ch["wq16"][ci], ch["s"].astype(BF16))
        return run

    def st_step_b(wave):
        for ch in wave["chains"]:
            ci, ws = ch["ci"], ch["ws"]
            r = slice(ci * c, (ci + 1) * c)
            vn16 = (ch["u"][r] - ws[0:c]).astype(BF16)
            o = ws[c:2 * c] + _dot(ch["qk16"][ci], vn16)
            glast = ch["glast"][ci * c:ci * c + 1, :]
            ch["s"] = ch["s"] * glast + _dot(ch["ktt16"][ci], vn16)
            ch["o_ref"][r, ch["hs"]] = o.astype(BF16)

    def st_store(wave):
        for ch in wave["chains"]:
            s_ref[ch["idx"]] = ch["s"]

    stages = [st_setup, st_gram]
    for pm in pair_masks[1:]:
        stages += [st_merge_a(jnp.where(pm, 1.0, 0.0).astype(BF16)), st_merge_b]
    stages += [st_apply, st_prep]
    for step in range(n_chunks):
        stages += [st_step_a(step), st_step_b]
    stages.append(st_store)

    waves = [dict(bb=bb) for bb in range(GDN_BATCH)]
    for tick in range(len(stages) + GDN_WAVE_LAG * (GDN_BATCH - 1)):
        for wi, wave in enumerate(waves):
            si = tick - GDN_WAVE_LAG * wi
            if 0 <= si < len(stages):
                stages[si](wave)


def _gdn_call(gq, gk, gv, gb, cm, n_ctx_blocks):
    b, ta, w = gq.shape
    blk = GDN_BLOCK
    nb = ta // blk

    def bwd_block(j):
        return jnp.where(j < n_ctx_blocks, n_ctx_blocks - 1 - j, nb - 1 - (j - n_ctx_blocks))

    gbt = GDN_BATCH
    assert b % gbt == 0
    f_spec = lambda wd: pl.BlockSpec((gbt, blk, wd), lambda bi, j: (bi, j, 0))
    b_spec = lambda wd: pl.BlockSpec((gbt, blk, wd), lambda bi, j: (bi, bwd_block(j), 0))
    return pl.pallas_call(
        _gdn_kernel,
        grid=(b // gbt, nb),
        in_specs=[f_spec(w), f_spec(w), f_spec(w), f_spec(LANES),
                  b_spec(w), b_spec(w), b_spec(w), b_spec(LANES),
                  _const_spec(cm.shape)],
        out_specs=[f_spec(w), b_spec(w)],
        out_shape=[jax.ShapeDtypeStruct((b, ta, w), BF16), jax.ShapeDtypeStruct((b, ta, w), BF16)],
        scratch_shapes=[pltpu.VMEM((gbt * N_DIR * GDN_HEADS, GDN_HEAD_DIM, GDN_HEAD_DIM), F32)],
        compiler_params=_params("parallel", "arbitrary"),
        name="gdn",
    )(gq, gk, gv, gb, gq, gk, gv, gb, cm)


def _out_kernel(x_ref, attn_ref, of_ref, ob_ref, zs_ref, mod_ref, gng_ref, wout_ref, n2g_ref,
                w1_ref, w2_ref, fng_ref, o_ref, *, d_model, final):
    mod = mod_ref[...]
    g1 = mod[:, 2 * d_model:3 * d_model]
    sh2 = mod[:, 3 * d_model:4 * d_model]
    sc2 = mod[:, 4 * d_model:5 * d_model]
    g2 = mod[:, 5 * d_model:6 * d_model]

    o = of_ref[...].astype(F32) + ob_ref[...].astype(F32)
    zs = zs_ref[...].astype(F32)
    gng = gng_ref[...]
    parts = [attn_ref[...]]
    for h in range(GDN_HEADS):
        hs = slice(h * LANES, (h + 1) * LANES)
        oh = o[:, hs]
        y = oh * lax.rsqrt(jnp.mean(oh * oh, axis=-1, keepdims=True) + NORM_EPS) * gng
        parts.append((y * zs[:, hs]).astype(BF16))
    mix = jnp.concatenate(parts, axis=1)
    x1 = x_ref[...] + g1 * _dot(mix, wout_ref[...])

    y2 = x1 * lax.rsqrt(jnp.mean(x1 * x1, axis=-1, keepdims=True) + NORM_EPS)
    h2 = ((y2 * n2g_ref[...]) * (1.0 + sc2) + sh2).astype(BF16)
    ff = jnp.maximum(_dot(h2, w1_ref[...]), 0.0)
    ff = (ff * ff).astype(BF16)
    x2 = x1 + g2 * _dot(ff, w2_ref[...])
    if final:
        x2 = x2 * lax.rsqrt(jnp.mean(x2 * x2, axis=-1, keepdims=True) + NORM_EPS) * fng_ref[...]
    o_ref[...] = x2


def _out_call(xs, attn, o_f, o_b, z, mod, layer, wts, fng, n_ctx_tiles, final):
    b, ta, d = xs.shape
    tm = TOKEN_TILE
    d6 = 6 * d
    first = n_ctx_tiles if final else 0
    n_tiles = ta // tm - first
    tok = lambda w: pl.BlockSpec((None, tm, w), lambda bi, t: (bi, t + first, 0))
    kern = functools.partial(_out_kernel, d_model=d, final=final)
    out_rows = n_tiles * tm
    return pl.pallas_call(
        kern,
        grid=(b, n_tiles),
        in_specs=[
            tok(d),
            tok(MLA_HEADS * MLA_V_DIM),
            tok(GDN_WIDTH), tok(GDN_WIDTH), tok(GDN_WIDTH),
            pl.BlockSpec((None, None, None, 1, d6),
                         lambda bi, t: (layer, bi, jnp.where(t + first < n_ctx_tiles, 1, 0), 0, 0)),
            _const_spec((1, GDN_HEAD_DIM)),
            _const_spec(wts["w_out"].shape),
            _const_spec((1, d)),
            _const_spec(wts["w_ff1"].shape),
            _const_spec(wts["w_ff2"].shape),
            _const_spec((1, d)),
        ],
        out_specs=pl.BlockSpec((None, tm, d), lambda bi, t: (bi, t, 0)),
        out_shape=jax.ShapeDtypeStruct((b, out_rows, d), F32),
        compiler_params=_params("parallel", "parallel"),
        name="out_final" if final else "out",
    )(xs, attn, o_f, o_b, z, mod, wts["gdn_norm_g"], wts["w_out"], wts["norm2_g"],
      wts["w_ff1"], wts["w_ff2"], fng)


def _rope_tables(t_lat, t_ctx):
    rows = t_lat // GRID_W
    row = jnp.broadcast_to(jnp.arange(rows)[:, None], (rows, GRID_W)).reshape(-1).astype(F32)
    col = jnp.broadcast_to(jnp.arange(GRID_W)[None, :], (rows, GRID_W)).reshape(-1).astype(F32)
    axis_pairs = MLA_ROPE_DIM // 4
    inv_freq = ROPE_THETA ** (-jnp.arange(axis_pairs, dtype=F32) / axis_pairs)
    ang = jnp.concatenate([row[:, None] * inv_freq, col[:, None] * inv_freq], axis=-1)
    cos, sin = jnp.cos(ang), jnp.sin(ang)
    half = MLA_ROPE_DIM // 2
    lo, mid, hi = MLA_NOPE_DIM, MLA_NOPE_DIM + half, MLA_NOPE_DIM + 2 * half
    ta = t_ctx + t_lat
    ctab = jnp.ones((ta, LANES), F32).at[t_ctx:, lo:mid].set(cos).at[t_ctx:, mid:hi].set(cos)
    sa = jnp.zeros((ta, LANES), F32).at[t_ctx:, lo:mid].set(-sin)
    sb = jnp.zeros((ta, LANES), F32).at[t_ctx:, mid:hi].set(sin)
    return ctab, sa, sb


def _chunk_sum_matrices():
    i = jnp.arange(GDN_BLOCK)
    same = (i[:, None] // GDN_CHUNK) == (i[None, :] // GDN_CHUNK)
    lower = same & (i[:, None] >= i[None, :])
    upper = same & (i[:, None] <= i[None, :])
    return jnp.stack([lower, upper], axis=0).astype(BF16)


def _pad_heads(w, used):
    k = w.shape[0]
    w = w.reshape(k, MLA_HEADS, used)
    return jnp.pad(w, ((0, 0), (0, 0), (0, LANES - used))).reshape(k, MLA_HEADS * LANES)


def _layer_weights(i, norm1_g, w_in, q_a_g, w_q_b, kv_a_g, w_kv_b, conv_w, a_log, dt_bias, gdn_norm_g,
                   w_out, norm2_g, w_ff1, w_ff2):
    d = w_in.shape[1]
    wi = w_in[i]
    o = 0
    cols = []
    for sz in (MLA_Q_RANK, MLA_KV_RANK, MLA_ROPE_DIM, 3 * GDN_WIDTH, GDN_WIDTH,
               N_DIR * GDN_HEADS, N_DIR * GDN_HEADS):
        cols.append(wi[:, o:o + sz])
        o += sz
    w_qa, w_kva, w_kr, w_qkv, w_z, w_a, w_b = cols
    w_kr_pad = jnp.zeros((d, LANES), F32).at[:, MLA_NOPE_DIM:MLA_QK_DIM].set(w_kr)
    nab = N_DIR * GDN_HEADS
    w_ab_pad = jnp.zeros((d, LANES), F32).at[:, 0:nab].set(w_a).at[:, nab:2 * nab].set(w_b)
    w_main = jnp.concatenate([w_qa, w_kva, w_kr_pad, w_z, w_ab_pad], axis=1)
    kv = w_kv_b[i].reshape(MLA_KV_RANK, MLA_HEADS, MLA_NOPE_DIM + MLA_V_DIM)
    w_k = _pad_heads(kv[:, :, :MLA_NOPE_DIM].reshape(MLA_KV_RANK, -1), MLA_NOPE_DIM)
    w_v = _pad_heads(kv[:, :, MLA_NOPE_DIM:].reshape(MLA_KV_RANK, -1), MLA_V_DIM)
    row = lambda v: jnp.zeros((1, LANES), F32).at[0, 0:nab].set(v.reshape(-1))
    return dict(
        norm1_g=norm1_g[i][None, :],
        w_qkv=w_qkv.astype(BF16),
        w_main=w_main.astype(BF16),
        q_a_g=q_a_g[i][None, :],
        kv_a_g=kv_a_g[i][None, :],
        w_q=_pad_heads(w_q_b[i], MLA_QK_DIM).astype(BF16),
        w_k=w_k.astype(BF16),
        w_v=w_v.astype(BF16),
        conv_w=jnp.pad(conv_w[i], ((0, 8 - GDN_CONV), (0, 0))),
        a_log=row(a_log[i]),
        dt_bias=row(dt_bias[i]),
        gdn_norm_g=gdn_norm_g[i][None, :],
        w_out=w_out[i].astype(BF16),
        norm2_g=norm2_g[i][None, :],
        w_ff1=w_ff1[i].astype(BF16),
        w_ff2=w_ff2[i].astype(BF16),
    )


def kernel(x, c, ctx, c_ctx, w_ada, b_ada, norm1_g, w_in, q_a_g, w_q_b, kv_a_g, w_kv_b, conv_w, a_log,
           dt_bias, gdn_norm_g, w_out, norm2_g, w_ff1, w_ff2, final_norm_g):
    b, t_lat, d = x.shape
    t_ctx = ctx.shape[1]
    depth = w_ada.shape[0]
    assert t_ctx % TOKEN_TILE == 0 and t_lat % TOKEN_TILE == 0 and t_lat % GRID_W == 0
    assert b + 1 <= PAD_ROWS
    n_ctx_tiles = t_ctx // TOKEN_TILE

    xs = jnp.concatenate([ctx, x], axis=1)
    cc = jnp.zeros((PAD_ROWS, d), F32).at[0:b].set(c).at[b].set(c_ctx)
    mods = _ada_call(cc, w_ada, b_ada)
    mod = jnp.stack([mods[:, 0:b], jnp.broadcast_to(mods[:, b:b + 1], (depth, b, 6 * d))], axis=2)
    mod = mod[:, :, :, None, :]
    tabs = _rope_tables(t_lat, t_ctx)
    cm = _chunk_sum_matrices()
    fng = final_norm_g[None, :]

    for i in range(depth):
        last = i == depth - 1
        wts = _layer_weights(i, norm1_g, w_in, q_a_g, w_q_b, kv_a_g, w_kv_b, conv_w, a_log, dt_bias,
                             gdn_norm_g, w_out, norm2_g, w_ff1, w_ff2)
        qt, k, vt, gq, gk, gv, gb, z = _proj_call(xs, mod, i, wts, tabs, n_ctx_tiles)
        attn = _attn_call(qt, k, vt, t_ctx)
        o_f, o_b = _gdn_call(gq, gk, gv, gb, cm, t_ctx // GDN_BLOCK)
        xs = _out_call(xs, attn, o_f, o_b, z, mod, i, wts, fng, n_ctx_tiles, last)
    return xs
```

```python
import functools

import jax
import jax.numpy as jnp
from jax import lax
from jax.experimental import pallas as pl
from jax.experimental.pallas import tpu as pltpu

F32 = jnp.float32
BF16 = jnp.bfloat16

GRID_W = 64
MLA_HEADS = 8
MLA_NOPE_DIM = 64
MLA_ROPE_DIM = 32
MLA_V_DIM = 64
MLA_Q_RANK = 256
MLA_KV_RANK = 128
MLA_QK_DIM = MLA_NOPE_DIM + MLA_ROPE_DIM
GDN_HEADS = 4
GDN_HEAD_DIM = 128
GDN_WIDTH = GDN_HEADS * GDN_HEAD_DIM
GDN_CONV = 5
GDN_CHUNK = 64
N_DIR = 2
ROPE_THETA = 10000.0
NORM_EPS = 1e-6
LOG2_E = 1.4426950408889634

LANES = 128
BF16_SUBLANES = 16
VMEM_LIMIT_BYTES = 56 * 1024 * 1024

TOKEN_TILE = 256
GDN_BLOCK = 256
GDN_BATCH = 2
GDN_WAVE_LAG = 0
ATTN_KEY_TILE = 256
NEG_BIG = -1e30
PAD_ROWS = 16


def _sigmoid(x):
    return 1.0 / (1.0 + jnp.exp(-x))


def _silu(x):
    return x * _sigmoid(x)


def _dot(a, b):
    return jnp.dot(a, b, preferred_element_type=F32)


def _dot_nt(a, b):
    return lax.dot_general(a, b, (((1,), (1,)), ((), ())), preferred_element_type=F32)


def _dot_tn(a, b):
    return lax.dot_general(a, b, (((0,), (0,)), ((), ())), preferred_element_type=F32)


def _const_spec(shape):
    nd = len(shape)
    return pl.BlockSpec(shape, lambda *_: (0,) * nd, pipeline_mode=pl.Buffered(1))


def _params(*sem):
    return pltpu.CompilerParams(dimension_semantics=sem, vmem_limit_bytes=VMEM_LIMIT_BYTES)


def _ada_kernel(c_ref, w_ref, b_ref, o_ref):
    s = _silu(c_ref[...]).astype(BF16)
    o_ref[0] = _dot(s, w_ref[0].astype(BF16)) + b_ref[0]


def _ada_call(cc, w_ada, b_ada):
    n_layers, d, d6 = w_ada.shape
    tn = 1536
    return pl.pallas_call(
        _ada_kernel,
        grid=(n_layers, d6 // tn),
        in_specs=[
            pl.BlockSpec((PAD_ROWS, d), lambda l, n: (0, 0)),
            pl.BlockSpec((1, d, tn), lambda l, n: (l, 0, n)),
            pl.BlockSpec((1, 1, tn), lambda l, n: (l, 0, n)),
        ],
        out_specs=pl.BlockSpec((1, PAD_ROWS, tn), lambda l, n: (l, 0, n)),
        out_shape=jax.ShapeDtypeStruct((n_layers, PAD_ROWS, d6), F32),
        compiler_params=_params("parallel", "parallel"),
        name="ada",
    )(cc, w_ada, b_ada.reshape(n_layers, 1, d6))


def _rope(x, c, sa, sb):
    return x * c + pltpu.roll(x, LANES - 16, 1) * sa + pltpu.roll(x, 16, 1) * sb


def _proj_kernel(xc_ref, xp_ref, xn_ref, mod_ref, n1g_ref, wqkv_ref, wmain_ref, qag_ref, kvag_ref,
                 wq_ref, wk_ref, wv_ref, ctab_ref, sa_ref, sb_ref, convw_ref, alog_ref, dtb_ref,
                 qt_ref, k_ref, vt_ref, gq_ref, gk_ref, gv_ref, gb_ref, zs_ref,
                 hext_ref, pqkv_ref, *, tm, n_ctx_tiles, n_tiles, d_model):
    t = pl.program_id(1)
    halo = BF16_SUBLANES
    mod = mod_ref[...]
    sh1 = mod[:, 0:d_model]
    sc1 = mod[:, d_model:2 * d_model]
    n1g = n1g_ref[...]

    def modulate(x):
        y = x * lax.rsqrt(jnp.mean(x * x, axis=-1, keepdims=True) + NORM_EPS)
        return ((y * n1g) * (1.0 + sc1) + sh1).astype(BF16)

    hext_ref[0:halo, :] = modulate(xp_ref[...])
    hext_ref[halo:halo + tm, :] = modulate(xc_ref[...])
    hext_ref[halo + tm:, :] = modulate(xn_ref[...])

    pqkv_ref[...] = _dot(hext_ref[...], wqkv_ref[...])
    seg_first = jnp.logical_or(t == 0, t == n_ctx_tiles)
    seg_last = jnp.logical_or(t == n_ctx_tiles - 1, t == n_tiles - 1)

    @pl.when(seg_first)
    def _():
        pqkv_ref[0:halo, :] = jnp.zeros((halo, 3 * GDN_WIDTH), F32)

    @pl.when(seg_last)
    def _():
        pqkv_ref[halo + tm:, :] = jnp.zeros((halo, 3 * GDN_WIDTH), F32)

    pm = _dot(hext_ref[halo:halo + tm, :], wmain_ref[...])
    o_kva = MLA_Q_RANK
    o_kr = o_kva + MLA_KV_RANK
    o_z = o_kr + LANES
    o_ab = o_z + GDN_WIDTH
    qa = pm[:, 0:o_kva]
    kva = pm[:, o_kva:o_kr]
    kr = pm[:, o_kr:o_z]
    zs_ref[...] = _silu(pm[:, o_z:o_ab]).astype(BF16)
    ab = pm[:, o_ab:o_ab + LANES]

    ctab = ctab_ref[...]
    sa = sa_ref[...]
    sb = sb_ref[...]

    qn = (qa * lax.rsqrt(jnp.mean(qa * qa, axis=-1, keepdims=True) + NORM_EPS) * qag_ref[...]).astype(BF16)
    q = _dot(qn, wq_ref[...])
    scale = MLA_QK_DIM ** -0.5 * LOG2_E
    for h in range(MLA_HEADS):
        qh = q[:, h * LANES:(h + 1) * LANES]
        qt_ref[h] = (_rope(qh, ctab, sa, sb) * scale).T.astype(BF16)

    kvn = (kva * lax.rsqrt(jnp.mean(kva * kva, axis=-1, keepdims=True) + NORM_EPS) * kvag_ref[...]).astype(BF16)
    kk = _dot(kvn, wk_ref[...])
    kpe = _rope(kr, ctab, sa, sb)
    for h in range(MLA_HEADS):
        k_ref[:, h * LANES:(h + 1) * LANES] = (kk[:, h * LANES:(h + 1) * LANES] + kpe).astype(BF16)
    vv = _dot(kvn, wv_ref[...])
    lane = lax.broadcasted_iota(jnp.int32, (tm, LANES), 1)
    for h in range(MLA_HEADS):
        vh = jnp.where(lane == MLA_V_DIM, 1.0, vv[:, h * LANES:(h + 1) * LANES])
        vt_ref[h] = vh.T.astype(BF16)

    cw = convw_ref[...]
    base = halo - GDN_CONV // 2
    for cb in range(3 * GDN_HEADS):
        cs = slice(cb * LANES, (cb + 1) * LANES)
        acc = pqkv_ref[base:base + tm, cs] * cw[0:1, cs]
        for j in range(1, GDN_CONV):
            acc = acc + pqkv_ref[base + j:base + j + tm, cs] * cw[j:j + 1, cs]
        act = _silu(acc)
        grp, hh = divmod(cb, GDN_HEADS)
        hs = slice(hh * LANES, (hh + 1) * LANES)
        if grp < 2:
            act = act * lax.rsqrt(jnp.sum(act * act, axis=-1, keepdims=True) + NORM_EPS)
            (gq_ref if grp == 0 else gk_ref)[:, hs] = act.astype(BF16)
        else:
            gv_ref[:, hs] = act.astype(BF16)

    x = ab + dtb_ref[...]
    softplus = jnp.maximum(x, 0.0) + jnp.log(1.0 + jnp.exp(-jnp.abs(x)))
    gval = -jnp.exp(alog_ref[...]) * softplus
    lane_ab = lax.broadcasted_iota(jnp.int32, ab.shape, 1)
    gb_ref[...] = jnp.where(lane_ab < N_DIR * GDN_HEADS, gval, _sigmoid(ab))


def _proj_call(xs, mod, layer, wts, tabs, n_ctx_tiles):
    b, ta, d = xs.shape
    tm = TOKEN_TILE
    n_tiles = ta // tm
    halo = BF16_SUBLANES
    hb = tm // halo
    n_halo_blocks = ta // halo
    d6 = 6 * d

    tok = lambda w: pl.BlockSpec((None, tm, w), lambda bi, t: (bi, t, 0))
    in_specs = [
        tok(d),
        pl.BlockSpec((None, halo, d), lambda bi, t: (bi, jnp.maximum(t * hb - 1, 0), 0)),
        pl.BlockSpec((None, halo, d), lambda bi, t: (bi, jnp.minimum((t + 1) * hb, n_halo_blocks - 1), 0)),
        pl.BlockSpec((None, None, None, 1, d6),
                     lambda bi, t: (layer, bi, jnp.where(t < n_ctx_tiles, 1, 0), 0, 0)),
        _const_spec((1, d)),
        _const_spec((d, 3 * GDN_WIDTH)),
        _const_spec(wts["w_main"].shape),
        _const_spec((1, MLA_Q_RANK)),
        _const_spec((1, MLA_KV_RANK)),
        _const_spec((MLA_Q_RANK, MLA_HEADS * LANES)),
        _const_spec((MLA_KV_RANK, MLA_HEADS * LANES)),
        _const_spec((MLA_KV_RANK, MLA_HEADS * LANES)),
        pl.BlockSpec((tm, LANES), lambda bi, t: (t, 0)),
        pl.BlockSpec((tm, LANES), lambda bi, t: (t, 0)),
        pl.BlockSpec((tm, LANES), lambda bi, t: (t, 0)),
        _const_spec((8, 3 * GDN_WIDTH)),
        _const_spec((1, LANES)),
        _const_spec((1, LANES)),
    ]
    head_t = pl.BlockSpec((None, MLA_HEADS, LANES, tm), lambda bi, t: (bi, 0, 0, t))
    out_specs = [
        head_t,
        tok(MLA_HEADS * LANES),
        head_t,
        tok(GDN_WIDTH), tok(GDN_WIDTH), tok(GDN_WIDTH),
        tok(LANES),
        tok(GDN_WIDTH),
    ]
    out_shape = [
        jax.ShapeDtypeStruct((b, MLA_HEADS, LANES, ta), BF16),
        jax.ShapeDtypeStruct((b, ta, MLA_HEADS * LANES), BF16),
        jax.ShapeDtypeStruct((b, MLA_HEADS, LANES, ta), BF16),
        jax.ShapeDtypeStruct((b, ta, GDN_WIDTH), BF16),
        jax.ShapeDtypeStruct((b, ta, GDN_WIDTH), BF16),
        jax.ShapeDtypeStruct((b, ta, GDN_WIDTH), BF16),
        jax.ShapeDtypeStruct((b, ta, LANES), F32),
        jax.ShapeDtypeStruct((b, ta, GDN_WIDTH), BF16),
    ]
    kern = functools.partial(_proj_kernel, tm=tm, n_ctx_tiles=n_ctx_tiles, n_tiles=n_tiles, d_model=d)
    return pl.pallas_call(
        kern,
        grid=(b, n_tiles),
        in_specs=in_specs,
        out_specs=out_specs,
        out_shape=out_shape,
        scratch_shapes=[pltpu.VMEM((tm + 2 * halo, d), BF16),
                        pltpu.VMEM((tm + 2 * halo, 3 * GDN_WIDTH), F32)],
        compiler_params=_params("parallel", "parallel"),
        name="proj",
    )(xs, xs, xs, mod, wts["norm1_g"], wts["w_qkv"], wts["w_main"], wts["q_a_g"], wts["kv_a_g"],
      wts["w_q"], wts["w_k"], wts["w_v"], tabs[0], tabs[1], tabs[2], wts["conv_w"], wts["a_log"], wts["dt_bias"])


def _attn_kernel(qt_ref, k_ref, vt_ref, o_ref, s_ref, *, n_ctx_q, tc, ta):
    qi = pl.program_id(1)
    kt = ATTN_KEY_TILE

    def score_tile(h, t, slot):
        s = _dot(k_ref[t * kt:(t + 1) * kt, h * LANES:(h + 1) * LANES], qt_ref[h])
        s_ref[slot, t * kt:(t + 1) * kt, :] = s
        return jnp.max(s, axis=0, keepdims=True)

    def pv_tile(h, t, slot, m):
        p = jnp.exp2(s_ref[slot, t * kt:(t + 1) * kt, :] - m).astype(BF16)
        return _dot(vt_ref[h, :, t * kt:(t + 1) * kt], p)

    def run(nk):
        nt = nk // kt
        outs = []
        m = functools.reduce(jnp.maximum, [score_tile(0, t, 0) for t in range(nt)])
        for h in range(MLA_HEADS):
            slot = h % 2
            acc = None
            m_parts = []
            for t in range(nt):
                if h + 1 < MLA_HEADS:
                    m_parts.append(score_tile(h + 1, t, 1 - slot))
                o = pv_tile(h, t, slot, m)
                acc = o if acc is None else acc + o
            outs.append(acc[0:MLA_V_DIM] / acc[MLA_V_DIM:MLA_V_DIM + 1])
            if m_parts:
                m = functools.reduce(jnp.maximum, m_parts)
        o_ref[...] = jnp.concatenate(outs, axis=0).T.astype(BF16)

    @pl.when(qi < n_ctx_q)
    def _():
        run(tc)

    @pl.when(qi >= n_ctx_q)
    def _():
        run(ta)


def _attn_call(qt, k, vt, tc):
    b, ta, _ = k.shape
    tq = TOKEN_TILE
    n_ctx_q = tc // tq
    nq = ta // tq
    kern = functools.partial(_attn_kernel, n_ctx_q=n_ctx_q, tc=tc, ta=ta)
    return pl.pallas_call(
        kern,
        grid=(b, nq),
        in_specs=[
            pl.BlockSpec((None, MLA_HEADS, LANES, tq), lambda bi, qi: (bi, 0, 0, qi)),
            pl.BlockSpec((None, ta, MLA_HEADS * LANES), lambda bi, qi: (bi, 0, 0)),
            pl.BlockSpec((None, MLA_HEADS, LANES, ta), lambda bi, qi: (bi, 0, 0, 0)),
        ],
        out_specs=pl.BlockSpec((None, tq, MLA_HEADS * MLA_V_DIM), lambda bi, qi: (bi, qi, 0)),
        out_shape=jax.ShapeDtypeStruct((b, nq * tq, MLA_HEADS * MLA_V_DIM), BF16),
        scratch_shapes=[pltpu.VMEM((2, ta, tq), F32)],
        compiler_params=_params("parallel", "arbitrary"),
        name="attn",
    )(qt, k, vt)


def _gdn_kernel(qf_ref, kf_ref, vf_ref, gf_ref, qb_ref, kb_ref, vb_ref, gbk_ref, cm_ref,
                of_ref, ob_ref, s_ref):
    j = pl.program_id(1)
    blk = GDN_BLOCK
    c = GDN_CHUNK
    n_chunks = blk // c

    @pl.when(j == 0)
    def _():
        s_ref[...] = jnp.zeros(s_ref.shape, F32)

    row = lax.broadcasted_iota(jnp.int32, (blk, blk), 0)
    col = lax.broadcasted_iota(jnp.int32, (blk, blk), 1)
    same = (row // c) == (col // c)
    eye = jnp.where(row == col, 1.0, 0.0)
    pair_masks = []
    size = 1
    while size < c:
        pair_masks.append(jnp.logical_and((row // (2 * size)) == (col // (2 * size)),
                                          (row // size) != (col // size)))
        size *= 2
    scale = GDN_HEAD_DIM ** -0.5

    dir_refs = ((qf_ref, kf_ref, vf_ref, gf_ref, of_ref), (qb_ref, kb_ref, vb_ref, gbk_ref, ob_ref))

    def st_setup(wave):
        bb = wave["bb"]
        chains = wave["chains"] = []
        for d in range(N_DIR):
            q_ref, k_ref, v_ref, g_ref, o_ref = (r.at[bb] for r in dir_refs[d])
            incl = jnp.logical_and(same, (row >= col) if d == 0 else (row <= col))
            strict = jnp.logical_and(same, (row > col) if d == 0 else (row < col))
            gb = g_ref[...]
            g1 = gb.astype(BF16)
            r1 = gb - g1.astype(F32)
            g2 = r1.astype(BF16)
            g3 = (r1 - g2.astype(F32)).astype(BF16)
            cmd = cm_ref[d]
            gc = _dot(cmd, g1) + _dot(cmd, g2) + _dot(cmd, g3)
            last = c - 1 if d == 0 else 0
            gtot = jnp.concatenate(
                [jnp.broadcast_to(gc[ci * c + last:ci * c + last + 1, :], (c, LANES)) for ci in range(n_chunks)],
                axis=0)
            gct = gc.T
            for h in range(GDN_HEADS):
                ln = d * GDN_HEADS + h
                hs = slice(h * LANES, (h + 1) * LANES)
                gcol = gc[:, ln:ln + 1]
                gl = gtot[:, ln:ln + 1]
                beta = gb[:, N_DIR * GDN_HEADS + ln:N_DIR * GDN_HEADS + ln + 1]
                decay = jnp.exp(jnp.where(incl, gcol - gct[ln:ln + 1, :], NEG_BIG))
                k16 = k_ref[:, hs]
                q16 = q_ref[:, hs]
                k = k16.astype(F32)
                kbeta = k * beta
                egc = jnp.exp(gcol)
                chains.append(dict(
                    d=d, hs=hs, o_ref=o_ref, idx=bb * N_DIR * GDN_HEADS + ln, strict=strict, decay=decay,
                    kb16=k16, kbeta16=kbeta.astype(BF16), q16=q16,
                    rhs16=jnp.concatenate([v_ref[:, hs].astype(F32) * beta, kbeta * egc], axis=1).astype(BF16),
                    qd=q16.astype(F32) * (egc * scale), kt=k * jnp.exp(gl - gcol), glast=jnp.exp(gl)))

    def st_gram(wave):
        for ch in wave["chains"]:
            gram = _dot_nt(ch["kbeta16"], ch["kb16"])
            lower = jnp.where(ch["strict"], gram * ch["decay"], 0.0)
            ch["lower16"] = lower.astype(BF16)
            ch["t16"] = (eye - jnp.where(pair_masks[0], lower, 0.0)).astype(BF16)

    def st_merge_a(pm16):
        def run(wave):
            for ch in wave["chains"]:
                ch["p16"] = _dot(ch["lower16"] * pm16, ch["t16"]).astype(BF16)
        return run

    def st_merge_b(wave):
        for ch in wave["chains"]:
            ch["t16"] = ch["t16"] - _dot(ch["t16"], ch["p16"]).astype(BF16)

    def half_rows(x, s, d):
        x3 = x.reshape(blk // (2 * s), 2 * s, x.shape[1])
        return (x3[:, s:, :] if d == 0 else x3[:, :s, :]).reshape(blk // 2, x.shape[1])

    def st_merge_half_a(pm16, s):
        pm_half = [half_rows(pm16, s, d) for d in range(N_DIR)]

        def run(wave):
            for ch in wave["chains"]:
                e_half = half_rows(ch["lower16"], s, ch["d"]) * pm_half[ch["d"]]
                ch["p_half16"] = _dot(e_half, ch["t16"]).astype(BF16)
        return run

    def st_merge_half_b(s):
        def run(wave):
            n_pairs = blk // (2 * s)
            for ch in wave["chains"]:
                t3 = ch["t16"].reshape(n_pairs, 2 * s, blk)
                p3 = ch["p_half16"].reshape(n_pairs, s, blk)
                zero = jnp.zeros_like(p3)
                if ch["d"] == 0:
                    p_full = jnp.concatenate([zero, p3], axis=1).reshape(blk, blk)
                    r = _dot(t3[:, s:, :].reshape(blk // 2, blk), p_full).astype(BF16)
                    parts = [t3[:, :s, :], t3[:, s:, :] - r.reshape(n_pairs, s, blk)]
                else:
                    p_full = jnp.concatenate([p3, zero], axis=1).reshape(blk, blk)
                    r = _dot(t3[:, :s, :].reshape(blk // 2, blk), p_full).astype(BF16)
                    parts = [t3[:, :s, :] - r.reshape(n_pairs, s, blk), t3[:, s:, :]]
                ch["t16"] = jnp.concatenate(parts, axis=1).reshape(blk, blk)
        return run

    def st_apply(wave):
        for ch in wave["chains"]:
            x = _dot(ch["t16"], ch["rhs16"])
            ch["u"] = x[:, 0:LANES]
            ch["w"] = x[:, LANES:2 * LANES]
            ch["qk"] = _dot_nt(ch["q16"], ch["kb16"]) * (ch["decay"] * scale)

    def st_prep(wave):
        for ch in wave["chains"]:
            ch["s"] = s_ref[ch["idx"]]
            ch["wq16"] = [jnp.concatenate([ch["w"][ci * c:(ci + 1) * c], ch["qd"][ci * c:(ci + 1) * c]],
                                          axis=0).astype(BF16) for ci in range(n_chunks)]
            ch["ktt16"] = [ch["kt"][ci * c:(ci + 1) * c].T.astype(BF16) for ci in range(n_chunks)]
            ch["qk16"] = [ch["qk"][ci * c:(ci + 1) * c, ci * c:(ci + 1) * c].astype(BF16)
                          for ci in range(n_chunks)]

    def st_step_a(step):
        def run(wave):
            for ch in wave["chains"]:
                ci = ch["ci"] = step if ch["d"] == 0 else n_chunks - 1 - step
                ch["ws"] = _dot(ch["wq16"][ci], ch["s"].astype(BF16))
        return run

    def st_step_b(wave):
        for ch in wave["chains"]:
            ci, ws = ch["ci"], ch["ws"]
            r = slice(ci * c, (ci + 1) * c)
            vn16 = (ch["u"][r] - ws[0:c]).astype(BF16)
            o = ws[c:2 * c] + _dot(ch["qk16"][ci], vn16)
            glast = ch["glast"][ci * c:ci * c + 1, :]
            ch["s"] = ch["s"] * glast + _dot(ch["ktt16"][ci], vn16)
            ch["o_ref"][r, ch["hs"]] = o.astype(BF16)

    def st_store(wave):
        for ch in wave["chains"]:
            s_ref[ch["idx"]] = ch["s"]

    stages = [st_setup, st_gram]
    for lvl, pm in enumerate(pair_masks[1:], start=1):
        pm16 = jnp.where(pm, 1.0, 0.0).astype(BF16)
        s = 2 ** lvl
        if s >= BF16_SUBLANES:
            stages += [st_merge_half_a(pm16, s), st_merge_half_b(s)]
        else:
            stages += [st_merge_a(pm16), st_merge_b]
    stages += [st_apply, st_prep]
    for step in range(n_chunks):
        stages += [st_step_a(step), st_step_b]
    stages.append(st_store)

    waves = [dict(bb=bb) for bb in range(GDN_BATCH)]
    for tick in range(len(stages) + GDN_WAVE_LAG * (GDN_BATCH - 1)):
        for wi, wave in enumerate(waves):
            si = tick - GDN_WAVE_LAG * wi
            if 0 <= si < len(stages):
                stages[si](wave)


def _gdn_call(gq, gk, gv, gb, cm, n_ctx_blocks):
    b, ta, w = gq.shape
    blk = GDN_BLOCK
    nb = ta // blk

    def bwd_block(j):
        return jnp.where(j < n_ctx_blocks, n_ctx_blocks - 1 - j, nb - 1 - (j - n_ctx_blocks))

    gbt = GDN_BATCH
    assert b % gbt == 0
    f_spec = lambda wd: pl.BlockSpec((gbt, blk, wd), lambda bi, j: (bi, j, 0))
    b_spec = lambda wd: pl.BlockSpec((gbt, blk, wd), lambda bi, j: (bi, bwd_block(j), 0))
    return pl.pallas_call(
        _gdn_kernel,
        grid=(b // gbt, nb),
        in_specs=[f_spec(w), f_spec(w), f_spec(w), f_spec(LANES),
                  b_spec(w), b_spec(w), b_spec(w), b_spec(LANES),
                  _const_spec(cm.shape)],
        out_specs=[f_spec(w), b_spec(w)],
        out_shape=[jax.ShapeDtypeStruct((b, ta, w), BF16), jax.ShapeDtypeStruct((b, ta, w), BF16)],
        scratch_shapes=[pltpu.VMEM((gbt * N_DIR * GDN_HEADS, GDN_HEAD_DIM, GDN_HEAD_DIM), F32)],
        compiler_params=_params("parallel", "arbitrary"),
        name="gdn",
    )(gq, gk, gv, gb, gq, gk, gv, gb, cm)


def _out_kernel(x_ref, attn_ref, of_ref, ob_ref, zs_ref, mod_ref, gng_ref, wout_ref, n2g_ref,
                w1_ref, w2_ref, fng_ref, o_ref, *, d_model, final):
    mod = mod_ref[...]
    g1 = mod[:, 2 * d_model:3 * d_model]
    sh2 = mod[:, 3 * d_model:4 * d_model]
    sc2 = mod[:, 4 * d_model:5 * d_model]
    g2 = mod[:, 5 * d_model:6 * d_model]

    o = of_ref[...].astype(F32) + ob_ref[...].astype(F32)
    zs = zs_ref[...].astype(F32)
    gng = gng_ref[...]
    parts = [attn_ref[...]]
    for h in range(GDN_HEADS):
        hs = slice(h * LANES, (h + 1) * LANES)
        oh = o[:, hs]
        y = oh * lax.rsqrt(jnp.mean(oh * oh, axis=-1, keepdims=True) + NORM_EPS) * gng
        parts.append((y * zs[:, hs]).astype(BF16))
    mix = jnp.concatenate(parts, axis=1)
    x1 = x_ref[...] + g1 * _dot(mix, wout_ref[...])

    y2 = x1 * lax.rsqrt(jnp.mean(x1 * x1, axis=-1, keepdims=True) + NORM_EPS)
    h2 = ((y2 * n2g_ref[...]) * (1.0 + sc2) + sh2).astype(BF16)
    ff = jnp.maximum(_dot(h2, w1_ref[...]), 0.0)
    ff = (ff * ff).astype(BF16)
    x2 = x1 + g2 * _dot(ff, w2_ref[...])
    if final:
        x2 = x2 * lax.rsqrt(jnp.mean(x2 * x2, axis=-1, keepdims=True) + NORM_EPS) * fng_ref[...]
    o_ref[...] = x2


def _out_call(xs, attn, o_f, o_b, z, mod, layer, wts, fng, n_ctx_tiles, final):
    b, ta, d = xs.shape
    tm = TOKEN_TILE
    d6 = 6 * d
    first = n_ctx_tiles if final else 0
    n_tiles = ta // tm - first
    tok = lambda w: pl.BlockSpec((None, tm, w), lambda bi, t: (bi, t + first, 0))
    kern = functools.partial(_out_kernel, d_model=d, final=final)
    out_rows = n_tiles * tm
    return pl.pallas_call(
        kern,
        grid=(b, n_tiles),
        in_specs=[
            tok(d),
            tok(MLA_HEADS * MLA_V_DIM),
            tok(GDN_WIDTH), tok(GDN_WIDTH), tok(GDN_WIDTH),
            pl.BlockSpec((None, None, None, 1, d6),
                         lambda bi, t: (layer, bi, jnp.where(t + first < n_ctx_tiles, 1, 0), 0, 0)),
            _const_spec((1, GDN_HEAD_DIM)),
            _const_spec(wts["w_out"].shape),
            _const_spec((1, d)),
            _const_spec(wts["w_ff1"].shape),
            _const_spec(wts["w_ff2"].shape),
            _const_spec((1, d)),
        ],
        out_specs=pl.BlockSpec((None, tm, d), lambda bi, t: (bi, t, 0)),
        out_shape=jax.ShapeDtypeStruct((b, out_rows, d), F32),
        compiler_params=_params("parallel", "parallel"),
        name="out_final" if final else "out",
    )(xs, attn, o_f, o_b, z, mod, wts["gdn_norm_g"], wts["w_out"], wts["norm2_g"],
      wts["w_ff1"], wts["w_ff2"], fng)


def _rope_tables(t_lat, t_ctx):
    rows = t_lat // GRID_W
    row = jnp.broadcast_to(jnp.arange(rows)[:, None], (rows, GRID_W)).reshape(-1).astype(F32)
    col = jnp.broadcast_to(jnp.arange(GRID_W)[None, :], (rows, GRID_W)).reshape(-1).astype(F32)
    axis_pairs = MLA_ROPE_DIM // 4
    inv_freq = ROPE_THETA ** (-jnp.arange(axis_pairs, dtype=F32) / axis_pairs)
    ang = jnp.concatenate([row[:, None] * inv_freq, col[:, None] * inv_freq], axis=-1)
    cos, sin = jnp.cos(ang), jnp.sin(ang)
    half = MLA_ROPE_DIM // 2
    lo, mid, hi = MLA_NOPE_DIM, MLA_NOPE_DIM + half, MLA_NOPE_DIM + 2 * half
    ta = t_ctx + t_lat
    ctab = jnp.ones((ta, LANES), F32).at[t_ctx:, lo:mid].set(cos).at[t_ctx:, mid:hi].set(cos)
    sa = jnp.zeros((ta, LANES), F32).at[t_ctx:, lo:mid].set(-sin)
    sb = jnp.zeros((ta, LANES), F32).at[t_ctx:, mid:hi].set(sin)
    return ctab, sa, sb


def _chunk_sum_matrices():
    i = jnp.arange(GDN_BLOCK)
    same = (i[:, None] // GDN_CHUNK) == (i[None, :] // GDN_CHUNK)
    lower = same & (i[:, None] >= i[None, :])
    upper = same & (i[:, None] <= i[None, :])
    return jnp.stack([lower, upper], axis=0).astype(BF16)


def _pad_heads(w, used):
    k = w.shape[0]
    w = w.reshape(k, MLA_HEADS, used)
    return jnp.pad(w, ((0, 0), (0, 0), (0, LANES - used))).reshape(k, MLA_HEADS * LANES)


def _layer_weights(i, norm1_g, w_in, q_a_g, w_q_b, kv_a_g, w_kv_b, conv_w, a_log, dt_bias, gdn_norm_g,
                   w_out, norm2_g, w_ff1, w_ff2):
    d = w_in.shape[1]
    wi = w_in[i]
    o = 0
    cols = []
    for sz in (MLA_Q_RANK, MLA_KV_RANK, MLA_ROPE_DIM, 3 * GDN_WIDTH, GDN_WIDTH,
               N_DIR * GDN_HEADS, N_DIR * GDN_HEADS):
        cols.append(wi[:, o:o + sz])
        o += sz
    w_qa, w_kva, w_kr, w_qkv, w_z, w_a, w_b = cols
    w_kr_pad = jnp.zeros((d, LANES), F32).at[:, MLA_NOPE_DIM:MLA_QK_DIM].set(w_kr)
    nab = N_DIR * GDN_HEADS
    w_ab_pad = jnp.zeros((d, LANES), F32).at[:, 0:nab].set(w_a).at[:, nab:2 * nab].set(w_b)
    w_main = jnp.concatenate([w_qa, w_kva, w_kr_pad, w_z, w_ab_pad], axis=1)
    kv = w_kv_b[i].reshape(MLA_KV_RANK, MLA_HEADS, MLA_NOPE_DIM + MLA_V_DIM)
    w_k = _pad_heads(kv[:, :, :MLA_NOPE_DIM].reshape(MLA_KV_RANK, -1), MLA_NOPE_DIM)
    w_v = _pad_heads(kv[:, :, MLA_NOPE_DIM:].reshape(MLA_KV_RANK, -1), MLA_V_DIM)
    row = lambda v: jnp.zeros((1, LANES), F32).at[0, 0:nab].set(v.reshape(-1))
    return dict(
        norm1_g=norm1_g[i][None, :],
        w_qkv=w_qkv.astype(BF16),
        w_main=w_main.astype(BF16),
        q_a_g=q_a_g[i][None, :],
        kv_a_g=kv_a_g[i][None, :],
        w_q=_pad_heads(w_q_b[i], MLA_QK_DIM).astype(BF16),
        w_k=w_k.astype(BF16),
        w_v=w_v.astype(BF16),
        conv_w=jnp.pad(conv_w[i], ((0, 8 - GDN_CONV), (0, 0))),
        a_log=row(a_log[i]),
        dt_bias=row(dt_bias[i]),
        gdn_norm_g=gdn_norm_g[i][None, :],
        w_out=w_out[i].astype(BF16),
        norm2_g=norm2_g[i][None, :],
        w_ff1=w_ff1[i].astype(BF16),
        w_ff2=w_ff2[i].astype(BF16),
    )


def kernel(x, c, ctx, c_ctx, w_ada, b_ada, norm1_g, w_in, q_a_g, w_q_b, kv_a_g, w_kv_b, conv_w, a_log,
           dt_bias, gdn_norm_g, w_out, norm2_g, w_ff1, w_ff2, final_norm_g):
    b, t_lat, d = x.shape
    t_ctx = ctx.shape[1]
    depth = w_ada.shape[0]
    assert t_ctx % TOKEN_TILE == 0 and t_lat % TOKEN_TILE == 0 and t_lat % GRID_W == 0
    assert b + 1 <= PAD_ROWS
    n_ctx_tiles = t_ctx // TOKEN_TILE

    xs = jnp.concatenate([ctx, x], axis=1)
    cc = jnp.zeros((PAD_ROWS, d), F32).at[0:b].set(c).at[b].set(c_ctx)
    mods = _ada_call(cc, w_ada, b_ada)
    mod = jnp.stack([mods[:, 0:b], jnp.broadcast_to(mods[:, b:b + 1], (depth, b, 6 * d))], axis=2)
    mod = mod[:, :, :, None, :]
    tabs = _rope_tables(t_lat, t_ctx)
    cm = _chunk_sum_matrices()
    fng = final_norm_g[None, :]

    for i in range(depth):
        last = i == depth - 1
        wts = _layer_weights(i, norm1_g, w_in, q_a_g, w_q_b, kv_a_g, w_kv_b, conv_w, a_log, dt_bias,
                             gdn_norm_g, w_out, norm2_g, w_ff1, w_ff2)
        qt, k, vt, gq, gk, gv, gb, z = _proj_call(xs, mod, i, wts, tabs, n_ctx_tiles)
        attn = _attn_call(qt, k, vt, t_ctx)
        o_f, o_b = _gdn_call(gq, gk, gv, gb, cm, t_ctx // GDN_BLOCK)
        xs = _out_call(xs, attn, o_f, o_b, z, mod, i, wts, fng, n_ctx_tiles, last)
    return xs
```

```python
import functools

import jax
import jax.numpy as jnp
from jax import lax
from jax.experimental import pallas as pl
from jax.experimental.pallas import tpu as pltpu

F32 = jnp.float32
BF16 = jnp.bfloat16

GRID_W = 64
MLA_HEADS = 8
MLA_NOPE_DIM = 64
MLA_ROPE_DIM = 32
MLA_V_DIM = 64
MLA_Q_RANK = 256
MLA_KV_RANK = 128
MLA_QK_DIM = MLA_NOPE_DIM + MLA_ROPE_DIM
GDN_HEADS = 4
GDN_HEAD_DIM = 128
GDN_WIDTH = GDN_HEADS * GDN_HEAD_DIM
GDN_CONV = 5
GDN_CHUNK = 64
N_DIR = 2
ROPE_THETA = 10000.0
NORM_EPS = 1e-6
LOG2_E = 1.4426950408889634

LANES = 128
BF16_SUBLANES = 16
VMEM_LIMIT_BYTES = 56 * 1024 * 1024

TOKEN_TILE = 256
GDN_BLOCK = 256
GDN_BATCH = 2
PROJ_BATCH = 2
GDN_WAVE_LAG = 0
ATTN_KEY_TILE = 256
NEG_BIG = -1e30
PAD_ROWS = 16


def _sigmoid(x):
    return 1.0 / (1.0 + jnp.exp(-x))


def _silu(x):
    return x * _sigmoid(x)


def _dot(a, b):
    return jnp.dot(a, b, preferred_element_type=F32)


def _dot_nt(a, b):
    return lax.dot_general(a, b, (((1,), (1,)), ((), ())), preferred_element_type=F32)


def _dot_tn(a, b):
    return lax.dot_general(a, b, (((0,), (0,)), ((), ())), preferred_element_type=F32)


def _const_spec(shape):
    nd = len(shape)
    return pl.BlockSpec(shape, lambda *_: (0,) * nd, pipeline_mode=pl.Buffered(1))


def _params(*sem):
    return pltpu.CompilerParams(dimension_semantics=sem, vmem_limit_bytes=VMEM_LIMIT_BYTES)


def _ada_kernel(c_ref, w_ref, b_ref, o_ref):
    s = _silu(c_ref[...]).astype(BF16)
    o_ref[0] = _dot(s, w_ref[0].astype(BF16)) + b_ref[0]


def _ada_call(cc, w_ada, b_ada):
    n_layers, d, d6 = w_ada.shape
    tn = 1536
    return pl.pallas_call(
        _ada_kernel,
        grid=(n_layers, d6 // tn),
        in_specs=[
            pl.BlockSpec((PAD_ROWS, d), lambda l, n: (0, 0)),
            pl.BlockSpec((1, d, tn), lambda l, n: (l, 0, n)),
            pl.BlockSpec((1, 1, tn), lambda l, n: (l, 0, n)),
        ],
        out_specs=pl.BlockSpec((1, PAD_ROWS, tn), lambda l, n: (l, 0, n)),
        out_shape=jax.ShapeDtypeStruct((n_layers, PAD_ROWS, d6), F32),
        compiler_params=_params("parallel", "parallel"),
        name="ada",
    )(cc, w_ada, b_ada.reshape(n_layers, 1, d6))


def _rope(x, c, sa, sb):
    return x * c + pltpu.roll(x, LANES - 16, 1) * sa + pltpu.roll(x, 16, 1) * sb


def _proj_kernel(xc_ref, xp_ref, xn_ref, mod_ref, n1g_ref, wall_ref, qag_ref, kvag_ref,
                 wq_ref, wk_ref, wv_ref, ctab_ref, sa_ref, sb_ref, convw_ref, shift_ref, alog_ref, dtb_ref,
                 qt_ref, k_ref, vt_ref, gq_ref, gk_ref, gv_ref, gb_ref, zs_ref,
                 hext_ref, pqkv_ref, cbuf_ref, *, tm, n_ctx_tiles, n_tiles, d_model):
    t = pl.program_id(1)
    halo = BF16_SUBLANES
    n_qkv = 3 * GDN_WIDTH
    n1g = n1g_ref[...]
    seg_first = jnp.logical_or(t == 0, t == n_ctx_tiles)
    seg_last = jnp.logical_or(t == n_ctx_tiles - 1, t == n_tiles - 1)
    o_kva = MLA_Q_RANK
    o_kr = o_kva + MLA_KV_RANK
    o_z = o_kr + LANES
    o_ab = o_z + GDN_WIDTH
    ctab = ctab_ref[...]
    sa = sa_ref[...]
    sb = sb_ref[...]
    cw = convw_ref[...]
    rows = tm + 2 * halo
    lane = lax.broadcasted_iota(jnp.int32, (tm, LANES), 1)
    pms = {}
    convs = {}

    def st_front(bb):
        mod = mod_ref[bb]
        sh1 = mod[:, 0:d_model]
        gs = n1g * (1.0 + mod[:, d_model:2 * d_model])

        def modulate(x):
            y = x * lax.rsqrt(jnp.mean(x * x, axis=-1, keepdims=True) + NORM_EPS)
            return (y * gs + sh1).astype(BF16)

        hext_ref[bb, 0:halo, :] = modulate(xp_ref[bb])
        hext_ref[bb, halo:halo + tm, :] = modulate(xc_ref[bb])
        hext_ref[bb, halo + tm:, :] = modulate(xn_ref[bb])

    def st_front_mm(bb):
        pqkv_ref[bb] = _dot(hext_ref[bb], wall_ref[:, 0:n_qkv])

        @pl.when(seg_first)
        def _():
            pqkv_ref[bb, 0:halo, :] = jnp.zeros((halo, n_qkv), F32)

        @pl.when(seg_last)
        def _():
            pqkv_ref[bb, halo + tm:, :] = jnp.zeros((halo, n_qkv), F32)

        pms[bb] = _dot(hext_ref[bb, halo:halo + tm, :], wall_ref[:, n_qkv:])

    def st_mla(bb):
        pm = pms[bb]
        qa = pm[:, 0:o_kva]
        kva = pm[:, o_kva:o_kr]
        kr = pm[:, o_kr:o_z]
        qn = (qa * lax.rsqrt(jnp.mean(qa * qa, axis=-1, keepdims=True) + NORM_EPS) * qag_ref[...]).astype(BF16)
        q = _dot(qn, wq_ref[...])
        scale = MLA_QK_DIM ** -0.5 * LOG2_E
        for h in range(MLA_HEADS):
            qh = q[:, h * LANES:(h + 1) * LANES]
            qt_ref[bb, h] = (_rope(qh, ctab, sa, sb) * scale).T.astype(BF16)

        kvn = (kva * lax.rsqrt(jnp.mean(kva * kva, axis=-1, keepdims=True) + NORM_EPS)
               * kvag_ref[...]).astype(BF16)
        kk = _dot(kvn, wk_ref[...])
        kpe = _rope(kr, ctab, sa, sb)
        for h in range(MLA_HEADS):
            k_ref[bb, :, h * LANES:(h + 1) * LANES] = (kk[:, h * LANES:(h + 1) * LANES] + kpe).astype(BF16)
        vv = _dot(kvn, wv_ref[...])
        for h in range(MLA_HEADS):
            vh = jnp.where(lane == MLA_V_DIM, 1.0, vv[:, h * LANES:(h + 1) * LANES])
            vt_ref[bb, h] = vh.T.astype(BF16)

    def st_cbuf(bb):
        pm = pms[bb]
        zs_ref[bb] = _silu(pm[:, o_z:o_ab]).astype(BF16)
        for j in range(GDN_CONV):
            cbuf_ref[bb, j * rows:(j + 1) * rows, :] = (pqkv_ref[bb] * cw[j:j + 1, :]).astype(BF16)

    def st_conv_mm(bb):
        convs[bb] = _dot(shift_ref[...], cbuf_ref[bb])

    def st_post(bb):
        conv = convs[bb]
        ab = pms[bb][:, o_ab:o_ab + LANES]
        for cb in range(3 * GDN_HEADS):
            cs = slice(cb * LANES, (cb + 1) * LANES)
            act = _silu(conv[:, cs])
            grp, hh = divmod(cb, GDN_HEADS)
            hs = slice(hh * LANES, (hh + 1) * LANES)
            if grp < 2:
                act = act * lax.rsqrt(jnp.sum(act * act, axis=-1, keepdims=True) + NORM_EPS)
                (gq_ref if grp == 0 else gk_ref)[bb, :, hs] = act.astype(BF16)
            else:
                gv_ref[bb, :, hs] = act.astype(BF16)

        x = ab + dtb_ref[...]
        softplus = jnp.maximum(x, 0.0) + jnp.log(1.0 + jnp.exp(-jnp.abs(x)))
        gval = -jnp.exp(alog_ref[...]) * softplus
        gb_ref[bb] = jnp.where(lane < N_DIR * GDN_HEADS, gval, _sigmoid(ab))

    rows_b = range(PROJ_BATCH)
    for bb in rows_b:
        st_front(bb)
    st_front_mm(0)
    for bb in rows_b:
        if bb + 1 < PROJ_BATCH:
            st_front_mm(bb + 1)
        st_mla(bb)
        st_cbuf(bb)
        st_conv_mm(bb)
    for bb in rows_b:
        st_post(bb)


def _conv_shift_matrix():
    rows = TOKEN_TILE + 2 * BF16_SUBLANES
    r = jnp.arange(TOKEN_TILE)[:, None]
    col = jnp.arange(GDN_CONV * rows)[None, :]
    j = col // rows
    return (col % rows == BF16_SUBLANES + r + j - GDN_CONV // 2).astype(BF16)


def _proj_call(xs, mod, layer, wts, tabs, shift, n_ctx_tiles):
    b, ta, d = xs.shape
    tm = TOKEN_TILE
    n_tiles = ta // tm
    halo = BF16_SUBLANES
    hb = tm // halo
    n_halo_blocks = ta // halo
    d6 = 6 * d

    pb = PROJ_BATCH
    assert b % pb == 0
    tok = lambda w: pl.BlockSpec((pb, tm, w), lambda bi, t: (bi, t, 0))
    in_specs = [
        tok(d),
        pl.BlockSpec((pb, halo, d), lambda bi, t: (bi, jnp.maximum(t * hb - 1, 0), 0)),
        pl.BlockSpec((pb, halo, d), lambda bi, t: (bi, jnp.minimum((t + 1) * hb, n_halo_blocks - 1), 0)),
        pl.BlockSpec((None, pb, None, 1, d6),
                     lambda bi, t: (layer, bi, jnp.where(t < n_ctx_tiles, 1, 0), 0, 0)),
        _const_spec((1, d)),
        _const_spec(wts["w_all"].shape),
        _const_spec((1, MLA_Q_RANK)),
        _const_spec((1, MLA_KV_RANK)),
        _const_spec((MLA_Q_RANK, MLA_HEADS * LANES)),
        _const_spec((MLA_KV_RANK, MLA_HEADS * LANES)),
        _const_spec((MLA_KV_RANK, MLA_HEADS * LANES)),
        pl.BlockSpec((tm, LANES), lambda bi, t: (t, 0)),
        pl.BlockSpec((tm, LANES), lambda bi, t: (t, 0)),
        pl.BlockSpec((tm, LANES), lambda bi, t: (t, 0)),
        _const_spec((8, 3 * GDN_WIDTH)),
        _const_spec(shift.shape),
        _const_spec((1, LANES)),
        _const_spec((1, LANES)),
    ]
    head_t = pl.BlockSpec((pb, MLA_HEADS, LANES, tm), lambda bi, t: (bi, 0, 0, t))
    out_specs = [
        head_t,
        tok(MLA_HEADS * LANES),
        head_t,
        tok(GDN_WIDTH), tok(GDN_WIDTH), tok(GDN_WIDTH),
        tok(LANES),
        tok(GDN_WIDTH),
    ]
    out_shape = [
        jax.ShapeDtypeStruct((b, MLA_HEADS, LANES, ta), BF16),
        jax.ShapeDtypeStruct((b, ta, MLA_HEADS * LANES), BF16),
        jax.ShapeDtypeStruct((b, MLA_HEADS, LANES, ta), BF16),
        jax.ShapeDtypeStruct((b, ta, GDN_WIDTH), BF16),
        jax.ShapeDtypeStruct((b, ta, GDN_WIDTH), BF16),
        jax.ShapeDtypeStruct((b, ta, GDN_WIDTH), BF16),
        jax.ShapeDtypeStruct((b, ta, LANES), F32),
        jax.ShapeDtypeStruct((b, ta, GDN_WIDTH), BF16),
    ]
    kern = functools.partial(_proj_kernel, tm=tm, n_ctx_tiles=n_ctx_tiles, n_tiles=n_tiles, d_model=d)
    return pl.pallas_call(
        kern,
        grid=(b // pb, n_tiles),
        in_specs=in_specs,
        out_specs=out_specs,
        out_shape=out_shape,
        scratch_shapes=[pltpu.VMEM((pb, tm + 2 * halo, d), BF16),
                        pltpu.VMEM((pb, tm + 2 * halo, 3 * GDN_WIDTH), F32),
                        pltpu.VMEM((pb, GDN_CONV * (tm + 2 * halo), 3 * GDN_WIDTH), BF16)],
        compiler_params=_params("parallel", "parallel"),
        name="proj",
    )(xs, xs, xs, mod, wts["norm1_g"], wts["w_all"], wts["q_a_g"], wts["kv_a_g"],
      wts["w_q"], wts["w_k"], wts["w_v"], tabs[0], tabs[1], tabs[2], wts["conv_w"], shift,
      wts["a_log"], wts["dt_bias"])


def _attn_kernel(qt_ref, k_ref, vt_ref, o_ref, s_ref, *, n_ctx_q, tc, ta):
    qi = pl.program_id(1)
    kt = ATTN_KEY_TILE

    def score_tile(h, t, slot):
        s = _dot(k_ref[t * kt:(t + 1) * kt, h * LANES:(h + 1) * LANES], qt_ref[h])
        s_ref[slot, t * kt:(t + 1) * kt, :] = s
        return jnp.max(s, axis=0, keepdims=True)

    def pv_tile(h, t, slot, m):
        p = jnp.exp2(s_ref[slot, t * kt:(t + 1) * kt, :] - m).astype(BF16)
        return _dot(vt_ref[h, :, t * kt:(t + 1) * kt], p)

    def run(nk):
        nt = nk // kt
        outs = []
        m = functools.reduce(jnp.maximum, [score_tile(0, t, 0) for t in range(nt)])
        for h in range(MLA_HEADS):
            slot = h % 2
            acc = None
            m_parts = []
            for t in range(nt):
                if h + 1 < MLA_HEADS:
                    m_parts.append(score_tile(h + 1, t, 1 - slot))
                o = pv_tile(h, t, slot, m)
                acc = o if acc is None else acc + o
            outs.append(acc[0:MLA_V_DIM] / acc[MLA_V_DIM:MLA_V_DIM + 1])
            if m_parts:
                m = functools.reduce(jnp.maximum, m_parts)
        o_ref[...] = jnp.concatenate(outs, axis=0).T.astype(BF16)

    @pl.when(qi < n_ctx_q)
    def _():
        run(tc)

    @pl.when(qi >= n_ctx_q)
    def _():
        run(ta)


def _attn_call(qt, k, vt, tc):
    b, ta, _ = k.shape
    tq = TOKEN_TILE
    n_ctx_q = tc // tq
    nq = ta // tq
    kern = functools.partial(_attn_kernel, n_ctx_q=n_ctx_q, tc=tc, ta=ta)
    return pl.pallas_call(
        kern,
        grid=(b, nq),
        in_specs=[
            pl.BlockSpec((None, MLA_HEADS, LANES, tq), lambda bi, qi: (bi, 0, 0, qi)),
            pl.BlockSpec((None, ta, MLA_HEADS * LANES), lambda bi, qi: (bi, 0, 0)),
            pl.BlockSpec((None, MLA_HEADS, LANES, ta), lambda bi, qi: (bi, 0, 0, 0)),
        ],
        out_specs=pl.BlockSpec((None, tq, MLA_HEADS * MLA_V_DIM), lambda bi, qi: (bi, qi, 0)),
        out_shape=jax.ShapeDtypeStruct((b, nq * tq, MLA_HEADS * MLA_V_DIM), BF16),
        scratch_shapes=[pltpu.VMEM((2, ta, tq), F32)],
        compiler_params=_params("parallel", "arbitrary"),
        name="attn",
    )(qt, k, vt)


def _gdn_kernel(qf_ref, kf_ref, vf_ref, gf_ref, qb_ref, kb_ref, vb_ref, gbk_ref, cm_ref,
                of_ref, ob_ref, s_ref):
    j = pl.program_id(1)
    blk = GDN_BLOCK
    c = GDN_CHUNK
    n_chunks = blk // c

    @pl.when(j == 0)
    def _():
        s_ref[...] = jnp.zeros(s_ref.shape, F32)

    row = lax.broadcasted_iota(jnp.int32, (blk, blk), 0)
    col = lax.broadcasted_iota(jnp.int32, (blk, blk), 1)
    same = (row // c) == (col // c)
    eye = jnp.where(row == col, 1.0, 0.0)
    pair_masks = []
    size = 1
    while size < c:
        pair_masks.append(jnp.logical_and((row // (2 * size)) == (col // (2 * size)),
                                          (row // size) != (col // size)))
        size *= 2
    scale = GDN_HEAD_DIM ** -0.5

    dir_refs = ((qf_ref, kf_ref, vf_ref, gf_ref, of_ref), (qb_ref, kb_ref, vb_ref, gbk_ref, ob_ref))

    def st_setup(wave):
        bb = wave["bb"]
        chains = wave["chains"] = []
        for d in range(N_DIR):
            q_ref, k_ref, v_ref, g_ref, o_ref = (r.at[bb] for r in dir_refs[d])
            incl = jnp.logical_and(same, (row >= col) if d == 0 else (row <= col))
            strict = jnp.logical_and(same, (row > col) if d == 0 else (row < col))
            gb = g_ref[...]
            g1 = gb.astype(BF16)
            r1 = gb - g1.astype(F32)
            g2 = r1.astype(BF16)
            g3 = (r1 - g2.astype(F32)).astype(BF16)
            cmd = cm_ref[d]
            gc = _dot(cmd, g1) + _dot(cmd, g2) + _dot(cmd, g3)
            last = c - 1 if d == 0 else 0
            gtot = jnp.concatenate(
                [jnp.broadcast_to(gc[ci * c + last:ci * c + last + 1, :], (c, LANES)) for ci in range(n_chunks)],
                axis=0)
            gct = gc.T
            for h in range(GDN_HEADS):
                ln = d * GDN_HEADS + h
                hs = slice(h * LANES, (h + 1) * LANES)
                gcol = gc[:, ln:ln + 1]
                gl = gtot[:, ln:ln + 1]
                beta = gb[:, N_DIR * GDN_HEADS + ln:N_DIR * GDN_HEADS + ln + 1]
                decay = jnp.exp(jnp.where(incl, gcol - gct[ln:ln + 1, :], NEG_BIG))
                k16 = k_ref[:, hs]
                q16 = q_ref[:, hs]
                k = k16.astype(F32)
                kbeta = k * beta
                egc = jnp.exp(gcol)
                chains.append(dict(
                    d=d, hs=hs, o_ref=o_ref, idx=bb * N_DIR * GDN_HEADS + ln, strict=strict, decay=decay,
                    kb16=k16, kbeta16=kbeta.astype(BF16), q16=q16,
                    rhs16=jnp.concatenate([v_ref[:, hs].astype(F32) * beta, kbeta * egc], axis=1).astype(BF16),
                    qd=q16.astype(F32) * (egc * scale), kt=k * jnp.exp(gl - gcol), glast=jnp.exp(gl)))

    def st_gram(wave):
        for ch in wave["chains"]:
            gram = _dot_nt(ch["kbeta16"], ch["kb16"])
            lower = jnp.where(ch["strict"], gram * ch["decay"], 0.0)
            ch["lower16"] = lower.astype(BF16)
            ch["t16"] = (eye - jnp.where(pair_masks[0], lower, 0.0)).astype(BF16)

    def st_merge_a(pm16):
        def run(wave):
            for ch in wave["chains"]:
                ch["p16"] = _dot(ch["lower16"] * pm16, ch["t16"]).astype(BF16)
        return run

    def st_merge_b(wave):
        for ch in wave["chains"]:
            ch["t16"] = ch["t16"] - _dot(ch["t16"], ch["p16"]).astype(BF16)

    def half_rows(x, s, d):
        x3 = x.reshape(blk // (2 * s), 2 * s, x.shape[1])
        return (x3[:, s:, :] if d == 0 else x3[:, :s, :]).reshape(blk // 2, x.shape[1])

    def st_merge_half_a(pm16, s):
        pm_half = [half_rows(pm16, s, d) for d in range(N_DIR)]

        def run(wave):
            for ch in wave["chains"]:
                e_half = half_rows(ch["lower16"], s, ch["d"]) * pm_half[ch["d"]]
                ch["p_half16"] = _dot(e_half, ch["t16"]).astype(BF16)
        return run

    def st_merge_half_b(s):
        def run(wave):
            n_pairs = blk // (2 * s)
            for ch in wave["chains"]:
                t3 = ch["t16"].reshape(n_pairs, 2 * s, blk)
                p3 = ch["p_half16"].reshape(n_pairs, s, blk)
                zero = jnp.zeros_like(p3)
                if ch["d"] == 0:
                    p_full = jnp.concatenate([zero, p3], axis=1).reshape(blk, blk)
                    r = _dot(t3[:, s:, :].reshape(blk // 2, blk), p_full).astype(BF16)
                    parts = [t3[:, :s, :], t3[:, s:, :] - r.reshape(n_pairs, s, blk)]
                else:
                    p_full = jnp.concatenate([p3, zero], axis=1).reshape(blk, blk)
                    r = _dot(t3[:, :s, :].reshape(blk // 2, blk), p_full).astype(BF16)
                    parts = [t3[:, :s, :] - r.reshape(n_pairs, s, blk), t3[:, s:, :]]
                ch["t16"] = jnp.concatenate(parts, axis=1).reshape(blk, blk)
        return run

    def st_apply(wave):
        for ch in wave["chains"]:
            x = _dot(ch["t16"], ch["rhs16"])
            ch["u"] = x[:, 0:LANES]
            ch["w"] = x[:, LANES:2 * LANES]
            ch["qk"] = _dot_nt(ch["q16"], ch["kb16"]) * (ch["decay"] * scale)

    def st_prep(wave):
        for ch in wave["chains"]:
            ch["s"] = s_ref[ch["idx"]]
            ch["wq16"] = [jnp.concatenate([ch["w"][ci * c:(ci + 1) * c], ch["qd"][ci * c:(ci + 1) * c]],
                                          axis=0).astype(BF16) for ci in range(n_chunks)]
            ch["ktt16"] = [ch["kt"][ci * c:(ci + 1) * c].T.astype(BF16) for ci in range(n_chunks)]
            ch["qk16"] = [ch["qk"][ci * c:(ci + 1) * c, ci * c:(ci + 1) * c].astype(BF16)
                          for ci in range(n_chunks)]

    def st_step_a(step):
        def run(wave):
            for ch in wave["chains"]:
                ci = ch["ci"] = step if ch["d"] == 0 else n_chunks - 1 - step
                ch["ws"] = _dot(ch["wq16"][ci], ch["s"].astype(BF16))
        return run

    def st_step_b(wave):
        for ch in wave["chains"]:
            ci, ws = ch["ci"], ch["ws"]
            r = slice(ci * c, (ci + 1) * c)
            vn16 = (ch["u"][r] - ws[0:c]).astype(BF16)
            o = ws[c:2 * c] + _dot(ch["qk16"][ci], vn16)
            glast = ch["glast"][ci * c:ci * c + 1, :]
            ch["s"] = ch["s"] * glast + _dot(ch["ktt16"][ci], vn16)
            ch["o_ref"][r, ch["hs"]] = o.astype(BF16)

    def st_store(wave):
        for ch in wave["chains"]:
            s_ref[ch["idx"]] = ch["s"]

    stages = [st_setup, st_gram]
    for lvl, pm in enumerate(pair_masks[1:], start=1):
        pm16 = jnp.where(pm, 1.0, 0.0).astype(BF16)
        s = 2 ** lvl
        if s >= BF16_SUBLANES:
            stages += [st_merge_half_a(pm16, s), st_merge_half_b(s)]
        else:
            stages += [st_merge_a(pm16), st_merge_b]
    stages += [st_apply, st_prep]
    for step in range(n_chunks):
        stages += [st_step_a(step), st_step_b]
    stages.append(st_store)

    waves = [dict(bb=bb) for bb in range(GDN_BATCH)]
    for tick in range(len(stages) + GDN_WAVE_LAG * (GDN_BATCH - 1)):
        for wi, wave in enumerate(waves):
            si = tick - GDN_WAVE_LAG * wi
            if 0 <= si < len(stages):
                stages[si](wave)


def _gdn_call(gq, gk, gv, gb, cm, n_ctx_blocks):
    b, ta, w = gq.shape
    blk = GDN_BLOCK
    nb = ta // blk

    def bwd_block(j):
        return jnp.where(j < n_ctx_blocks, n_ctx_blocks - 1 - j, nb - 1 - (j - n_ctx_blocks))

    gbt = GDN_BATCH
    assert b % gbt == 0
    f_spec = lambda wd: pl.BlockSpec((gbt, blk, wd), lambda bi, j: (bi, j, 0))
    b_spec = lambda wd: pl.BlockSpec((gbt, blk, wd), lambda bi, j: (bi, bwd_block(j), 0))
    return pl.pallas_call(
        _gdn_kernel,
        grid=(b // gbt, nb),
        in_specs=[f_spec(w), f_spec(w), f_spec(w), f_spec(LANES),
                  b_spec(w), b_spec(w), b_spec(w), b_spec(LANES),
                  _const_spec(cm.shape)],
        out_specs=[f_spec(w), b_spec(w)],
        out_shape=[jax.ShapeDtypeStruct((b, ta, w), BF16), jax.ShapeDtypeStruct((b, ta, w), BF16)],
        scratch_shapes=[pltpu.VMEM((gbt * N_DIR * GDN_HEADS, GDN_HEAD_DIM, GDN_HEAD_DIM), F32)],
        compiler_params=_params("parallel", "arbitrary"),
        name="gdn",
    )(gq, gk, gv, gb, gq, gk, gv, gb, cm)


def _out_kernel(x_ref, attn_ref, of_ref, ob_ref, zs_ref, mod_ref, gng_ref, wout_ref, n2g_ref,
                w1_ref, w2_ref, fng_ref, o_ref, *, d_model, final):
    mod = mod_ref[...]
    g1 = mod[:, 2 * d_model:3 * d_model]
    sh2 = mod[:, 3 * d_model:4 * d_model]
    sc2 = mod[:, 4 * d_model:5 * d_model]
    g2 = mod[:, 5 * d_model:6 * d_model]

    o = of_ref[...].astype(F32) + ob_ref[...].astype(F32)
    zs = zs_ref[...].astype(F32)
    gng = gng_ref[...]
    parts = [attn_ref[...]]
    for h in range(GDN_HEADS):
        hs = slice(h * LANES, (h + 1) * LANES)
        oh = o[:, hs]
        y = oh * lax.rsqrt(jnp.mean(oh * oh, axis=-1, keepdims=True) + NORM_EPS) * gng
        parts.append((y * zs[:, hs]).astype(BF16))
    mix = jnp.concatenate(parts, axis=1)
    x1 = x_ref[...] + g1 * _dot(mix, wout_ref[...])

    y2 = x1 * lax.rsqrt(jnp.mean(x1 * x1, axis=-1, keepdims=True) + NORM_EPS)
    h2 = ((y2 * n2g_ref[...]) * (1.0 + sc2) + sh2).astype(BF16)
    ff = jnp.maximum(_dot(h2, w1_ref[...]), 0.0)
    ff = (ff * ff).astype(BF16)
    x2 = x1 + g2 * _dot(ff, w2_ref[...])
    if final:
        x2 = x2 * lax.rsqrt(jnp.mean(x2 * x2, axis=-1, keepdims=True) + NORM_EPS) * fng_ref[...]
    o_ref[...] = x2


def _out_call(xs, attn, o_f, o_b, z, mod, layer, wts, fng, n_ctx_tiles, final):
    b, ta, d = xs.shape
    tm = TOKEN_TILE
    d6 = 6 * d
    first = n_ctx_tiles if final else 0
    n_tiles = ta // tm - first
    tok = lambda w: pl.BlockSpec((None, tm, w), lambda bi, t: (bi, t + first, 0))
    kern = functools.partial(_out_kernel, d_model=d, final=final)
    out_rows = n_tiles * tm
    return pl.pallas_call(
        kern,
        grid=(b, n_tiles),
        in_specs=[
            tok(d),
            tok(MLA_HEADS * MLA_V_DIM),
            tok(GDN_WIDTH), tok(GDN_WIDTH), tok(GDN_WIDTH),
            pl.BlockSpec((None, None, None, 1, d6),
                         lambda bi, t: (layer, bi, jnp.where(t + first < n_ctx_tiles, 1, 0), 0, 0)),
            _const_spec((1, GDN_HEAD_DIM)),
            _const_spec(wts["w_out"].shape),
            _const_spec((1, d)),
            _const_spec(wts["w_ff1"].shape),
            _const_spec(wts["w_ff2"].shape),
            _const_spec((1, d)),
        ],
        out_specs=pl.BlockSpec((None, tm, d), lambda bi, t: (bi, t, 0)),
        out_shape=jax.ShapeDtypeStruct((b, out_rows, d), F32),
        compiler_params=_params("parallel", "parallel"),
        name="out_final" if final else "out",
    )(xs, attn, o_f, o_b, z, mod, wts["gdn_norm_g"], wts["w_out"], wts["norm2_g"],
      wts["w_ff1"], wts["w_ff2"], fng)


def _rope_tables(t_lat, t_ctx):
    rows = t_lat // GRID_W
    row = jnp.broadcast_to(jnp.arange(rows)[:, None], (rows, GRID_W)).reshape(-1).astype(F32)
    col = jnp.broadcast_to(jnp.arange(GRID_W)[None, :], (rows, GRID_W)).reshape(-1).astype(F32)
    axis_pairs = MLA_ROPE_DIM // 4
    inv_freq = ROPE_THETA ** (-jnp.arange(axis_pairs, dtype=F32) / axis_pairs)
    ang = jnp.concatenate([row[:, None] * inv_freq, col[:, None] * inv_freq], axis=-1)
    cos, sin = jnp.cos(ang), jnp.sin(ang)
    half = MLA_ROPE_DIM // 2
    lo, mid, hi = MLA_NOPE_DIM, MLA_NOPE_DIM + half, MLA_NOPE_DIM + 2 * half
    ta = t_ctx + t_lat
    ctab = jnp.ones((ta, LANES), F32).at[t_ctx:, lo:mid].set(cos).at[t_ctx:, mid:hi].set(cos)
    sa = jnp.zeros((ta, LANES), F32).at[t_ctx:, lo:mid].set(-sin)
    sb = jnp.zeros((ta, LANES), F32).at[t_ctx:, mid:hi].set(sin)
    return ctab, sa, sb


def _chunk_sum_matrices():
    i = jnp.arange(GDN_BLOCK)
    same = (i[:, None] // GDN_CHUNK) == (i[None, :] // GDN_CHUNK)
    lower = same & (i[:, None] >= i[None, :])
    upper = same & (i[:, None] <= i[None, :])
    return jnp.stack([lower, upper], axis=0).astype(BF16)


def _pad_heads(w, used):
    k = w.shape[0]
    w = w.reshape(k, MLA_HEADS, used)
    return jnp.pad(w, ((0, 0), (0, 0), (0, LANES - used))).reshape(k, MLA_HEADS * LANES)


def _layer_weights(i, norm1_g, w_in, q_a_g, w_q_b, kv_a_g, w_kv_b, conv_w, a_log, dt_bias, gdn_norm_g,
                   w_out, norm2_g, w_ff1, w_ff2):
    d = w_in.shape[1]
    wi = w_in[i]
    o = 0
    cols = []
    for sz in (MLA_Q_RANK, MLA_KV_RANK, MLA_ROPE_DIM, 3 * GDN_WIDTH, GDN_WIDTH,
               N_DIR * GDN_HEADS, N_DIR * GDN_HEADS):
        cols.append(wi[:, o:o + sz])
        o += sz
    w_qa, w_kva, w_kr, w_qkv, w_z, w_a, w_b = cols
    w_kr_pad = jnp.zeros((d, LANES), F32).at[:, MLA_NOPE_DIM:MLA_QK_DIM].set(w_kr)
    nab = N_DIR * GDN_HEADS
    w_ab_pad = jnp.zeros((d, LANES), F32).at[:, 0:nab].set(w_a).at[:, nab:2 * nab].set(w_b)
    w_main = jnp.concatenate([w_qa, w_kva, w_kr_pad, w_z, w_ab_pad], axis=1)
    kv = w_kv_b[i].reshape(MLA_KV_RANK, MLA_HEADS, MLA_NOPE_DIM + MLA_V_DIM)
    w_k = _pad_heads(kv[:, :, :MLA_NOPE_DIM].reshape(MLA_KV_RANK, -1), MLA_NOPE_DIM)
    w_v = _pad_heads(kv[:, :, MLA_NOPE_DIM:].reshape(MLA_KV_RANK, -1), MLA_V_DIM)
    row = lambda v: jnp.zeros((1, LANES), F32).at[0, 0:nab].set(v.reshape(-1))
    return dict(
        norm1_g=norm1_g[i][None, :],
        w_all=jnp.concatenate([w_qkv, w_main], axis=1).astype(BF16),
        q_a_g=q_a_g[i][None, :],
        kv_a_g=kv_a_g[i][None, :],
        w_q=_pad_heads(w_q_b[i], MLA_QK_DIM).astype(BF16),
        w_k=w_k.astype(BF16),
        w_v=w_v.astype(BF16),
        conv_w=jnp.pad(conv_w[i], ((0, 8 - GDN_CONV), (0, 0))),
        a_log=row(a_log[i]),
        dt_bias=row(dt_bias[i]),
        gdn_norm_g=gdn_norm_g[i][None, :],
        w_out=w_out[i].astype(BF16),
        norm2_g=norm2_g[i][None, :],
        w_ff1=w_ff1[i].astype(BF16),
        w_ff2=w_ff2[i].astype(BF16),
    )


def kernel(x, c, ctx, c_ctx, w_ada, b_ada, norm1_g, w_in, q_a_g, w_q_b, kv_a_g, w_kv_b, conv_w, a_log,
           dt_bias, gdn_norm_g, w_out, norm2_g, w_ff1, w_ff2, final_norm_g):
    b, t_lat, d = x.shape
    t_ctx = ctx.shape[1]
    depth = w_ada.shape[0]
    assert t_ctx % TOKEN_TILE == 0 and t_lat % TOKEN_TILE == 0 and t_lat % GRID_W == 0
    assert b + 1 <= PAD_ROWS
    n_ctx_tiles = t_ctx // TOKEN_TILE

    xs = jnp.concatenate([ctx, x], axis=1)
    cc = jnp.zeros((PAD_ROWS, d), F32).at[0:b].set(c).at[b].set(c_ctx)
    mods = _ada_call(cc, w_ada, b_ada)
    mod = jnp.stack([mods[:, 0:b], jnp.broadcast_to(mods[:, b:b + 1], (depth, b, 6 * d))], axis=2)
    mod = mod[:, :, :, None, :]
    tabs = _rope_tables(t_lat, t_ctx)
    cm = _chunk_sum_matrices()
    shift = _conv_shift_matrix()
    fng = final_norm_g[None, :]

    for i in range(depth):
        last = i == depth - 1
        wts = _layer_weights(i, norm1_g, w_in, q_a_g, w_q_b, kv_a_g, w_kv_b, conv_w, a_log, dt_bias,
                             gdn_norm_g, w_out, norm2_g, w_ff1, w_ff2)
        qt, k, vt, gq, gk, gv, gb, z = _proj_call(xs, mod, i, wts, tabs, shift, n_ctx_tiles)
        attn = _attn_call(qt, k, vt, t_ctx)
        o_f, o_b = _gdn_call(gq, gk, gv, gb, cm, t_ctx // GDN_BLOCK)
        xs = _out_call(xs, attn, o_f, o_b, z, mod, i, wts, fng, n_ctx_tiles, last)
    return xs
```

```python
import functools

import jax
import jax.numpy as jnp
from jax import lax
from jax.experimental import pallas as pl
from jax.experimental.pallas import tpu as pltpu

F32 = jnp.float32
BF16 = jnp.bfloat16

GRID_W = 64
MLA_HEADS = 8
MLA_NOPE_DIM = 64
MLA_ROPE_DIM = 32
MLA_V_DIM = 64
MLA_Q_RANK = 256
MLA_KV_RANK = 128
MLA_QK_DIM = MLA_NOPE_DIM + MLA_ROPE_DIM
GDN_HEADS = 4
GDN_HEAD_DIM = 128
GDN_WIDTH = GDN_HEADS * GDN_HEAD_DIM
GDN_CONV = 5
GDN_CHUNK = 64
N_DIR = 2
ROPE_THETA = 10000.0
NORM_EPS = 1e-6
LOG2_E = 1.4426950408889634

LANES = 128
BF16_SUBLANES = 16
VMEM_LIMIT_BYTES = 56 * 1024 * 1024

TOKEN_TILE = 256
GDN_BLOCK = 256
GDN_BATCH = 2
PROJ_BATCH = 2
GDN_WAVE_LAG = 0
ATTN_KEY_TILE = 256
ATTN_SUBTILES = 2
NEG_BIG = -1e30
PAD_ROWS = 16


def _sigmoid(x):
    return 1.0 / (1.0 + jnp.exp(-x))


def _silu(x):
    return x * _sigmoid(x)


def _dot(a, b):
    return jnp.dot(a, b, preferred_element_type=F32)


def _dot_nt(a, b):
    return lax.dot_general(a, b, (((1,), (1,)), ((), ())), preferred_element_type=F32)


def _dot_tn(a, b):
    return lax.dot_general(a, b, (((0,), (0,)), ((), ())), preferred_element_type=F32)


def _const_spec(shape):
    nd = len(shape)
    return pl.BlockSpec(shape, lambda *_: (0,) * nd, pipeline_mode=pl.Buffered(1))


def _params(*sem):
    return pltpu.CompilerParams(dimension_semantics=sem, vmem_limit_bytes=VMEM_LIMIT_BYTES)


def _ada_kernel(c_ref, w_ref, b_ref, o_ref):
    s = _silu(c_ref[...]).astype(BF16)
    o_ref[0] = _dot(s, w_ref[0].astype(BF16)) + b_ref[0]


def _ada_call(cc, w_ada, b_ada):
    n_layers, d, d6 = w_ada.shape
    tn = 1536
    return pl.pallas_call(
        _ada_kernel,
        grid=(n_layers, d6 // tn),
        in_specs=[
            pl.BlockSpec((PAD_ROWS, d), lambda l, n: (0, 0)),
            pl.BlockSpec((1, d, tn), lambda l, n: (l, 0, n)),
            pl.BlockSpec((1, 1, tn), lambda l, n: (l, 0, n)),
        ],
        out_specs=pl.BlockSpec((1, PAD_ROWS, tn), lambda l, n: (l, 0, n)),
        out_shape=jax.ShapeDtypeStruct((n_layers, PAD_ROWS, d6), F32),
        compiler_params=_params("parallel", "parallel"),
        name="ada",
    )(cc, w_ada, b_ada.reshape(n_layers, 1, d6))


def _rope(x, c, sa, sb):
    return x * c + pltpu.roll(x, LANES - 16, 1) * sa + pltpu.roll(x, 16, 1) * sb


def _proj_kernel(xc_ref, xp_ref, xn_ref, mod_ref, n1g_ref, wall_ref, qag_ref, kvag_ref,
                 wq_ref, wk_ref, wv_ref, ctab_ref, sa_ref, sb_ref, convw_ref, shift_ref, alog_ref, dtb_ref,
                 qt_ref, k_ref, vt_ref, gq_ref, gk_ref, gv_ref, gb_ref, zs_ref,
                 hext_ref, pqkv_ref, cbuf_ref, *, tm, n_ctx_tiles, n_tiles, d_model):
    t = pl.program_id(1)
    halo = BF16_SUBLANES
    n_qkv = 3 * GDN_WIDTH
    n1g = n1g_ref[...]
    seg_first = jnp.logical_or(t == 0, t == n_ctx_tiles)
    seg_last = jnp.logical_or(t == n_ctx_tiles - 1, t == n_tiles - 1)
    o_kva = MLA_Q_RANK
    o_kr = o_kva + MLA_KV_RANK
    o_z = o_kr + LANES
    o_ab = o_z + GDN_WIDTH
    ctab = ctab_ref[...]
    sa = sa_ref[...]
    sb = sb_ref[...]
    cw = convw_ref[...]
    rows = tm + 2 * halo
    lane = lax.broadcasted_iota(jnp.int32, (tm, LANES), 1)
    pms = {}
    convs = {}

    def st_front(bb):
        mod = mod_ref[bb]
        sh1 = mod[:, 0:d_model]
        gs = n1g * (1.0 + mod[:, d_model:2 * d_model])

        def modulate(x):
            y = x * lax.rsqrt(jnp.mean(x * x, axis=-1, keepdims=True) + NORM_EPS)
            return (y * gs + sh1).astype(BF16)

        hext_ref[bb, 0:halo, :] = modulate(xp_ref[bb])
        hext_ref[bb, halo:halo + tm, :] = modulate(xc_ref[bb])
        hext_ref[bb, halo + tm:, :] = modulate(xn_ref[bb])

    def st_front_mm(bb):
        pqkv_ref[bb] = _dot(hext_ref[bb], wall_ref[:, 0:n_qkv])

        @pl.when(seg_first)
        def _():
            pqkv_ref[bb, 0:halo, :] = jnp.zeros((halo, n_qkv), F32)

        @pl.when(seg_last)
        def _():
            pqkv_ref[bb, halo + tm:, :] = jnp.zeros((halo, n_qkv), F32)

        pms[bb] = _dot(hext_ref[bb, halo:halo + tm, :], wall_ref[:, n_qkv:])

    def st_mla(bb):
        pm = pms[bb]
        qa = pm[:, 0:o_kva]
        kva = pm[:, o_kva:o_kr]
        kr = pm[:, o_kr:o_z]
        qn = (qa * lax.rsqrt(jnp.mean(qa * qa, axis=-1, keepdims=True) + NORM_EPS) * qag_ref[...]).astype(BF16)
        q = _dot(qn, wq_ref[...])
        scale = MLA_QK_DIM ** -0.5 * LOG2_E
        for h in range(MLA_HEADS):
            qh = q[:, h * LANES:(h + 1) * LANES]
            qt_ref[bb, h] = (_rope(qh, ctab, sa, sb) * scale).T.astype(BF16)

        kvn = (kva * lax.rsqrt(jnp.mean(kva * kva, axis=-1, keepdims=True) + NORM_EPS)
               * kvag_ref[...]).astype(BF16)
        kk = _dot(kvn, wk_ref[...])
        kpe = _rope(kr, ctab, sa, sb)
        for h in range(MLA_HEADS):
            k_ref[bb, :, h * LANES:(h + 1) * LANES] = (kk[:, h * LANES:(h + 1) * LANES] + kpe).astype(BF16)
        vv = _dot(kvn, wv_ref[...])
        for h in range(MLA_HEADS):
            vh = jnp.where(lane == MLA_V_DIM, 1.0, vv[:, h * LANES:(h + 1) * LANES])
            vt_ref[bb, h] = vh.T.astype(BF16)

    def st_cbuf(bb):
        pm = pms[bb]
        zs_ref[bb] = _silu(pm[:, o_z:o_ab]).astype(BF16)
        for j in range(GDN_CONV):
            cbuf_ref[bb, j * rows:(j + 1) * rows, :] = (pqkv_ref[bb] * cw[j:j + 1, :]).astype(BF16)

    def st_conv_mm(bb):
        convs[bb] = _dot(shift_ref[...], cbuf_ref[bb])

    def st_post(bb):
        conv = convs[bb]
        ab = pms[bb][:, o_ab:o_ab + LANES]
        for cb in range(3 * GDN_HEADS):
            cs = slice(cb * LANES, (cb + 1) * LANES)
            act = _silu(conv[:, cs])
            grp, hh = divmod(cb, GDN_HEADS)
            hs = slice(hh * LANES, (hh + 1) * LANES)
            if grp < 2:
                act = act * lax.rsqrt(jnp.sum(act * act, axis=-1, keepdims=True) + NORM_EPS)
                (gq_ref if grp == 0 else gk_ref)[bb, :, hs] = act.astype(BF16)
            else:
                gv_ref[bb, :, hs] = act.astype(BF16)

        x = ab + dtb_ref[...]
        softplus = jnp.maximum(x, 0.0) + jnp.log(1.0 + jnp.exp(-jnp.abs(x)))
        gval = -jnp.exp(alog_ref[...]) * softplus
        gb_ref[bb] = jnp.where(lane < N_DIR * GDN_HEADS, gval, _sigmoid(ab))

    rows_b = range(PROJ_BATCH)
    for bb in rows_b:
        st_front(bb)
    st_front_mm(0)
    for bb in rows_b:
        if bb + 1 < PROJ_BATCH:
            st_front_mm(bb + 1)
        st_mla(bb)
        st_cbuf(bb)
        st_conv_mm(bb)
    for bb in rows_b:
        st_post(bb)


def _conv_shift_matrix():
    rows = TOKEN_TILE + 2 * BF16_SUBLANES
    r = jnp.arange(TOKEN_TILE)[:, None]
    col = jnp.arange(GDN_CONV * rows)[None, :]
    j = col // rows
    return (col % rows == BF16_SUBLANES + r + j - GDN_CONV // 2).astype(BF16)


def _proj_call(xs, mod, layer, wts, tabs, shift, n_ctx_tiles):
    b, ta, d = xs.shape
    tm = TOKEN_TILE
    n_tiles = ta // tm
    halo = BF16_SUBLANES
    hb = tm // halo
    n_halo_blocks = ta // halo
    d6 = 6 * d

    pb = PROJ_BATCH
    assert b % pb == 0
    tok = lambda w: pl.BlockSpec((pb, tm, w), lambda bi, t: (bi, t, 0))
    in_specs = [
        tok(d),
        pl.BlockSpec((pb, halo, d), lambda bi, t: (bi, jnp.maximum(t * hb - 1, 0), 0)),
        pl.BlockSpec((pb, halo, d), lambda bi, t: (bi, jnp.minimum((t + 1) * hb, n_halo_blocks - 1), 0)),
        pl.BlockSpec((None, pb, None, 1, d6),
                     lambda bi, t: (layer, bi, jnp.where(t < n_ctx_tiles, 1, 0), 0, 0)),
        _const_spec((1, d)),
        _const_spec(wts["w_all"].shape),
        _const_spec((1, MLA_Q_RANK)),
        _const_spec((1, MLA_KV_RANK)),
        _const_spec((MLA_Q_RANK, MLA_HEADS * LANES)),
        _const_spec((MLA_KV_RANK, MLA_HEADS * LANES)),
        _const_spec((MLA_KV_RANK, MLA_HEADS * LANES)),
        pl.BlockSpec((tm, LANES), lambda bi, t: (t, 0)),
        pl.BlockSpec((tm, LANES), lambda bi, t: (t, 0)),
        pl.BlockSpec((tm, LANES), lambda bi, t: (t, 0)),
        _const_spec((8, 3 * GDN_WIDTH)),
        _const_spec(shift.shape),
        _const_spec((1, LANES)),
        _const_spec((1, LANES)),
    ]
    head_t = pl.BlockSpec((pb, MLA_HEADS, LANES, tm), lambda bi, t: (bi, 0, 0, t))
    out_specs = [
        head_t,
        tok(MLA_HEADS * LANES),
        head_t,
        tok(GDN_WIDTH), tok(GDN_WIDTH), tok(GDN_WIDTH),
        tok(LANES),
        tok(GDN_WIDTH),
    ]
    out_shape = [
        jax.ShapeDtypeStruct((b, MLA_HEADS, LANES, ta), BF16),
        jax.ShapeDtypeStruct((b, ta, MLA_HEADS * LANES), BF16),
        jax.ShapeDtypeStruct((b, MLA_HEADS, LANES, ta), BF16),
        jax.ShapeDtypeStruct((b, ta, GDN_WIDTH), BF16),
        jax.ShapeDtypeStruct((b, ta, GDN_WIDTH), BF16),
        jax.ShapeDtypeStruct((b, ta, GDN_WIDTH), BF16),
        jax.ShapeDtypeStruct((b, ta, LANES), F32),
        jax.ShapeDtypeStruct((b, ta, GDN_WIDTH), BF16),
    ]
    kern = functools.partial(_proj_kernel, tm=tm, n_ctx_tiles=n_ctx_tiles, n_tiles=n_tiles, d_model=d)
    return pl.pallas_call(
        kern,
        grid=(b // pb, n_tiles),
        in_specs=in_specs,
        out_specs=out_specs,
        out_shape=out_shape,
        scratch_shapes=[pltpu.VMEM((pb, tm + 2 * halo, d), BF16),
                        pltpu.VMEM((pb, tm + 2 * halo, 3 * GDN_WIDTH), F32),
                        pltpu.VMEM((pb, GDN_CONV * (tm + 2 * halo), 3 * GDN_WIDTH), BF16)],
        compiler_params=_params("parallel", "parallel"),
        name="proj",
    )(xs, xs, xs, mod, wts["norm1_g"], wts["w_all"], wts["q_a_g"], wts["kv_a_g"],
      wts["w_q"], wts["w_k"], wts["w_v"], tabs[0], tabs[1], tabs[2], wts["conv_w"], shift,
      wts["a_log"], wts["dt_bias"])


def _attn_kernel(qt_ref, k_ref, vt_ref, o_ref, s_ref, *, variants, tq):
    qi = pl.program_id(1)
    kt = ATTN_KEY_TILE

    def run(sub_nks):
        units = [(sub, h, nk // kt) for sub, nk in enumerate(sub_nks) for h in range(MLA_HEADS)]

        def score_tile(u, t, slot):
            sub, h, _ = units[u]
            s = _dot(k_ref[t * kt:(t + 1) * kt, h * LANES:(h + 1) * LANES],
                     qt_ref[h, :, sub * tq:(sub + 1) * tq])
            s_ref[slot, t * kt:(t + 1) * kt, :] = s
            return jnp.max(s, axis=0, keepdims=True)

        def pv_tile(u, t, slot, m):
            h = units[u][1]
            p = jnp.exp2(s_ref[slot, t * kt:(t + 1) * kt, :] - m).astype(BF16)
            return _dot(vt_ref[h, :, t * kt:(t + 1) * kt], p)

        outs = []
        m = functools.reduce(jnp.maximum, [score_tile(0, t, 0) for t in range(units[0][2])])
        for u, (sub, h, nt) in enumerate(units):
            slot = u % 2
            nt_next = units[u + 1][2] if u + 1 < len(units) else 0
            acc = None
            m_parts = []
            for t in range(max(nt, nt_next)):
                if t < nt_next:
                    m_parts.append(score_tile(u + 1, t, 1 - slot))
                if t < nt:
                    o = pv_tile(u, t, slot, m)
                    acc = o if acc is None else acc + o
            outs.append(acc[0:MLA_V_DIM] / acc[MLA_V_DIM:MLA_V_DIM + 1])
            if m_parts:
                m = functools.reduce(jnp.maximum, m_parts)
            if h == MLA_HEADS - 1:
                o_ref[sub * tq:(sub + 1) * tq, :] = jnp.concatenate(outs, axis=0).T.astype(BF16)
                outs = []

    for sub_nks, steps in variants.items():
        @pl.when(functools.reduce(jnp.logical_or, [qi == i for i in steps]))
        def _(sub_nks=sub_nks):
            run(sub_nks)


def _attn_call(qt, k, vt, tc):
    b, ta, _ = k.shape
    tq = TOKEN_TILE
    n_ctx_q = tc // tq
    nq = ta // tq
    n_sub = ATTN_SUBTILES
    n_steps = pl.cdiv(nq, n_sub)
    variants = {}
    for i in range(n_steps):
        key = tuple(tc if g < n_ctx_q else ta for g in range(i * n_sub, min((i + 1) * n_sub, nq)))
        variants.setdefault(key, []).append(i)
    kern = functools.partial(_attn_kernel, variants=variants, tq=tq)
    return pl.pallas_call(
        kern,
        grid=(b, n_steps),
        in_specs=[
            pl.BlockSpec((None, MLA_HEADS, LANES, n_sub * tq), lambda bi, qi: (bi, 0, 0, qi)),
            pl.BlockSpec((None, ta, MLA_HEADS * LANES), lambda bi, qi: (bi, 0, 0)),
            pl.BlockSpec((None, MLA_HEADS, LANES, ta), lambda bi, qi: (bi, 0, 0, 0)),
        ],
        out_specs=pl.BlockSpec((None, n_sub * tq, MLA_HEADS * MLA_V_DIM), lambda bi, qi: (bi, qi, 0)),
        out_shape=jax.ShapeDtypeStruct((b, nq * tq, MLA_HEADS * MLA_V_DIM), BF16),
        scratch_shapes=[pltpu.VMEM((2, ta, tq), F32)],
        compiler_params=_params("parallel", "arbitrary"),
        name="attn",
    )(qt, k, vt)


def _gdn_kernel(qf_ref, kf_ref, vf_ref, gf_ref, qb_ref, kb_ref, vb_ref, gbk_ref, cm_ref,
                of_ref, ob_ref, s_ref):
    j = pl.program_id(1)
    blk = GDN_BLOCK
    c = GDN_CHUNK
    n_chunks = blk // c

    @pl.when(j == 0)
    def _():
        s_ref[...] = jnp.zeros(s_ref.shape, F32)

    row = lax.broadcasted_iota(jnp.int32, (blk, blk), 0)
    col = lax.broadcasted_iota(jnp.int32, (blk, blk), 1)
    same = (row // c) == (col // c)
    eye = jnp.where(row == col, 1.0, 0.0)
    pair_masks = []
    size = 1
    while size < c:
        pair_masks.append(jnp.logical_and((row // (2 * size)) == (col // (2 * size)),
                                          (row // size) != (col // size)))
        size *= 2
    scale = GDN_HEAD_DIM ** -0.5

    dir_refs = ((qf_ref, kf_ref, vf_ref, gf_ref, of_ref), (qb_ref, kb_ref, vb_ref, gbk_ref, ob_ref))

    def st_setup(wave):
        bb = wave["bb"]
        chains = wave["chains"] = []
        for d in range(N_DIR):
            q_ref, k_ref, v_ref, g_ref, o_ref = (r.at[bb] for r in dir_refs[d])
            incl = jnp.logical_and(same, (row >= col) if d == 0 else (row <= col))
            strict = jnp.logical_and(same, (row > col) if d == 0 else (row < col))
            gb = g_ref[...]
            g1 = gb.astype(BF16)
            r1 = gb - g1.astype(F32)
            g2 = r1.astype(BF16)
            g3 = (r1 - g2.astype(F32)).astype(BF16)
            cmd = cm_ref[d]
            gc = _dot(cmd, g1) + _dot(cmd, g2) + _dot(cmd, g3)
            last = c - 1 if d == 0 else 0
            gtot = jnp.concatenate(
                [jnp.broadcast_to(gc[ci * c + last:ci * c + last + 1, :], (c, LANES)) for ci in range(n_chunks)],
                axis=0)
            gct = gc.T
            for h in range(GDN_HEADS):
                ln = d * GDN_HEADS + h
                hs = slice(h * LANES, (h + 1) * LANES)
                gcol = gc[:, ln:ln + 1]
                gl = gtot[:, ln:ln + 1]
                beta = gb[:, N_DIR * GDN_HEADS + ln:N_DIR * GDN_HEADS + ln + 1]
                decay = jnp.exp(jnp.where(incl, gcol - gct[ln:ln + 1, :], NEG_BIG))
                k16 = k_ref[:, hs]
                q16 = q_ref[:, hs]
                k = k16.astype(F32)
                kbeta = k * beta
                egc = jnp.exp(gcol)
                chains.append(dict(
                    d=d, hs=hs, o_ref=o_ref, idx=bb * N_DIR * GDN_HEADS + ln, strict=strict, decay=decay,
                    kb16=k16, kbeta16=kbeta.astype(BF16), q16=q16,
                    rhs16=jnp.concatenate([v_ref[:, hs].astype(F32) * beta, kbeta * egc], axis=1).astype(BF16),
                    qd=q16.astype(F32) * (egc * scale), kt=k * jnp.exp(gl - gcol), glast=jnp.exp(gl)))

    def st_gram(wave):
        for ch in wave["chains"]:
            gram = _dot_nt(ch["kbeta16"], ch["kb16"])
            lower = jnp.where(ch["strict"], gram * ch["decay"], 0.0)
            ch["lower16"] = lower.astype(BF16)
            ch["t16"] = (eye - jnp.where(pair_masks[0], lower, 0.0)).astype(BF16)

    def st_merge_a(pm16):
        def run(wave):
            for ch in wave["chains"]:
                ch["p16"] = _dot(ch["lower16"] * pm16, ch["t16"]).astype(BF16)
        return run

    def st_merge_b(wave):
        for ch in wave["chains"]:
            ch["t16"] = ch["t16"] - _dot(ch["t16"], ch["p16"]).astype(BF16)

    def half_rows(x, s, d):
        x3 = x.reshape(blk // (2 * s), 2 * s, x.shape[1])
        return (x3[:, s:, :] if d == 0 else x3[:, :s, :]).reshape(blk // 2, x.shape[1])

    def st_merge_half_a(pm16, s):
        pm_half = [half_rows(pm16, s, d) for d in range(N_DIR)]

        def run(wave):
            for ch in wave["chains"]:
                e_half = half_rows(ch["lower16"], s, ch["d"]) * pm_half[ch["d"]]
                ch["p_half16"] = _dot(e_half, ch["t16"]).astype(BF16)
        return run

    def st_merge_half_b(s):
        def run(wave):
            n_pairs = blk // (2 * s)
            for ch in wave["chains"]:
                t3 = ch["t16"].reshape(n_pairs, 2 * s, blk)
                p3 = ch["p_half16"].reshape(n_pairs, s, blk)
                zero = jnp.zeros_like(p3)
                if ch["d"] == 0:
                    p_full = jnp.concatenate([zero, p3], axis=1).reshape(blk, blk)
                    r = _dot(t3[:, s:, :].reshape(blk // 2, blk), p_full).astype(BF16)
                    parts = [t3[:, :s, :], t3[:, s:, :] - r.reshape(n_pairs, s, blk)]
                else:
                    p_full = jnp.concatenate([p3, zero], axis=1).reshape(blk, blk)
                    r = _dot(t3[:, :s, :].reshape(blk // 2, blk), p_full).astype(BF16)
                    parts = [t3[:, :s, :] - r.reshape(n_pairs, s, blk), t3[:, s:, :]]
                ch["t16"] = jnp.concatenate(parts, axis=1).reshape(blk, blk)
        return run

    def st_apply(wave):
        for ch in wave["chains"]:
            x = _dot(ch["t16"], ch["rhs16"])
            ch["u"] = x[:, 0:LANES]
            ch["w"] = x[:, LANES:2 * LANES]
            ch["qk"] = _dot_nt(ch["q16"], ch["kb16"]) * (ch["decay"] * scale)

    def st_prep(wave):
        for ch in wave["chains"]:
            ch["s"] = s_ref[ch["idx"]]
            ch["wq16"] = [jnp.concatenate([ch["w"][ci * c:(ci + 1) * c], ch["qd"][ci * c:(ci + 1) * c]],
                                          axis=0).astype(BF16) for ci in range(n_chunks)]
            ch["ktt16"] = [ch["kt"][ci * c:(ci + 1) * c].T.astype(BF16) for ci in range(n_chunks)]
            ch["qk16"] = [ch["qk"][ci * c:(ci + 1) * c, ci * c:(ci + 1) * c].astype(BF16)
                          for ci in range(n_chunks)]

    def st_step_a(step):
        def run(wave):
            for ch in wave["chains"]:
                ci = ch["ci"] = step if ch["d"] == 0 else n_chunks - 1 - step
                ch["ws"] = _dot(ch["wq16"][ci], ch["s"].astype(BF16))
        return run

    def st_step_b(wave):
        for ch in wave["chains"]:
            ci, ws = ch["ci"], ch["ws"]
            r = slice(ci * c, (ci + 1) * c)
            vn16 = (ch["u"][r] - ws[0:c]).astype(BF16)
            o = ws[c:2 * c] + _dot(ch["qk16"][ci], vn16)
            glast = ch["glast"][ci * c:ci * c + 1, :]
            ch["s"] = ch["s"] * glast + _dot(ch["ktt16"][ci], vn16)
            ch["o_ref"][r, ch["hs"]] = o.astype(BF16)

    def st_store(wave):
        for ch in wave["chains"]:
            s_ref[ch["idx"]] = ch["s"]

    stages = [st_setup, st_gram]
    for lvl, pm in enumerate(pair_masks[1:], start=1):
        pm16 = jnp.where(pm, 1.0, 0.0).astype(BF16)
        s = 2 ** lvl
        if s >= BF16_SUBLANES:
            stages += [st_merge_half_a(pm16, s), st_merge_half_b(s)]
        else:
            stages += [st_merge_a(pm16), st_merge_b]
    stages += [st_apply, st_prep]
    for step in range(n_chunks):
        stages += [st_step_a(step), st_step_b]
    stages.append(st_store)

    waves = [dict(bb=bb) for bb in range(GDN_BATCH)]
    for tick in range(len(stages) + GDN_WAVE_LAG * (GDN_BATCH - 1)):
        for wi, wave in enumerate(waves):
            si = tick - GDN_WAVE_LAG * wi
            if 0 <= si < len(stages):
                stages[si](wave)


def _gdn_call(gq, gk, gv, gb, cm, n_ctx_blocks):
    b, ta, w = gq.shape
    blk = GDN_BLOCK
    nb = ta // blk

    def bwd_block(j):
        return jnp.where(j < n_ctx_blocks, n_ctx_blocks - 1 - j, nb - 1 - (j - n_ctx_blocks))

    gbt = GDN_BATCH
    assert b % gbt == 0
    f_spec = lambda wd: pl.BlockSpec((gbt, blk, wd), lambda bi, j: (bi, j, 0))
    b_spec = lambda wd: pl.BlockSpec((gbt, blk, wd), lambda bi, j: (bi, bwd_block(j), 0))
    return pl.pallas_call(
        _gdn_kernel,
        grid=(b // gbt, nb),
        in_specs=[f_spec(w), f_spec(w), f_spec(w), f_spec(LANES),
                  b_spec(w), b_spec(w), b_spec(w), b_spec(LANES),
                  _const_spec(cm.shape)],
        out_specs=[f_spec(w), b_spec(w)],
        out_shape=[jax.ShapeDtypeStruct((b, ta, w), BF16), jax.ShapeDtypeStruct((b, ta, w), BF16)],
        scratch_shapes=[pltpu.VMEM((gbt * N_DIR * GDN_HEADS, GDN_HEAD_DIM, GDN_HEAD_DIM), F32)],
        compiler_params=_params("parallel", "arbitrary"),
        name="gdn",
    )(gq, gk, gv, gb, gq, gk, gv, gb, cm)


def _out_kernel(x_ref, attn_ref, of_ref, ob_ref, zs_ref, mod_ref, gng_ref, wout_ref, n2g_ref,
                w1_ref, w2_ref, fng_ref, o_ref, *, d_model, final):
    mod = mod_ref[...]
    g1 = mod[:, 2 * d_model:3 * d_model]
    sh2 = mod[:, 3 * d_model:4 * d_model]
    sc2 = mod[:, 4 * d_model:5 * d_model]
    g2 = mod[:, 5 * d_model:6 * d_model]

    o = of_ref[...].astype(F32) + ob_ref[...].astype(F32)
    zs = zs_ref[...].astype(F32)
    gng = gng_ref[...]
    parts = [attn_ref[...]]
    for h in range(GDN_HEADS):
        hs = slice(h * LANES, (h + 1) * LANES)
        oh = o[:, hs]
        y = oh * lax.rsqrt(jnp.mean(oh * oh, axis=-1, keepdims=True) + NORM_EPS) * gng
        parts.append((y * zs[:, hs]).astype(BF16))
    mix = jnp.concatenate(parts, axis=1)
    x1 = x_ref[...] + g1 * _dot(mix, wout_ref[...])

    y2 = x1 * lax.rsqrt(jnp.mean(x1 * x1, axis=-1, keepdims=True) + NORM_EPS)
    h2 = ((y2 * n2g_ref[...]) * (1.0 + sc2) + sh2).astype(BF16)
    ff = jnp.maximum(_dot(h2, w1_ref[...]), 0.0)
    ff = (ff * ff).astype(BF16)
    x2 = x1 + g2 * _dot(ff, w2_ref[...])
    if final:
        x2 = x2 * lax.rsqrt(jnp.mean(x2 * x2, axis=-1, keepdims=True) + NORM_EPS) * fng_ref[...]
    o_ref[...] = x2


def _out_call(xs, attn, o_f, o_b, z, mod, layer, wts, fng, n_ctx_tiles, final):
    b, ta, d = xs.shape
    tm = TOKEN_TILE
    d6 = 6 * d
    first = n_ctx_tiles if final else 0
    n_tiles = ta // tm - first
    tok = lambda w: pl.BlockSpec((None, tm, w), lambda bi, t: (bi, t + first, 0))
    kern = functools.partial(_out_kernel, d_model=d, final=final)
    out_rows = n_tiles * tm
    return pl.pallas_call(
        kern,
        grid=(b, n_tiles),
        in_specs=[
            tok(d),
            tok(MLA_HEADS * MLA_V_DIM),
            tok(GDN_WIDTH), tok(GDN_WIDTH), tok(GDN_WIDTH),
            pl.BlockSpec((None, None, None, 1, d6),
                         lambda bi, t: (layer, bi, jnp.where(t + first < n_ctx_tiles, 1, 0), 0, 0)),
            _const_spec((1, GDN_HEAD_DIM)),
            _const_spec(wts["w_out"].shape),
            _const_spec((1, d)),
            _const_spec(wts["w_ff1"].shape),
            _const_spec(wts["w_ff2"].shape),
            _const_spec((1, d)),
        ],
        out_specs=pl.BlockSpec((None, tm, d), lambda bi, t: (bi, t, 0)),
        out_shape=jax.ShapeDtypeStruct((b, out_rows, d), F32),
        compiler_params=_params("parallel", "parallel"),
        name="out_final" if final else "out",
    )(xs, attn, o_f, o_b, z, mod, wts["gdn_norm_g"], wts["w_out"], wts["norm2_g"],
      wts["w_ff1"], wts["w_ff2"], fng)


def _rope_tables(t_lat, t_ctx):
    rows = t_lat // GRID_W
    row = jnp.broadcast_to(jnp.arange(rows)[:, None], (rows, GRID_W)).reshape(-1).astype(F32)
    col = jnp.broadcast_to(jnp.arange(GRID_W)[None, :], (rows, GRID_W)).reshape(-1).astype(F32)
    axis_pairs = MLA_ROPE_DIM // 4
    inv_freq = ROPE_THETA ** (-jnp.arange(axis_pairs, dtype=F32) / axis_pairs)
    ang = jnp.concatenate([row[:, None] * inv_freq, col[:, None] * inv_freq], axis=-1)
    cos, sin = jnp.cos(ang), jnp.sin(ang)
    half = MLA_ROPE_DIM // 2
    lo, mid, hi = MLA_NOPE_DIM, MLA_NOPE_DIM + half, MLA_NOPE_DIM + 2 * half
    ta = t_ctx + t_lat
    ctab = jnp.ones((ta, LANES), F32).at[t_ctx:, lo:mid].set(cos).at[t_ctx:, mid:hi].set(cos)
    sa = jnp.zeros((ta, LANES), F32).at[t_ctx:, lo:mid].set(-sin)
    sb = jnp.zeros((ta, LANES), F32).at[t_ctx:, mid:hi].set(sin)
    return ctab, sa, sb


def _chunk_sum_matrices():
    i = jnp.arange(GDN_BLOCK)
    same = (i[:, None] // GDN_CHUNK) == (i[None, :] // GDN_CHUNK)
    lower = same & (i[:, None] >= i[None, :])
    upper = same & (i[:, None] <= i[None, :])
    return jnp.stack([lower, upper], axis=0).astype(BF16)


def _pad_heads(w, used):
    k = w.shape[0]
    w = w.reshape(k, MLA_HEADS, used)
    return jnp.pad(w, ((0, 0), (0, 0), (0, LANES - used))).reshape(k, MLA_HEADS * LANES)


def _layer_weights(i, norm1_g, w_in, q_a_g, w_q_b, kv_a_g, w_kv_b, conv_w, a_log, dt_bias, gdn_norm_g,
                   w_out, norm2_g, w_ff1, w_ff2):
    d = w_in.shape[1]
    wi = w_in[i]
    o = 0
    cols = []
    for sz in (MLA_Q_RANK, MLA_KV_RANK, MLA_ROPE_DIM, 3 * GDN_WIDTH, GDN_WIDTH,
               N_DIR * GDN_HEADS, N_DIR * GDN_HEADS):
        cols.append(wi[:, o:o + sz])
        o += sz
    w_qa, w_kva, w_kr, w_qkv, w_z, w_a, w_b = cols
    w_kr_pad = jnp.zeros((d, LANES), F32).at[:, MLA_NOPE_DIM:MLA_QK_DIM].set(w_kr)
    nab = N_DIR * GDN_HEADS
    w_ab_pad = jnp.zeros((d, LANES), F32).at[:, 0:nab].set(w_a).at[:, nab:2 * nab].set(w_b)
    w_main = jnp.concatenate([w_qa, w_kva, w_kr_pad, w_z, w_ab_pad], axis=1)
    kv = w_kv_b[i].reshape(MLA_KV_RANK, MLA_HEADS, MLA_NOPE_DIM + MLA_V_DIM)
    w_k = _pad_heads(kv[:, :, :MLA_NOPE_DIM].reshape(MLA_KV_RANK, -1), MLA_NOPE_DIM)
    w_v = _pad_heads(kv[:, :, MLA_NOPE_DIM:].reshape(MLA_KV_RANK, -1), MLA_V_DIM)
    row = lambda v: jnp.zeros((1, LANES), F32).at[0, 0:nab].set(v.reshape(-1))
    return dict(
        norm1_g=norm1_g[i][None, :],
        w_all=jnp.concatenate([w_qkv, w_main], axis=1).astype(BF16),
        q_a_g=q_a_g[i][None, :],
        kv_a_g=kv_a_g[i][None, :],
        w_q=_pad_heads(w_q_b[i], MLA_QK_DIM).astype(BF16),
        w_k=w_k.astype(BF16),
        w_v=w_v.astype(BF16),
        conv_w=jnp.pad(conv_w[i], ((0, 8 - GDN_CONV), (0, 0))),
        a_log=row(a_log[i]),
        dt_bias=row(dt_bias[i]),
        gdn_norm_g=gdn_norm_g[i][None, :],
        w_out=w_out[i].astype(BF16),
        norm2_g=norm2_g[i][None, :],
        w_ff1=w_ff1[i].astype(BF16),
        w_ff2=w_ff2[i].astype(BF16),
    )


def kernel(x, c, ctx, c_ctx, w_ada, b_ada, norm1_g, w_in, q_a_g, w_q_b, kv_a_g, w_kv_b, conv_w, a_log,
           dt_bias, gdn_norm_g, w_out, norm2_g, w_ff1, w_ff2, final_norm_g):
    b, t_lat, d = x.shape
    t_ctx = ctx.shape[1]
    depth = w_ada.shape[0]
    assert t_ctx % TOKEN_TILE == 0 and t_lat % TOKEN_TILE == 0 and t_lat % GRID_W == 0
    assert b + 1 <= PAD_ROWS
    n_ctx_tiles = t_ctx // TOKEN_TILE

    xs = jnp.concatenate([ctx, x], axis=1)
    cc = jnp.zeros((PAD_ROWS, d), F32).at[0:b].set(c).at[b].set(c_ctx)
    mods = _ada_call(cc, w_ada, b_ada)
    mod = jnp.stack([mods[:, 0:b], jnp.broadcast_to(mods[:, b:b + 1], (depth, b, 6 * d))], axis=2)
    mod = mod[:, :, :, None, :]
    tabs = _rope_tables(t_lat, t_ctx)
    cm = _chunk_sum_matrices()
    shift = _conv_shift_matrix()
    fng = final_norm_g[None, :]

    for i in range(depth):
        last = i == depth - 1
        wts = _layer_weights(i, norm1_g, w_in, q_a_g, w_q_b, kv_a_g, w_kv_b, conv_w, a_log, dt_bias,
                             gdn_norm_g, w_out, norm2_g, w_ff1, w_ff2)
        qt, k, vt, gq, gk, gv, gb, z = _proj_call(xs, mod, i, wts, tabs, shift, n_ctx_tiles)
        attn = _attn_call(qt, k, vt, t_ctx)
        o_f, o_b = _gdn_call(gq, gk, gv, gb, cm, t_ctx // GDN_BLOCK)
        xs = _out_call(xs, attn, o_f, o_b, z, mod, i, wts, fng, n_ctx_tiles, last)
    return xs
```

```python
import functools

import jax
import jax.numpy as jnp
from jax import lax
from jax.experimental import pallas as pl
from jax.experimental.pallas import tpu as pltpu

F32 = jnp.float32
BF16 = jnp.bfloat16

GRID_W = 64
MLA_HEADS = 8
MLA_NOPE_DIM = 64
MLA_ROPE_DIM = 32
MLA_V_DIM = 64
MLA_Q_RANK = 256
MLA_KV_RANK = 128
MLA_QK_DIM = MLA_NOPE_DIM + MLA_ROPE_DIM
GDN_HEADS = 4
GDN_HEAD_DIM = 128
GDN_WIDTH = GDN_HEADS * GDN_HEAD_DIM
GDN_CONV = 5
GDN_CHUNK = 64
N_DIR = 2
ROPE_THETA = 10000.0
NORM_EPS = 1e-6
LOG2_E = 1.4426950408889634

LANES = 128
BF16_SUBLANES = 16
VMEM_LIMIT_BYTES = 56 * 1024 * 1024

TOKEN_TILE = 256
GDN_BLOCK = 256
GDN_BATCH = 2
PROJ_BATCH = 2
OUT_BATCH = 2
GDN_WAVE_LAG = 0
ATTN_KEY_TILE = 256
ATTN_SUBTILES = 2
NEG_BIG = -1e30
PAD_ROWS = 16


def _sigmoid(x):
    return 1.0 / (1.0 + jnp.exp(-x))


def _silu(x):
    return x * _sigmoid(x)


def _dot(a, b):
    return jnp.dot(a, b, preferred_element_type=F32)


def _dot_nt(a, b):
    return lax.dot_general(a, b, (((1,), (1,)), ((), ())), preferred_element_type=F32)


def _dot_tn(a, b):
    return lax.dot_general(a, b, (((0,), (0,)), ((), ())), preferred_element_type=F32)


def _const_spec(shape):
    nd = len(shape)
    return pl.BlockSpec(shape, lambda *_: (0,) * nd, pipeline_mode=pl.Buffered(1))


def _params(*sem):
    return pltpu.CompilerParams(dimension_semantics=sem, vmem_limit_bytes=VMEM_LIMIT_BYTES)


def _ada_kernel(c_ref, w_ref, b_ref, o_ref):
    s = _silu(c_ref[...]).astype(BF16)
    o_ref[0] = _dot(s, w_ref[0].astype(BF16)) + b_ref[0]


def _ada_call(cc, w_ada, b_ada):
    n_layers, d, d6 = w_ada.shape
    tn = 1536
    return pl.pallas_call(
        _ada_kernel,
        grid=(n_layers, d6 // tn),
        in_specs=[
            pl.BlockSpec((PAD_ROWS, d), lambda l, n: (0, 0)),
            pl.BlockSpec((1, d, tn), lambda l, n: (l, 0, n)),
            pl.BlockSpec((1, 1, tn), lambda l, n: (l, 0, n)),
        ],
        out_specs=pl.BlockSpec((1, PAD_ROWS, tn), lambda l, n: (l, 0, n)),
        out_shape=jax.ShapeDtypeStruct((n_layers, PAD_ROWS, d6), F32),
        compiler_params=_params("parallel", "parallel"),
        name="ada",
    )(cc, w_ada, b_ada.reshape(n_layers, 1, d6))


def _rope(x, c, sa, sb):
    return x * c + pltpu.roll(x, LANES - 16, 1) * sa + pltpu.roll(x, 16, 1) * sb


def _proj_kernel(xc_ref, xp_ref, xn_ref, mod_ref, n1g_ref, wall_ref, qag_ref, kvag_ref,
                 wq_ref, wk_ref, wv_ref, ctab_ref, sa_ref, sb_ref, convw_ref, shift_ref, alog_ref, dtb_ref,
                 qt_ref, k_ref, vt_ref, gq_ref, gk_ref, gv_ref, gb_ref, zs_ref,
                 hext_ref, pqkv_ref, cbuf_ref, *, tm, n_ctx_tiles, n_tiles, d_model):
    t = pl.program_id(1)
    halo = BF16_SUBLANES
    n_qkv = 3 * GDN_WIDTH
    n1g = n1g_ref[...]
    seg_first = jnp.logical_or(t == 0, t == n_ctx_tiles)
    seg_last = jnp.logical_or(t == n_ctx_tiles - 1, t == n_tiles - 1)
    o_kva = MLA_Q_RANK
    o_kr = o_kva + MLA_KV_RANK
    o_z = o_kr + LANES
    o_ab = o_z + GDN_WIDTH
    ctab = ctab_ref[...]
    sa = sa_ref[...]
    sb = sb_ref[...]
    cw = convw_ref[...]
    rows = tm + 2 * halo
    lane = lax.broadcasted_iota(jnp.int32, (tm, LANES), 1)
    pms = {}
    convs = {}

    def st_front(bb):
        mod = mod_ref[bb]
        sh1 = mod[:, 0:d_model]
        gs = n1g * (1.0 + mod[:, d_model:2 * d_model])

        def modulate(x):
            y = x * lax.rsqrt(jnp.mean(x * x, axis=-1, keepdims=True) + NORM_EPS)
            return (y * gs + sh1).astype(BF16)

        hext_ref[bb, 0:halo, :] = modulate(xp_ref[bb])
        hext_ref[bb, halo:halo + tm, :] = modulate(xc_ref[bb])
        hext_ref[bb, halo + tm:, :] = modulate(xn_ref[bb])

    def st_front_mm(bb):
        pqkv_ref[bb] = _dot(hext_ref[bb], wall_ref[:, 0:n_qkv])

        @pl.when(seg_first)
        def _():
            pqkv_ref[bb, 0:halo, :] = jnp.zeros((halo, n_qkv), F32)

        @pl.when(seg_last)
        def _():
            pqkv_ref[bb, halo + tm:, :] = jnp.zeros((halo, n_qkv), F32)

        pms[bb] = _dot(hext_ref[bb, halo:halo + tm, :], wall_ref[:, n_qkv:])

    def st_mla(bb):
        pm = pms[bb]
        qa = pm[:, 0:o_kva]
        kva = pm[:, o_kva:o_kr]
        kr = pm[:, o_kr:o_z]
        qn = (qa * lax.rsqrt(jnp.mean(qa * qa, axis=-1, keepdims=True) + NORM_EPS) * qag_ref[...]).astype(BF16)
        q = _dot(qn, wq_ref[...])
        scale = MLA_QK_DIM ** -0.5 * LOG2_E
        for h in range(MLA_HEADS):
            qh = q[:, h * LANES:(h + 1) * LANES]
            qt_ref[bb, h] = (_rope(qh, ctab, sa, sb) * scale).T.astype(BF16)

        kvn = (kva * lax.rsqrt(jnp.mean(kva * kva, axis=-1, keepdims=True) + NORM_EPS)
               * kvag_ref[...]).astype(BF16)
        kk = _dot(kvn, wk_ref[...])
        kpe = _rope(kr, ctab, sa, sb)
        for h in range(MLA_HEADS):
            k_ref[bb, :, h * LANES:(h + 1) * LANES] = (kk[:, h * LANES:(h + 1) * LANES] + kpe).astype(BF16)
        vv = _dot(kvn, wv_ref[...])
        for h in range(MLA_HEADS):
            vh = jnp.where(lane == MLA_V_DIM, 1.0, vv[:, h * LANES:(h + 1) * LANES])
            vt_ref[bb, h] = vh.T.astype(BF16)

    def st_cbuf(bb):
        pm = pms[bb]
        zs_ref[bb] = _silu(pm[:, o_z:o_ab]).astype(BF16)
        for j in range(GDN_CONV):
            cbuf_ref[bb, j * rows:(j + 1) * rows, :] = (pqkv_ref[bb] * cw[j:j + 1, :]).astype(BF16)

    def st_conv_mm(bb):
        convs[bb] = _dot(shift_ref[...], cbuf_ref[bb])

    def st_post(bb):
        conv = convs[bb]
        ab = pms[bb][:, o_ab:o_ab + LANES]
        for cb in range(3 * GDN_HEADS):
            cs = slice(cb * LANES, (cb + 1) * LANES)
            act = _silu(conv[:, cs])
            grp, hh = divmod(cb, GDN_HEADS)
            hs = slice(hh * LANES, (hh + 1) * LANES)
            if grp < 2:
                act = act * lax.rsqrt(jnp.sum(act * act, axis=-1, keepdims=True) + NORM_EPS)
                (gq_ref if grp == 0 else gk_ref)[bb, :, hs] = act.astype(BF16)
            else:
                gv_ref[bb, :, hs] = act.astype(BF16)

        x = ab + dtb_ref[...]
        softplus = jnp.maximum(x, 0.0) + jnp.log(1.0 + jnp.exp(-jnp.abs(x)))
        gval = -jnp.exp(alog_ref[...]) * softplus
        gb_ref[bb] = jnp.where(lane < N_DIR * GDN_HEADS, gval, _sigmoid(ab))

    rows_b = range(PROJ_BATCH)
    for bb in rows_b:
        st_front(bb)
    st_front_mm(0)
    for bb in rows_b:
        if bb + 1 < PROJ_BATCH:
            st_front_mm(bb + 1)
        st_mla(bb)
        st_cbuf(bb)
        st_conv_mm(bb)
    for bb in rows_b:
        st_post(bb)


def _conv_shift_matrix():
    rows = TOKEN_TILE + 2 * BF16_SUBLANES
    r = jnp.arange(TOKEN_TILE)[:, None]
    col = jnp.arange(GDN_CONV * rows)[None, :]
    j = col // rows
    return (col % rows == BF16_SUBLANES + r + j - GDN_CONV // 2).astype(BF16)


def _proj_call(xs, mod, layer, wts, tabs, shift, n_ctx_tiles):
    b, ta, d = xs.shape
    tm = TOKEN_TILE
    n_tiles = ta // tm
    halo = BF16_SUBLANES
    hb = tm // halo
    n_halo_blocks = ta // halo
    d6 = 6 * d

    pb = PROJ_BATCH
    assert b % pb == 0
    tok = lambda w: pl.BlockSpec((pb, tm, w), lambda bi, t: (bi, t, 0))
    in_specs = [
        tok(d),
        pl.BlockSpec((pb, halo, d), lambda bi, t: (bi, jnp.maximum(t * hb - 1, 0), 0)),
        pl.BlockSpec((pb, halo, d), lambda bi, t: (bi, jnp.minimum((t + 1) * hb, n_halo_blocks - 1), 0)),
        pl.BlockSpec((None, pb, None, 1, d6),
                     lambda bi, t: (layer, bi, jnp.where(t < n_ctx_tiles, 1, 0), 0, 0)),
        _const_spec((1, d)),
        _const_spec(wts["w_all"].shape),
        _const_spec((1, MLA_Q_RANK)),
        _const_spec((1, MLA_KV_RANK)),
        _const_spec((MLA_Q_RANK, MLA_HEADS * LANES)),
        _const_spec((MLA_KV_RANK, MLA_HEADS * LANES)),
        _const_spec((MLA_KV_RANK, MLA_HEADS * LANES)),
        pl.BlockSpec((tm, LANES), lambda bi, t: (t, 0)),
        pl.BlockSpec((tm, LANES), lambda bi, t: (t, 0)),
        pl.BlockSpec((tm, LANES), lambda bi, t: (t, 0)),
        _const_spec((8, 3 * GDN_WIDTH)),
        _const_spec(shift.shape),
        _const_spec((1, LANES)),
        _const_spec((1, LANES)),
    ]
    head_t = pl.BlockSpec((pb, MLA_HEADS, LANES, tm), lambda bi, t: (bi, 0, 0, t))
    out_specs = [
        head_t,
        tok(MLA_HEADS * LANES),
        head_t,
        tok(GDN_WIDTH), tok(GDN_WIDTH), tok(GDN_WIDTH),
        tok(LANES),
        tok(GDN_WIDTH),
    ]
    out_shape = [
        jax.ShapeDtypeStruct((b, MLA_HEADS, LANES, ta), BF16),
        jax.ShapeDtypeStruct((b, ta, MLA_HEADS * LANES), BF16),
        jax.ShapeDtypeStruct((b, MLA_HEADS, LANES, ta), BF16),
        jax.ShapeDtypeStruct((b, ta, GDN_WIDTH), BF16),
        jax.ShapeDtypeStruct((b, ta, GDN_WIDTH), BF16),
        jax.ShapeDtypeStruct((b, ta, GDN_WIDTH), BF16),
        jax.ShapeDtypeStruct((b, ta, LANES), F32),
        jax.ShapeDtypeStruct((b, ta, GDN_WIDTH), BF16),
    ]
    kern = functools.partial(_proj_kernel, tm=tm, n_ctx_tiles=n_ctx_tiles, n_tiles=n_tiles, d_model=d)
    return pl.pallas_call(
        kern,
        grid=(b // pb, n_tiles),
        in_specs=in_specs,
        out_specs=out_specs,
        out_shape=out_shape,
        scratch_shapes=[pltpu.VMEM((pb, tm + 2 * halo, d), BF16),
                        pltpu.VMEM((pb, tm + 2 * halo, 3 * GDN_WIDTH), F32),
                        pltpu.VMEM((pb, GDN_CONV * (tm + 2 * halo), 3 * GDN_WIDTH), BF16)],
        compiler_params=_params("parallel", "parallel"),
        name="proj",
    )(xs, xs, xs, mod, wts["norm1_g"], wts["w_all"], wts["q_a_g"], wts["kv_a_g"],
      wts["w_q"], wts["w_k"], wts["w_v"], tabs[0], tabs[1], tabs[2], wts["conv_w"], shift,
      wts["a_log"], wts["dt_bias"])


def _attn_kernel(qt_ref, k_ref, vt_ref, o_ref, s_ref, *, variants, tq):
    qi = pl.program_id(1)
    kt = ATTN_KEY_TILE

    def run(sub_nks):
        units = [(sub, h, nk // kt) for sub, nk in enumerate(sub_nks) for h in range(MLA_HEADS)]

        def score_tile(u, t, slot):
            sub, h, _ = units[u]
            s = _dot(k_ref[t * kt:(t + 1) * kt, h * LANES:(h + 1) * LANES],
                     qt_ref[h, :, sub * tq:(sub + 1) * tq])
            s_ref[slot, t * kt:(t + 1) * kt, :] = s
            return jnp.max(s, axis=0, keepdims=True)

        def pv_tile(u, t, slot, m):
            h = units[u][1]
            p = jnp.exp2(s_ref[slot, t * kt:(t + 1) * kt, :] - m).astype(BF16)
            return _dot(vt_ref[h, :, t * kt:(t + 1) * kt], p)

        outs = []
        m = functools.reduce(jnp.maximum, [score_tile(0, t, 0) for t in range(units[0][2])])
        for u, (sub, h, nt) in enumerate(units):
            slot = u % 2
            nt_next = units[u + 1][2] if u + 1 < len(units) else 0
            acc = None
            m_parts = []
            for t in range(max(nt, nt_next)):
                if t < nt_next:
                    m_parts.append(score_tile(u + 1, t, 1 - slot))
                if t < nt:
                    o = pv_tile(u, t, slot, m)
                    acc = o if acc is None else acc + o
            outs.append(acc[0:MLA_V_DIM] / acc[MLA_V_DIM:MLA_V_DIM + 1])
            if m_parts:
                m = functools.reduce(jnp.maximum, m_parts)
            if h == MLA_HEADS - 1:
                o_ref[sub * tq:(sub + 1) * tq, :] = jnp.concatenate(outs, axis=0).T.astype(BF16)
                outs = []

    for sub_nks, steps in variants.items():
        @pl.when(functools.reduce(jnp.logical_or, [qi == i for i in steps]))
        def _(sub_nks=sub_nks):
            run(sub_nks)


def _attn_call(qt, k, vt, tc):
    b, ta, _ = k.shape
    tq = TOKEN_TILE
    n_ctx_q = tc // tq
    nq = ta // tq
    n_sub = ATTN_SUBTILES
    n_steps = pl.cdiv(nq, n_sub)
    variants = {}
    for i in range(n_steps):
        key = tuple(tc if g < n_ctx_q else ta for g in range(i * n_sub, min((i + 1) * n_sub, nq)))
        variants.setdefault(key, []).append(i)
    kern = functools.partial(_attn_kernel, variants=variants, tq=tq)
    return pl.pallas_call(
        kern,
        grid=(b, n_steps),
        in_specs=[
            pl.BlockSpec((None, MLA_HEADS, LANES, n_sub * tq), lambda bi, qi: (bi, 0, 0, qi)),
            pl.BlockSpec((None, ta, MLA_HEADS * LANES), lambda bi, qi: (bi, 0, 0)),
            pl.BlockSpec((None, MLA_HEADS, LANES, ta), lambda bi, qi: (bi, 0, 0, 0)),
        ],
        out_specs=pl.BlockSpec((None, n_sub * tq, MLA_HEADS * MLA_V_DIM), lambda bi, qi: (bi, qi, 0)),
        out_shape=jax.ShapeDtypeStruct((b, nq * tq, MLA_HEADS * MLA_V_DIM), BF16),
        scratch_shapes=[pltpu.VMEM((2, ta, tq), F32)],
        compiler_params=_params("parallel", "arbitrary"),
        name="attn",
    )(qt, k, vt)


def _gdn_kernel(qf_ref, kf_ref, vf_ref, gf_ref, qb_ref, kb_ref, vb_ref, gbk_ref, cm_ref,
                of_ref, ob_ref, s_ref):
    j = pl.program_id(1)
    blk = GDN_BLOCK
    c = GDN_CHUNK
    n_chunks = blk // c

    @pl.when(j == 0)
    def _():
        s_ref[...] = jnp.zeros(s_ref.shape, F32)

    row = lax.broadcasted_iota(jnp.int32, (blk, blk), 0)
    col = lax.broadcasted_iota(jnp.int32, (blk, blk), 1)
    same = (row // c) == (col // c)
    eye = jnp.where(row == col, 1.0, 0.0)
    pair_masks = []
    size = 1
    while size < c:
        pair_masks.append(jnp.logical_and((row // (2 * size)) == (col // (2 * size)),
                                          (row // size) != (col // size)))
        size *= 2
    scale = GDN_HEAD_DIM ** -0.5

    dir_refs = ((qf_ref, kf_ref, vf_ref, gf_ref, of_ref), (qb_ref, kb_ref, vb_ref, gbk_ref, ob_ref))

    def st_setup(wave):
        bb = wave["bb"]
        chains = wave["chains"] = []
        for d in range(N_DIR):
            q_ref, k_ref, v_ref, g_ref, o_ref = (r.at[bb] for r in dir_refs[d])
            incl = jnp.logical_and(same, (row >= col) if d == 0 else (row <= col))
            strict = jnp.logical_and(same, (row > col) if d == 0 else (row < col))
            gb = g_ref[...]
            g1 = gb.astype(BF16)
            r1 = gb - g1.astype(F32)
            g2 = r1.astype(BF16)
            g3 = (r1 - g2.astype(F32)).astype(BF16)
            cmd = cm_ref[d]
            gc = _dot(cmd, g1) + _dot(cmd, g2) + _dot(cmd, g3)
            last = c - 1 if d == 0 else 0
            gtot = jnp.concatenate(
                [jnp.broadcast_to(gc[ci * c + last:ci * c + last + 1, :], (c, LANES)) for ci in range(n_chunks)],
                axis=0)
            gct = gc.T
            for h in range(GDN_HEADS):
                ln = d * GDN_HEADS + h
                hs = slice(h * LANES, (h + 1) * LANES)
                gcol = gc[:, ln:ln + 1]
                gl = gtot[:, ln:ln + 1]
                beta = gb[:, N_DIR * GDN_HEADS + ln:N_DIR * GDN_HEADS + ln + 1]
                decay = jnp.exp(jnp.where(incl, gcol - gct[ln:ln + 1, :], NEG_BIG))
                k16 = k_ref[:, hs]
                q16 = q_ref[:, hs]
                k = k16.astype(F32)
                kbeta = k * beta
                egc = jnp.exp(gcol)
                chains.append(dict(
                    d=d, hs=hs, o_ref=o_ref, idx=bb * N_DIR * GDN_HEADS + ln, strict=strict, decay=decay,
                    kb16=k16, kbeta16=kbeta.astype(BF16), q16=q16,
                    rhs16=jnp.concatenate([v_ref[:, hs].astype(F32) * beta, kbeta * egc], axis=1).astype(BF16),
                    qd=q16.astype(F32) * (egc * scale), kt=k * jnp.exp(gl - gcol), glast=jnp.exp(gl)))

    def st_gram(wave):
        for ch in wave["chains"]:
            gram = _dot_nt(ch["kbeta16"], ch["kb16"])
            lower = jnp.where(ch["strict"], gram * ch["decay"], 0.0)
            ch["lower16"] = lower.astype(BF16)
            ch["t16"] = (eye - jnp.where(pair_masks[0], lower, 0.0)).astype(BF16)

    def st_merge_a(pm16):
        def run(wave):
            for ch in wave["chains"]:
                ch["p16"] = _dot(ch["lower16"] * pm16, ch["t16"]).astype(BF16)
        return run

    def st_merge_b(wave):
        for ch in wave["chains"]:
            ch["t16"] = ch["t16"] - _dot(ch["t16"], ch["p16"]).astype(BF16)

    def half_rows(x, s, d):
        x3 = x.reshape(blk // (2 * s), 2 * s, x.shape[1])
        return (x3[:, s:, :] if d == 0 else x3[:, :s, :]).reshape(blk // 2, x.shape[1])

    def st_merge_half_a(pm16, s):
        pm_half = [half_rows(pm16, s, d) for d in range(N_DIR)]

        def run(wave):
            for ch in wave["chains"]:
                e_half = half_rows(ch["lower16"], s, ch["d"]) * pm_half[ch["d"]]
                ch["p_half16"] = _dot(e_half, ch["t16"]).astype(BF16)
        return run

    def st_merge_half_b(s):
        def run(wave):
            n_pairs = blk // (2 * s)
            for ch in wave["chains"]:
                t3 = ch["t16"].reshape(n_pairs, 2 * s, blk)
                p3 = ch["p_half16"].reshape(n_pairs, s, blk)
                zero = jnp.zeros_like(p3)
                if ch["d"] == 0:
                    p_full = jnp.concatenate([zero, p3], axis=1).reshape(blk, blk)
                    r = _dot(t3[:, s:, :].reshape(blk // 2, blk), p_full).astype(BF16)
                    parts = [t3[:, :s, :], t3[:, s:, :] - r.reshape(n_pairs, s, blk)]
                else:
                    p_full = jnp.concatenate([p3, zero], axis=1).reshape(blk, blk)
                    r = _dot(t3[:, :s, :].reshape(blk // 2, blk), p_full).astype(BF16)
                    parts = [t3[:, :s, :] - r.reshape(n_pairs, s, blk), t3[:, s:, :]]
                ch["t16"] = jnp.concatenate(parts, axis=1).reshape(blk, blk)
        return run

    def st_apply(wave):
        for ch in wave["chains"]:
            x = _dot(ch["t16"], ch["rhs16"])
            ch["u"] = x[:, 0:LANES]
            ch["w"] = x[:, LANES:2 * LANES]
            ch["qk"] = _dot_nt(ch["q16"], ch["kb16"]) * (ch["decay"] * scale)

    def st_prep(wave):
        for ch in wave["chains"]:
            ch["s"] = s_ref[ch["idx"]]
            ch["wq16"] = [jnp.concatenate([ch["w"][ci * c:(ci + 1) * c], ch["qd"][ci * c:(ci + 1) * c]],
                                          axis=0).astype(BF16) for ci in range(n_chunks)]
            ch["ktt16"] = [ch["kt"][ci * c:(ci + 1) * c].T.astype(BF16) for ci in range(n_chunks)]
            ch["qk16"] = [ch["qk"][ci * c:(ci + 1) * c, ci * c:(ci + 1) * c].astype(BF16)
                          for ci in range(n_chunks)]

    def st_step_a(step):
        def run(wave):
            for ch in wave["chains"]:
                ci = ch["ci"] = step if ch["d"] == 0 else n_chunks - 1 - step
                ch["ws"] = _dot(ch["wq16"][ci], ch["s"].astype(BF16))
        return run

    def st_step_b(wave):
        for ch in wave["chains"]:
            ci, ws = ch["ci"], ch["ws"]
            r = slice(ci * c, (ci + 1) * c)
            vn16 = (ch["u"][r] - ws[0:c]).astype(BF16)
            o = ws[c:2 * c] + _dot(ch["qk16"][ci], vn16)
            glast = ch["glast"][ci * c:ci * c + 1, :]
            ch["s"] = ch["s"] * glast + _dot(ch["ktt16"][ci], vn16)
            ch["o_ref"][r, ch["hs"]] = o.astype(BF16)

    def st_store(wave):
        for ch in wave["chains"]:
            s_ref[ch["idx"]] = ch["s"]

    stages = [st_setup, st_gram]
    for lvl, pm in enumerate(pair_masks[1:], start=1):
        pm16 = jnp.where(pm, 1.0, 0.0).astype(BF16)
        s = 2 ** lvl
        if s >= BF16_SUBLANES:
            stages += [st_merge_half_a(pm16, s), st_merge_half_b(s)]
        else:
            stages += [st_merge_a(pm16), st_merge_b]
    stages += [st_apply, st_prep]
    for step in range(n_chunks):
        stages += [st_step_a(step), st_step_b]
    stages.append(st_store)

    waves = [dict(bb=bb) for bb in range(GDN_BATCH)]
    for tick in range(len(stages) + GDN_WAVE_LAG * (GDN_BATCH - 1)):
        for wi, wave in enumerate(waves):
            si = tick - GDN_WAVE_LAG * wi
            if 0 <= si < len(stages):
                stages[si](wave)


def _gdn_call(gq, gk, gv, gb, cm, n_ctx_blocks):
    b, ta, w = gq.shape
    blk = GDN_BLOCK
    nb = ta // blk

    def bwd_block(j):
        return jnp.where(j < n_ctx_blocks, n_ctx_blocks - 1 - j, nb - 1 - (j - n_ctx_blocks))

    gbt = GDN_BATCH
    assert b % gbt == 0
    f_spec = lambda wd: pl.BlockSpec((gbt, blk, wd), lambda bi, j: (bi, j, 0))
    b_spec = lambda wd: pl.BlockSpec((gbt, blk, wd), lambda bi, j: (bi, bwd_block(j), 0))
    return pl.pallas_call(
        _gdn_kernel,
        grid=(b // gbt, nb),
        in_specs=[f_spec(w), f_spec(w), f_spec(w), f_spec(LANES),
                  b_spec(w), b_spec(w), b_spec(w), b_spec(LANES),
                  _const_spec(cm.shape)],
        out_specs=[f_spec(w), b_spec(w)],
        out_shape=[jax.ShapeDtypeStruct((b, ta, w), BF16), jax.ShapeDtypeStruct((b, ta, w), BF16)],
        scratch_shapes=[pltpu.VMEM((gbt * N_DIR * GDN_HEADS, GDN_HEAD_DIM, GDN_HEAD_DIM), F32)],
        compiler_params=_params("parallel", "arbitrary"),
        name="gdn",
    )(gq, gk, gv, gb, gq, gk, gv, gb, cm)


def _out_kernel(x_ref, attn_ref, of_ref, ob_ref, zs_ref, mod_ref, gng_ref, wout_ref, n2g_ref,
                w1_ref, w2_ref, fng_ref, o_ref, *, d_model, final):
    gng = gng_ref[...]
    st = [dict() for _ in range(OUT_BATCH)]

    def st_mix(bb):
        o = of_ref[bb].astype(F32) + ob_ref[bb].astype(F32)
        zs = zs_ref[bb].astype(F32)
        parts = [attn_ref[bb]]
        for h in range(GDN_HEADS):
            hs = slice(h * LANES, (h + 1) * LANES)
            oh = o[:, hs]
            y = oh * lax.rsqrt(jnp.mean(oh * oh, axis=-1, keepdims=True) + NORM_EPS) * gng
            parts.append((y * zs[:, hs]).astype(BF16))
        st[bb]["mix"] = jnp.concatenate(parts, axis=1)

    def st_proj(bb):
        mod = mod_ref[bb]
        g1 = mod[:, 2 * d_model:3 * d_model]
        sh2 = mod[:, 3 * d_model:4 * d_model]
        sc2 = mod[:, 4 * d_model:5 * d_model]
        x1 = x_ref[bb] + g1 * _dot(st[bb]["mix"], wout_ref[...])
        y2 = x1 * lax.rsqrt(jnp.mean(x1 * x1, axis=-1, keepdims=True) + NORM_EPS)
        st[bb]["x1"] = x1
        st[bb]["h2"] = ((y2 * n2g_ref[...]) * (1.0 + sc2) + sh2).astype(BF16)

    def st_ff1(bb):
        ff = jnp.maximum(_dot(st[bb]["h2"], w1_ref[...]), 0.0)
        st[bb]["ff"] = (ff * ff).astype(BF16)

    def st_ff2(bb):
        g2 = mod_ref[bb][:, 5 * d_model:6 * d_model]
        x2 = st[bb]["x1"] + g2 * _dot(st[bb]["ff"], w2_ref[...])
        if final:
            x2 = x2 * lax.rsqrt(jnp.mean(x2 * x2, axis=-1, keepdims=True) + NORM_EPS) * fng_ref[...]
        o_ref[bb] = x2

    for stage in (st_mix, st_proj, st_ff1, st_ff2):
        for bb in range(OUT_BATCH):
            stage(bb)


def _out_call(xs, attn, o_f, o_b, z, mod, layer, wts, fng, n_ctx_tiles, final):
    b, ta, d = xs.shape
    tm = TOKEN_TILE
    d6 = 6 * d
    first = n_ctx_tiles if final else 0
    n_tiles = ta // tm - first
    ob = OUT_BATCH
    assert b % ob == 0
    tok = lambda w: pl.BlockSpec((ob, tm, w), lambda bi, t: (bi, t + first, 0))
    kern = functools.partial(_out_kernel, d_model=d, final=final)
    out_rows = n_tiles * tm
    return pl.pallas_call(
        kern,
        grid=(b // ob, n_tiles),
        in_specs=[
            tok(d),
            tok(MLA_HEADS * MLA_V_DIM),
            tok(GDN_WIDTH), tok(GDN_WIDTH), tok(GDN_WIDTH),
            pl.BlockSpec((None, ob, None, 1, d6),
                         lambda bi, t: (layer, bi, jnp.where(t + first < n_ctx_tiles, 1, 0), 0, 0)),
            _const_spec((1, GDN_HEAD_DIM)),
            _const_spec(wts["w_out"].shape),
            _const_spec((1, d)),
            _const_spec(wts["w_ff1"].shape),
            _const_spec(wts["w_ff2"].shape),
            _const_spec((1, d)),
        ],
        out_specs=pl.BlockSpec((ob, tm, d), lambda bi, t: (bi, t, 0)),
        out_shape=jax.ShapeDtypeStruct((b, out_rows, d), F32),
        compiler_params=_params("parallel", "parallel"),
        name="out_final" if final else "out",
    )(xs, attn, o_f, o_b, z, mod, wts["gdn_norm_g"], wts["w_out"], wts["norm2_g"],
      wts["w_ff1"], wts["w_ff2"], fng)


def _rope_tables(t_lat, t_ctx):
    rows = t_lat // GRID_W
    row = jnp.broadcast_to(jnp.arange(rows)[:, None], (rows, GRID_W)).reshape(-1).astype(F32)
    col = jnp.broadcast_to(jnp.arange(GRID_W)[None, :], (rows, GRID_W)).reshape(-1).astype(F32)
    axis_pairs = MLA_ROPE_DIM // 4
    inv_freq = ROPE_THETA ** (-jnp.arange(axis_pairs, dtype=F32) / axis_pairs)
    ang = jnp.concatenate([row[:, None] * inv_freq, col[:, None] * inv_freq], axis=-1)
    cos, sin = jnp.cos(ang), jnp.sin(ang)
    half = MLA_ROPE_DIM // 2
    lo, mid, hi = MLA_NOPE_DIM, MLA_NOPE_DIM + half, MLA_NOPE_DIM + 2 * half
    ta = t_ctx + t_lat
    ctab = jnp.ones((ta, LANES), F32).at[t_ctx:, lo:mid].set(cos).at[t_ctx:, mid:hi].set(cos)
    sa = jnp.zeros((ta, LANES), F32).at[t_ctx:, lo:mid].set(-sin)
    sb = jnp.zeros((ta, LANES), F32).at[t_ctx:, mid:hi].set(sin)
    return ctab, sa, sb


def _chunk_sum_matrices():
    i = jnp.arange(GDN_BLOCK)
    same = (i[:, None] // GDN_CHUNK) == (i[None, :] // GDN_CHUNK)
    lower = same & (i[:, None] >= i[None, :])
    upper = same & (i[:, None] <= i[None, :])
    return jnp.stack([lower, upper], axis=0).astype(BF16)


def _pad_heads(w, used):
    k = w.shape[0]
    w = w.reshape(k, MLA_HEADS, used)
    return jnp.pad(w, ((0, 0), (0, 0), (0, LANES - used))).reshape(k, MLA_HEADS * LANES)


def _layer_weights(i, norm1_g, w_in, q_a_g, w_q_b, kv_a_g, w_kv_b, conv_w, a_log, dt_bias, gdn_norm_g,
                   w_out, norm2_g, w_ff1, w_ff2):
    d = w_in.shape[1]
    wi = w_in[i]
    o = 0
    cols = []
    for sz in (MLA_Q_RANK, MLA_KV_RANK, MLA_ROPE_DIM, 3 * GDN_WIDTH, GDN_WIDTH,
               N_DIR * GDN_HEADS, N_DIR * GDN_HEADS):
        cols.append(wi[:, o:o + sz])
        o += sz
    w_qa, w_kva, w_kr, w_qkv, w_z, w_a, w_b = cols
    w_kr_pad = jnp.zeros((d, LANES), F32).at[:, MLA_NOPE_DIM:MLA_QK_DIM].set(w_kr)
    nab = N_DIR * GDN_HEADS
    w_ab_pad = jnp.zeros((d, LANES), F32).at[:, 0:nab].set(w_a).at[:, nab:2 * nab].set(w_b)
    w_main = jnp.concatenate([w_qa, w_kva, w_kr_pad, w_z, w_ab_pad], axis=1)
    kv = w_kv_b[i].reshape(MLA_KV_RANK, MLA_HEADS, MLA_NOPE_DIM + MLA_V_DIM)
    w_k = _pad_heads(kv[:, :, :MLA_NOPE_DIM].reshape(MLA_KV_RANK, -1), MLA_NOPE_DIM)
    w_v = _pad_heads(kv[:, :, MLA_NOPE_DIM:].reshape(MLA_KV_RANK, -1), MLA_V_DIM)
    row = lambda v: jnp.zeros((1, LANES), F32).at[0, 0:nab].set(v.reshape(-1))
    return dict(
        norm1_g=norm1_g[i][None, :],
        w_all=jnp.concatenate([w_qkv, w_main], axis=1).astype(BF16),
        q_a_g=q_a_g[i][None, :],
        kv_a_g=kv_a_g[i][None, :],
        w_q=_pad_heads(w_q_b[i], MLA_QK_DIM).astype(BF16),
        w_k=w_k.astype(BF16),
        w_v=w_v.astype(BF16),
        conv_w=jnp.pad(conv_w[i], ((0, 8 - GDN_CONV), (0, 0))),
        a_log=row(a_log[i]),
        dt_bias=row(dt_bias[i]),
        gdn_norm_g=gdn_norm_g[i][None, :],
        w_out=w_out[i].astype(BF16),
        norm2_g=norm2_g[i][None, :],
        w_ff1=w_ff1[i].astype(BF16),
        w_ff2=w_ff2[i].astype(BF16),
    )


def kernel(x, c, ctx, c_ctx, w_ada, b_ada, norm1_g, w_in, q_a_g, w_q_b, kv_a_g, w_kv_b, conv_w, a_log,
           dt_bias, gdn_norm_g, w_out, norm2_g, w_ff1, w_ff2, final_norm_g):
    b, t_lat, d = x.shape
    t_ctx = ctx.shape[1]
    depth = w_ada.shape[0]
    assert t_ctx % TOKEN_TILE == 0 and t_lat % TOKEN_TILE == 0 and t_lat % GRID_W == 0
    assert b + 1 <= PAD_ROWS
    n_ctx_tiles = t_ctx // TOKEN_TILE

    xs = jnp.concatenate([ctx, x], axis=1)
    cc = jnp.zeros((PAD_ROWS, d), F32).at[0:b].set(c).at[b].set(c_ctx)
    mods = _ada_call(cc, w_ada, b_ada)
    mod = jnp.stack([mods[:, 0:b], jnp.broadcast_to(mods[:, b:b + 1], (depth, b, 6 * d))], axis=2)
    mod = mod[:, :, :, None, :]
    tabs = _rope_tables(t_lat, t_ctx)
    cm = _chunk_sum_matrices()
    shift = _conv_shift_matrix()
    fng = final_norm_g[None, :]

    for i in range(depth):
        last = i == depth - 1
        wts = _layer_weights(i, norm1_g, w_in, q_a_g, w_q_b, kv_a_g, w_kv_b, conv_w, a_log, dt_bias,
                             gdn_norm_g, w_out, norm2_g, w_ff1, w_ff2)
        qt, k, vt, gq, gk, gv, gb, z = _proj_call(xs, mod, i, wts, tabs, shift, n_ctx_tiles)
        attn = _attn_call(qt, k, vt, t_ctx)
        o_f, o_b = _gdn_call(gq, gk, gv, gb, cm, t_ctx // GDN_BLOCK)
        xs = _out_call(xs, attn, o_f, o_b, z, mod, i, wts, fng, n_ctx_tiles, last)
    return xs
```

```python
import functools

import jax
import jax.numpy as jnp
from jax import lax
from jax.experimental import pallas as pl
from jax.experimental.pallas import tpu as pltpu

F32 = jnp.float32
BF16 = jnp.bfloat16

GRID_W = 64
MLA_HEADS = 8
MLA_NOPE_DIM = 64
MLA_ROPE_DIM = 32
MLA_V_DIM = 64
MLA_Q_RANK = 256
MLA_KV_RANK = 128
MLA_QK_DIM = MLA_NOPE_DIM + MLA_ROPE_DIM
GDN_HEADS = 4
GDN_HEAD_DIM = 128
GDN_WIDTH = GDN_HEADS * GDN_HEAD_DIM
GDN_CONV = 5
GDN_CHUNK = 64
N_DIR = 2
ROPE_THETA = 10000.0
NORM_EPS = 1e-6
LOG2_E = 1.4426950408889634

LANES = 128
BF16_SUBLANES = 16
VMEM_LIMIT_BYTES = 56 * 1024 * 1024

TOKEN_TILE = 256
GDN_BLOCK = 256
GDN_BATCH = 2
PROJ_BATCH = 2
OUT_BATCH = 2
GDN_WAVE_LAG = 0
ATTN_KEY_TILE = 256
ATTN_SUBTILES = 2
NEG_BIG = -1e30
PAD_ROWS = 16


def _sigmoid(x):
    return 1.0 / (1.0 + jnp.exp(-x))


def _silu(x):
    return x * _sigmoid(x)


def _dot(a, b):
    return jnp.dot(a, b, preferred_element_type=F32)


def _dot_nt(a, b):
    return lax.dot_general(a, b, (((1,), (1,)), ((), ())), preferred_element_type=F32)


def _dot_tn(a, b):
    return lax.dot_general(a, b, (((0,), (0,)), ((), ())), preferred_element_type=F32)


def _const_spec(shape):
    nd = len(shape)
    return pl.BlockSpec(shape, lambda *_: (0,) * nd, pipeline_mode=pl.Buffered(1))


def _layer_spec(stacked, layer):
    shape = stacked.shape[1:]
    return pl.BlockSpec((None,) + shape, lambda *_: (layer,) + (0,) * len(shape), pipeline_mode=pl.Buffered(1))


def _params(*sem):
    return pltpu.CompilerParams(dimension_semantics=sem, vmem_limit_bytes=VMEM_LIMIT_BYTES)


def _ada_kernel(c_ref, w_ref, b_ref, o_ref):
    s = _silu(c_ref[...]).astype(BF16)
    o_ref[0] = _dot(s, w_ref[0].astype(BF16)) + b_ref[0]


def _ada_call(cc, w_ada, b_ada):
    n_layers, d, d6 = w_ada.shape
    tn = 1536
    return pl.pallas_call(
        _ada_kernel,
        grid=(n_layers, d6 // tn),
        in_specs=[
            pl.BlockSpec((PAD_ROWS, d), lambda l, n: (0, 0)),
            pl.BlockSpec((1, d, tn), lambda l, n: (l, 0, n)),
            pl.BlockSpec((1, 1, tn), lambda l, n: (l, 0, n)),
        ],
        out_specs=pl.BlockSpec((1, PAD_ROWS, tn), lambda l, n: (l, 0, n)),
        out_shape=jax.ShapeDtypeStruct((n_layers, PAD_ROWS, d6), F32),
        compiler_params=_params("parallel", "parallel"),
        name="ada",
    )(cc, w_ada, b_ada.reshape(n_layers, 1, d6))


def _rope(x, c, sa, sb):
    return x * c + pltpu.roll(x, LANES - 16, 1) * sa + pltpu.roll(x, 16, 1) * sb


def _proj_kernel(xc_ref, xp_ref, xn_ref, mod_ref, n1g_ref, wall_ref, qag_ref, kvag_ref,
                 wq_ref, wk_ref, wv_ref, ctab_ref, sa_ref, sb_ref, convw_ref, shift_ref, alog_ref, dtb_ref,
                 qt_ref, k_ref, vt_ref, gq_ref, gk_ref, gv_ref, gb_ref, zs_ref,
                 hext_ref, pqkv_ref, cbuf_ref, *, tm, n_ctx_tiles, n_tiles, d_model):
    t = pl.program_id(1)
    halo = BF16_SUBLANES
    n_qkv = 3 * GDN_WIDTH
    n1g = n1g_ref[...]
    seg_first = jnp.logical_or(t == 0, t == n_ctx_tiles)
    seg_last = jnp.logical_or(t == n_ctx_tiles - 1, t == n_tiles - 1)
    o_kva = MLA_Q_RANK
    o_kr = o_kva + MLA_KV_RANK
    o_z = o_kr + LANES
    o_ab = o_z + GDN_WIDTH
    ctab = ctab_ref[...]
    sa = sa_ref[...]
    sb = sb_ref[...]
    cw = convw_ref[...]
    rows = tm + 2 * halo
    lane = lax.broadcasted_iota(jnp.int32, (tm, LANES), 1)
    pms = {}
    convs = {}

    def st_front(bb):
        mod = mod_ref[bb]
        sh1 = mod[:, 0:d_model]
        gs = n1g * (1.0 + mod[:, d_model:2 * d_model])

        def modulate(x):
            y = x * lax.rsqrt(jnp.mean(x * x, axis=-1, keepdims=True) + NORM_EPS)
            return (y * gs + sh1).astype(BF16)

        hext_ref[bb, 0:halo, :] = modulate(xp_ref[bb])
        hext_ref[bb, halo:halo + tm, :] = modulate(xc_ref[bb])
        hext_ref[bb, halo + tm:, :] = modulate(xn_ref[bb])

    def st_front_mm(bb):
        pqkv_ref[bb] = _dot(hext_ref[bb], wall_ref[:, 0:n_qkv])

        @pl.when(seg_first)
        def _():
            pqkv_ref[bb, 0:halo, :] = jnp.zeros((halo, n_qkv), F32)

        @pl.when(seg_last)
        def _():
            pqkv_ref[bb, halo + tm:, :] = jnp.zeros((halo, n_qkv), F32)

        pms[bb] = _dot(hext_ref[bb, halo:halo + tm, :], wall_ref[:, n_qkv:])

    def st_mla(bb):
        pm = pms[bb]
        qa = pm[:, 0:o_kva]
        kva = pm[:, o_kva:o_kr]
        kr = pm[:, o_kr:o_z]
        qn = (qa * lax.rsqrt(jnp.mean(qa * qa, axis=-1, keepdims=True) + NORM_EPS) * qag_ref[...]).astype(BF16)
        q = _dot(qn, wq_ref[...])
        scale = MLA_QK_DIM ** -0.5 * LOG2_E
        for h in range(MLA_HEADS):
            qh = q[:, h * LANES:(h + 1) * LANES]
            qt_ref[bb, h] = (_rope(qh, ctab, sa, sb) * scale).T.astype(BF16)

        kvn = (kva * lax.rsqrt(jnp.mean(kva * kva, axis=-1, keepdims=True) + NORM_EPS)
               * kvag_ref[...]).astype(BF16)
        kk = _dot(kvn, wk_ref[...])
        kpe = _rope(kr, ctab, sa, sb)
        for h in range(MLA_HEADS):
            k_ref[bb, :, h * LANES:(h + 1) * LANES] = (kk[:, h * LANES:(h + 1) * LANES] + kpe).astype(BF16)
        vv = _dot(kvn, wv_ref[...])
        for h in range(MLA_HEADS):
            vh = jnp.where(lane == MLA_V_DIM, 1.0, vv[:, h * LANES:(h + 1) * LANES])
            vt_ref[bb, h] = vh.T.astype(BF16)

    def st_cbuf(bb):
        pm = pms[bb]
        zs_ref[bb] = _silu(pm[:, o_z:o_ab]).astype(BF16)
        for j in range(GDN_CONV):
            cbuf_ref[bb, j * rows:(j + 1) * rows, :] = (pqkv_ref[bb] * cw[j:j + 1, :]).astype(BF16)

    def st_conv_mm(bb):
        convs[bb] = _dot(shift_ref[...], cbuf_ref[bb])

    def st_post(bb):
        conv = convs[bb]
        ab = pms[bb][:, o_ab:o_ab + LANES]
        for cb in range(3 * GDN_HEADS):
            cs = slice(cb * LANES, (cb + 1) * LANES)
            act = _silu(conv[:, cs])
            grp, hh = divmod(cb, GDN_HEADS)
            hs = slice(hh * LANES, (hh + 1) * LANES)
            if grp < 2:
                act = act * lax.rsqrt(jnp.sum(act * act, axis=-1, keepdims=True) + NORM_EPS)
                (gq_ref if grp == 0 else gk_ref)[bb, :, hs] = act.astype(BF16)
            else:
                gv_ref[bb, :, hs] = act.astype(BF16)

        x = ab + dtb_ref[...]
        softplus = jnp.maximum(x, 0.0) + jnp.log(1.0 + jnp.exp(-jnp.abs(x)))
        gval = -jnp.exp(alog_ref[...]) * softplus
        gb_ref[bb] = jnp.where(lane < N_DIR * GDN_HEADS, gval, _sigmoid(ab))

    rows_b = range(PROJ_BATCH)
    for bb in rows_b:
        st_front(bb)
    st_front_mm(0)
    for bb in rows_b:
        if bb + 1 < PROJ_BATCH:
            st_front_mm(bb + 1)
        st_mla(bb)
        st_cbuf(bb)
        st_conv_mm(bb)
    for bb in rows_b:
        st_post(bb)


def _conv_shift_matrix():
    rows = TOKEN_TILE + 2 * BF16_SUBLANES
    r = jnp.arange(TOKEN_TILE)[:, None]
    col = jnp.arange(GDN_CONV * rows)[None, :]
    j = col // rows
    return (col % rows == BF16_SUBLANES + r + j - GDN_CONV // 2).astype(BF16)


def _proj_call(xs, mod, layer, wts, tabs, shift, n_ctx_tiles):
    b, ta, d = xs.shape
    tm = TOKEN_TILE
    n_tiles = ta // tm
    halo = BF16_SUBLANES
    hb = tm // halo
    n_halo_blocks = ta // halo
    d6 = 6 * d

    pb = PROJ_BATCH
    assert b % pb == 0
    tok = lambda w: pl.BlockSpec((pb, tm, w), lambda bi, t: (bi, t, 0))
    in_specs = [
        tok(d),
        pl.BlockSpec((pb, halo, d), lambda bi, t: (bi, jnp.maximum(t * hb - 1, 0), 0)),
        pl.BlockSpec((pb, halo, d), lambda bi, t: (bi, jnp.minimum((t + 1) * hb, n_halo_blocks - 1), 0)),
        pl.BlockSpec((None, pb, None, 1, d6),
                     lambda bi, t: (layer, bi, jnp.where(t < n_ctx_tiles, 1, 0), 0, 0)),
        _layer_spec(wts["norm1_g"], layer),
        _layer_spec(wts["w_all"], layer),
        _layer_spec(wts["q_a_g"], layer),
        _layer_spec(wts["kv_a_g"], layer),
        _layer_spec(wts["w_q"], layer),
        _layer_spec(wts["w_k"], layer),
        _layer_spec(wts["w_v"], layer),
        pl.BlockSpec((tm, LANES), lambda bi, t: (t, 0)),
        pl.BlockSpec((tm, LANES), lambda bi, t: (t, 0)),
        pl.BlockSpec((tm, LANES), lambda bi, t: (t, 0)),
        _layer_spec(wts["conv_w"], layer),
        _const_spec(shift.shape),
        _layer_spec(wts["a_log"], layer),
        _layer_spec(wts["dt_bias"], layer),
    ]
    head_t = pl.BlockSpec((pb, MLA_HEADS, LANES, tm), lambda bi, t: (bi, 0, 0, t))
    out_specs = [
        head_t,
        tok(MLA_HEADS * LANES),
        head_t,
        tok(GDN_WIDTH), tok(GDN_WIDTH), tok(GDN_WIDTH),
        tok(LANES),
        tok(GDN_WIDTH),
    ]
    out_shape = [
        jax.ShapeDtypeStruct((b, MLA_HEADS, LANES, ta), BF16),
        jax.ShapeDtypeStruct((b, ta, MLA_HEADS * LANES), BF16),
        jax.ShapeDtypeStruct((b, MLA_HEADS, LANES, ta), BF16),
        jax.ShapeDtypeStruct((b, ta, GDN_WIDTH), BF16),
        jax.ShapeDtypeStruct((b, ta, GDN_WIDTH), BF16),
        jax.ShapeDtypeStruct((b, ta, GDN_WIDTH), BF16),
        jax.ShapeDtypeStruct((b, ta, LANES), F32),
        jax.ShapeDtypeStruct((b, ta, GDN_WIDTH), BF16),
    ]
    kern = functools.partial(_proj_kernel, tm=tm, n_ctx_tiles=n_ctx_tiles, n_tiles=n_tiles, d_model=d)
    return pl.pallas_call(
        kern,
        grid=(b // pb, n_tiles),
        in_specs=in_specs,
        out_specs=out_specs,
        out_shape=out_shape,
        scratch_shapes=[pltpu.VMEM((pb, tm + 2 * halo, d), BF16),
                        pltpu.VMEM((pb, tm + 2 * halo, 3 * GDN_WIDTH), F32),
                        pltpu.VMEM((pb, GDN_CONV * (tm + 2 * halo), 3 * GDN_WIDTH), BF16)],
        compiler_params=_params("parallel", "parallel"),
        name="proj",
    )(xs, xs, xs, mod, wts["norm1_g"], wts["w_all"], wts["q_a_g"], wts["kv_a_g"],
      wts["w_q"], wts["w_k"], wts["w_v"], tabs[0], tabs[1], tabs[2], wts["conv_w"], shift,
      wts["a_log"], wts["dt_bias"])


def _attn_kernel(qt_ref, k_ref, vt_ref, o_ref, s_ref, *, variants, tq):
    qi = pl.program_id(1)
    kt = ATTN_KEY_TILE

    def run(sub_nks):
        units = [(sub, h, nk // kt) for sub, nk in enumerate(sub_nks) for h in range(MLA_HEADS)]

        def score_tile(u, t, slot):
            sub, h, _ = units[u]
            s = _dot(k_ref[t * kt:(t + 1) * kt, h * LANES:(h + 1) * LANES],
                     qt_ref[h, :, sub * tq:(sub + 1) * tq])
            s_ref[slot, t * kt:(t + 1) * kt, :] = s
            return jnp.max(s, axis=0, keepdims=True)

        def pv_tile(u, t, slot, m):
            h = units[u][1]
            p = jnp.exp2(s_ref[slot, t * kt:(t + 1) * kt, :] - m).astype(BF16)
            return _dot(vt_ref[h, :, t * kt:(t + 1) * kt], p)

        outs = []
        m = functools.reduce(jnp.maximum, [score_tile(0, t, 0) for t in range(units[0][2])])
        for u, (sub, h, nt) in enumerate(units):
            slot = u % 2
            nt_next = units[u + 1][2] if u + 1 < len(units) else 0
            acc = None
            m_parts = []
            for t in range(max(nt, nt_next)):
                if t < nt_next:
                    m_parts.append(score_tile(u + 1, t, 1 - slot))
                if t < nt:
                    o = pv_tile(u, t, slot, m)
                    acc = o if acc is None else acc + o
            outs.append(acc[0:MLA_V_DIM] / acc[MLA_V_DIM:MLA_V_DIM + 1])
            if m_parts:
                m = functools.reduce(jnp.maximum, m_parts)
            if h == MLA_HEADS - 1:
                o_ref[sub * tq:(sub + 1) * tq, :] = jnp.concatenate(outs, axis=0).T.astype(BF16)
                outs = []

    for sub_nks, steps in variants.items():
        @pl.when(functools.reduce(jnp.logical_or, [qi == i for i in steps]))
        def _(sub_nks=sub_nks):
            run(sub_nks)


def _attn_call(qt, k, vt, tc):
    b, ta, _ = k.shape
    tq = TOKEN_TILE
    n_ctx_q = tc // tq
    nq = ta // tq
    n_sub = ATTN_SUBTILES
    n_steps = pl.cdiv(nq, n_sub)
    variants = {}
    for i in range(n_steps):
        key = tuple(tc if g < n_ctx_q else ta for g in range(i * n_sub, min((i + 1) * n_sub, nq)))
        variants.setdefault(key, []).append(i)
    kern = functools.partial(_attn_kernel, variants=variants, tq=tq)
    return pl.pallas_call(
        kern,
        grid=(b, n_steps),
        in_specs=[
            pl.BlockSpec((None, MLA_HEADS, LANES, n_sub * tq), lambda bi, qi: (bi, 0, 0, qi)),
            pl.BlockSpec((None, ta, MLA_HEADS * LANES), lambda bi, qi: (bi, 0, 0)),
            pl.BlockSpec((None, MLA_HEADS, LANES, ta), lambda bi, qi: (bi, 0, 0, 0)),
        ],
        out_specs=pl.BlockSpec((None, n_sub * tq, MLA_HEADS * MLA_V_DIM), lambda bi, qi: (bi, qi, 0)),
        out_shape=jax.ShapeDtypeStruct((b, nq * tq, MLA_HEADS * MLA_V_DIM), BF16),
        scratch_shapes=[pltpu.VMEM((2, ta, tq), F32)],
        compiler_params=_params("parallel", "arbitrary"),
        name="attn",
    )(qt, k, vt)


def _gdn_kernel(qf_ref, kf_ref, vf_ref, gf_ref, qb_ref, kb_ref, vb_ref, gbk_ref, cm_ref,
                of_ref, ob_ref, s_ref):
    j = pl.program_id(1)
    blk = GDN_BLOCK
    c = GDN_CHUNK
    n_chunks = blk // c

    @pl.when(j == 0)
    def _():
        s_ref[...] = jnp.zeros(s_ref.shape, F32)

    row = lax.broadcasted_iota(jnp.int32, (blk, blk), 0)
    col = lax.broadcasted_iota(jnp.int32, (blk, blk), 1)
    same = (row // c) == (col // c)
    eye = jnp.where(row == col, 1.0, 0.0)
    pair_masks = []
    size = 1
    while size < c:
        pair_masks.append(jnp.logical_and((row // (2 * size)) == (col // (2 * size)),
                                          (row // size) != (col // size)))
        size *= 2
    scale = GDN_HEAD_DIM ** -0.5

    dir_refs = ((qf_ref, kf_ref, vf_ref, gf_ref, of_ref), (qb_ref, kb_ref, vb_ref, gbk_ref, ob_ref))

    def st_setup(wave):
        bb = wave["bb"]
        chains = wave["chains"] = []
        for d in range(N_DIR):
            q_ref, k_ref, v_ref, g_ref, o_ref = (r.at[bb] for r in dir_refs[d])
            incl = jnp.logical_and(same, (row >= col) if d == 0 else (row <= col))
            strict = jnp.logical_and(same, (row > col) if d == 0 else (row < col))
            gb = g_ref[...]
            g1 = gb.astype(BF16)
            r1 = gb - g1.astype(F32)
            g2 = r1.astype(BF16)
            g3 = (r1 - g2.astype(F32)).astype(BF16)
            cmd = cm_ref[d]
            gc = _dot(cmd, g1) + _dot(cmd, g2) + _dot(cmd, g3)
            last = c - 1 if d == 0 else 0
            gtot = jnp.concatenate(
                [jnp.broadcast_to(gc[ci * c + last:ci * c + last + 1, :], (c, LANES)) for ci in range(n_chunks)],
                axis=0)
            gct = gc.T
            for h in range(GDN_HEADS):
                ln = d * GDN_HEADS + h
                hs = slice(h * LANES, (h + 1) * LANES)
                gcol = gc[:, ln:ln + 1]
                gl = gtot[:, ln:ln + 1]
                beta = gb[:, N_DIR * GDN_HEADS + ln:N_DIR * GDN_HEADS + ln + 1]
                decay = jnp.exp(jnp.where(incl, gcol - gct[ln:ln + 1, :], NEG_BIG))
                k16 = k_ref[:, hs]
                q16 = q_ref[:, hs]
                k = k16.astype(F32)
                kbeta = k * beta
                egc = jnp.exp(gcol)
                chains.append(dict(
                    d=d, hs=hs, o_ref=o_ref, idx=bb * N_DIR * GDN_HEADS + ln, strict=strict, decay=decay,
                    kb16=k16, kbeta16=kbeta.astype(BF16), q16=q16,
                    rhs16=jnp.concatenate([v_ref[:, hs].astype(F32) * beta, kbeta * egc], axis=1).astype(BF16),
                    qd=q16.astype(F32) * (egc * scale), kt=k * jnp.exp(gl - gcol), glast=jnp.exp(gl)))

    def st_gram(wave):
        for ch in wave["chains"]:
            gram = _dot_nt(ch["kbeta16"], ch["kb16"])
            lower = jnp.where(ch["strict"], gram * ch["decay"], 0.0)
            ch["lower16"] = lower.astype(BF16)
            ch["t16"] = (eye - jnp.where(pair_masks[0], lower, 0.0)).astype(BF16)

    def st_merge_a(pm16):
        def run(wave):
            for ch in wave["chains"]:
                ch["p16"] = _dot(ch["lower16"] * pm16, ch["t16"]).astype(BF16)
        return run

    def st_merge_b(wave):
        for ch in wave["chains"]:
            ch["t16"] = ch["t16"] - _dot(ch["t16"], ch["p16"]).astype(BF16)

    def half_rows(x, s, d):
        x3 = x.reshape(blk // (2 * s), 2 * s, x.shape[1])
        return (x3[:, s:, :] if d == 0 else x3[:, :s, :]).reshape(blk // 2, x.shape[1])

    def st_merge_half_a(pm16, s):
        pm_half = [half_rows(pm16, s, d) for d in range(N_DIR)]

        def run(wave):
            for ch in wave["chains"]:
                e_half = half_rows(ch["lower16"], s, ch["d"]) * pm_half[ch["d"]]
                ch["p_half16"] = _dot(e_half, ch["t16"]).astype(BF16)
        return run

    def st_merge_half_b(s):
        def run(wave):
            n_pairs = blk // (2 * s)
            for ch in wave["chains"]:
                t3 = ch["t16"].reshape(n_pairs, 2 * s, blk)
                p3 = ch["p_half16"].reshape(n_pairs, s, blk)
                zero = jnp.zeros_like(p3)
                if ch["d"] == 0:
                    p_full = jnp.concatenate([zero, p3], axis=1).reshape(blk, blk)
                    r = _dot(t3[:, s:, :].reshape(blk // 2, blk), p_full).astype(BF16)
                    parts = [t3[:, :s, :], t3[:, s:, :] - r.reshape(n_pairs, s, blk)]
                else:
                    p_full = jnp.concatenate([p3, zero], axis=1).reshape(blk, blk)
                    r = _dot(t3[:, :s, :].reshape(blk // 2, blk), p_full).astype(BF16)
                    parts = [t3[:, :s, :] - r.reshape(n_pairs, s, blk), t3[:, s:, :]]
                ch["t16"] = jnp.concatenate(parts, axis=1).reshape(blk, blk)
        return run

    def st_apply(wave):
        for ch in wave["chains"]:
            x = _dot(ch["t16"], ch["rhs16"])
            ch["u"] = x[:, 0:LANES]
            ch["w"] = x[:, LANES:2 * LANES]
            ch["qk"] = _dot_nt(ch["q16"], ch["kb16"]) * (ch["decay"] * scale)

    def st_prep(wave):
        for ch in wave["chains"]:
            ch["s"] = s_ref[ch["idx"]]
            ch["wq16"] = [jnp.concatenate([ch["w"][ci * c:(ci + 1) * c], ch["qd"][ci * c:(ci + 1) * c]],
                                          axis=0).astype(BF16) for ci in range(n_chunks)]
            ch["ktt16"] = [ch["kt"][ci * c:(ci + 1) * c].T.astype(BF16) for ci in range(n_chunks)]
            ch["qk16"] = [ch["qk"][ci * c:(ci + 1) * c, ci * c:(ci + 1) * c].astype(BF16)
                          for ci in range(n_chunks)]

    def st_step_a(step):
        def run(wave):
            for ch in wave["chains"]:
                ci = ch["ci"] = step if ch["d"] == 0 else n_chunks - 1 - step
                ch["ws"] = _dot(ch["wq16"][ci], ch["s"].astype(BF16))
        return run

    def st_step_b(wave):
        for ch in wave["chains"]:
            ci, ws = ch["ci"], ch["ws"]
            r = slice(ci * c, (ci + 1) * c)
            vn16 = (ch["u"][r] - ws[0:c]).astype(BF16)
            o = ws[c:2 * c] + _dot(ch["qk16"][ci], vn16)
            glast = ch["glast"][ci * c:ci * c + 1, :]
            ch["s"] = ch["s"] * glast + _dot(ch["ktt16"][ci], vn16)
            ch["o_ref"][r, ch["hs"]] = o.astype(BF16)

    def st_store(wave):
        for ch in wave["chains"]:
            s_ref[ch["idx"]] = ch["s"]

    stages = [st_setup, st_gram]
    for lvl, pm in enumerate(pair_masks[1:], start=1):
        pm16 = jnp.where(pm, 1.0, 0.0).astype(BF16)
        s = 2 ** lvl
        if s >= BF16_SUBLANES:
            stages += [st_merge_half_a(pm16, s), st_merge_half_b(s)]
        else:
            stages += [st_merge_a(pm16), st_merge_b]
    stages += [st_apply, st_prep]
    for step in range(n_chunks):
        stages += [st_step_a(step), st_step_b]
    stages.append(st_store)

    waves = [dict(bb=bb) for bb in range(GDN_BATCH)]
    for tick in range(len(stages) + GDN_WAVE_LAG * (GDN_BATCH - 1)):
        for wi, wave in enumerate(waves):
            si = tick - GDN_WAVE_LAG * wi
            if 0 <= si < len(stages):
                stages[si](wave)


def _gdn_call(gq, gk, gv, gb, cm, n_ctx_blocks):
    b, ta, w = gq.shape
    blk = GDN_BLOCK
    nb = ta // blk

    def bwd_block(j):
        return jnp.where(j < n_ctx_blocks, n_ctx_blocks - 1 - j, nb - 1 - (j - n_ctx_blocks))

    gbt = GDN_BATCH
    assert b % gbt == 0
    f_spec = lambda wd: pl.BlockSpec((gbt, blk, wd), lambda bi, j: (bi, j, 0))
    b_spec = lambda wd: pl.BlockSpec((gbt, blk, wd), lambda bi, j: (bi, bwd_block(j), 0))
    return pl.pallas_call(
        _gdn_kernel,
        grid=(b // gbt, nb),
        in_specs=[f_spec(w), f_spec(w), f_spec(w), f_spec(LANES),
                  b_spec(w), b_spec(w), b_spec(w), b_spec(LANES),
                  _const_spec(cm.shape)],
        out_specs=[f_spec(w), b_spec(w)],
        out_shape=[jax.ShapeDtypeStruct((b, ta, w), BF16), jax.ShapeDtypeStruct((b, ta, w), BF16)],
        scratch_shapes=[pltpu.VMEM((gbt * N_DIR * GDN_HEADS, GDN_HEAD_DIM, GDN_HEAD_DIM), F32)],
        compiler_params=_params("parallel", "arbitrary"),
        name="gdn",
    )(gq, gk, gv, gb, gq, gk, gv, gb, cm)


def _out_kernel(x_ref, attn_ref, of_ref, ob_ref, zs_ref, mod_ref, gng_ref, wout_ref, n2g_ref,
                w1_ref, w2_ref, fng_ref, o_ref, *, d_model, final):
    gng = gng_ref[...]
    st = [dict() for _ in range(OUT_BATCH)]

    def st_mix(bb):
        o = of_ref[bb].astype(F32) + ob_ref[bb].astype(F32)
        zs = zs_ref[bb].astype(F32)
        parts = [attn_ref[bb]]
        for h in range(GDN_HEADS):
            hs = slice(h * LANES, (h + 1) * LANES)
            oh = o[:, hs]
            y = oh * lax.rsqrt(jnp.mean(oh * oh, axis=-1, keepdims=True) + NORM_EPS) * gng
            parts.append((y * zs[:, hs]).astype(BF16))
        st[bb]["mix"] = jnp.concatenate(parts, axis=1)

    def st_proj(bb):
        mod = mod_ref[bb]
        g1 = mod[:, 2 * d_model:3 * d_model]
        sh2 = mod[:, 3 * d_model:4 * d_model]
        sc2 = mod[:, 4 * d_model:5 * d_model]
        x1 = x_ref[bb] + g1 * _dot(st[bb]["mix"], wout_ref[...])
        y2 = x1 * lax.rsqrt(jnp.mean(x1 * x1, axis=-1, keepdims=True) + NORM_EPS)
        st[bb]["x1"] = x1
        st[bb]["h2"] = ((y2 * n2g_ref[...]) * (1.0 + sc2) + sh2).astype(BF16)

    def st_ff1(bb):
        ff = jnp.maximum(_dot(st[bb]["h2"], w1_ref[...]), 0.0)
        st[bb]["ff"] = (ff * ff).astype(BF16)

    def st_ff2(bb):
        g2 = mod_ref[bb][:, 5 * d_model:6 * d_model]
        x2 = st[bb]["x1"] + g2 * _dot(st[bb]["ff"], w2_ref[...])
        if final:
            x2 = x2 * lax.rsqrt(jnp.mean(x2 * x2, axis=-1, keepdims=True) + NORM_EPS) * fng_ref[...]
        o_ref[bb] = x2

    for stage in (st_mix, st_proj, st_ff1, st_ff2):
        for bb in range(OUT_BATCH):
            stage(bb)


def _out_call(xs, attn, o_f, o_b, z, mod, layer, wts, fng, n_ctx_tiles, final):
    b, ta, d = xs.shape
    tm = TOKEN_TILE
    d6 = 6 * d
    first = n_ctx_tiles if final else 0
    n_tiles = ta // tm - first
    ob = OUT_BATCH
    assert b % ob == 0
    tok = lambda w: pl.BlockSpec((ob, tm, w), lambda bi, t: (bi, t + first, 0))
    kern = functools.partial(_out_kernel, d_model=d, final=final)
    out_rows = n_tiles * tm
    return pl.pallas_call(
        kern,
        grid=(b // ob, n_tiles),
        in_specs=[
            tok(d),
            tok(MLA_HEADS * MLA_V_DIM),
            tok(GDN_WIDTH), tok(GDN_WIDTH), tok(GDN_WIDTH),
            pl.BlockSpec((None, ob, None, 1, d6),
                         lambda bi, t: (layer, bi, jnp.where(t + first < n_ctx_tiles, 1, 0), 0, 0)),
            _layer_spec(wts["gdn_norm_g"], layer),
            _layer_spec(wts["w_out"], layer),
            _layer_spec(wts["norm2_g"], layer),
            _layer_spec(wts["w_ff1"], layer),
            _layer_spec(wts["w_ff2"], layer),
            _const_spec((1, d)),
        ],
        out_specs=pl.BlockSpec((ob, tm, d), lambda bi, t: (bi, t, 0)),
        out_shape=jax.ShapeDtypeStruct((b, out_rows, d), F32),
        compiler_params=_params("parallel", "parallel"),
        name="out_final" if final else "out",
    )(xs, attn, o_f, o_b, z, mod, wts["gdn_norm_g"], wts["w_out"], wts["norm2_g"],
      wts["w_ff1"], wts["w_ff2"], fng)


def _rope_tables(t_lat, t_ctx):
    rows = t_lat // GRID_W
    row = jnp.broadcast_to(jnp.arange(rows)[:, None], (rows, GRID_W)).reshape(-1).astype(F32)
    col = jnp.broadcast_to(jnp.arange(GRID_W)[None, :], (rows, GRID_W)).reshape(-1).astype(F32)
    axis_pairs = MLA_ROPE_DIM // 4
    inv_freq = ROPE_THETA ** (-jnp.arange(axis_pairs, dtype=F32) / axis_pairs)
    ang = jnp.concatenate([row[:, None] * inv_freq, col[:, None] * inv_freq], axis=-1)
    cos, sin = jnp.cos(ang), jnp.sin(ang)
    half = MLA_ROPE_DIM // 2
    lo, mid, hi = MLA_NOPE_DIM, MLA_NOPE_DIM + half, MLA_NOPE_DIM + 2 * half
    ta = t_ctx + t_lat
    ctab = jnp.ones((ta, LANES), F32).at[t_ctx:, lo:mid].set(cos).at[t_ctx:, mid:hi].set(cos)
    sa = jnp.zeros((ta, LANES), F32).at[t_ctx:, lo:mid].set(-sin)
    sb = jnp.zeros((ta, LANES), F32).at[t_ctx:, mid:hi].set(sin)
    return ctab, sa, sb


def _chunk_sum_matrices():
    i = jnp.arange(GDN_BLOCK)
    same = (i[:, None] // GDN_CHUNK) == (i[None, :] // GDN_CHUNK)
    lower = same & (i[:, None] >= i[None, :])
    upper = same & (i[:, None] <= i[None, :])
    return jnp.stack([lower, upper], axis=0).astype(BF16)


def _pad_heads(w, used):
    n_layers, k, _ = w.shape
    w = w.reshape(n_layers, k, MLA_HEADS, used)
    return jnp.pad(w, ((0, 0), (0, 0), (0, 0), (0, LANES - used))).reshape(n_layers, k, MLA_HEADS * LANES)


def _stacked_weights(norm1_g, w_in, q_a_g, w_q_b, kv_a_g, w_kv_b, conv_w, a_log, dt_bias, gdn_norm_g,
                     w_out, norm2_g, w_ff1, w_ff2):
    n_layers, d, _ = w_in.shape
    o = 0
    cols = []
    for sz in (MLA_Q_RANK, MLA_KV_RANK, MLA_ROPE_DIM, 3 * GDN_WIDTH, GDN_WIDTH,
               N_DIR * GDN_HEADS, N_DIR * GDN_HEADS):
        cols.append(w_in[:, :, o:o + sz])
        o += sz
    w_qa, w_kva, w_kr, w_qkv, w_z, w_a, w_b = cols
    w_kr_pad = jnp.zeros((n_layers, d, LANES), F32).at[:, :, MLA_NOPE_DIM:MLA_QK_DIM].set(w_kr)
    nab = N_DIR * GDN_HEADS
    w_ab_pad = jnp.zeros((n_layers, d, LANES), F32).at[:, :, 0:nab].set(w_a).at[:, :, nab:2 * nab].set(w_b)
    kv = w_kv_b.reshape(n_layers, MLA_KV_RANK, MLA_HEADS, MLA_NOPE_DIM + MLA_V_DIM)
    w_k = _pad_heads(kv[..., :MLA_NOPE_DIM].reshape(n_layers, MLA_KV_RANK, -1), MLA_NOPE_DIM)
    w_v = _pad_heads(kv[..., MLA_NOPE_DIM:].reshape(n_layers, MLA_KV_RANK, -1), MLA_V_DIM)
    row = lambda v: jnp.zeros((n_layers, 1, LANES), F32).at[:, 0, 0:nab].set(v.reshape(n_layers, -1))
    return dict(
        norm1_g=norm1_g[:, None, :],
        w_all=jnp.concatenate([w_qkv, w_qa, w_kva, w_kr_pad, w_z, w_ab_pad], axis=-1).astype(BF16),
        q_a_g=q_a_g[:, None, :],
        kv_a_g=kv_a_g[:, None, :],
        w_q=_pad_heads(w_q_b, MLA_QK_DIM).astype(BF16),
        w_k=w_k.astype(BF16),
        w_v=w_v.astype(BF16),
        conv_w=jnp.pad(conv_w, ((0, 0), (0, 8 - GDN_CONV), (0, 0))),
        a_log=row(a_log),
        dt_bias=row(dt_bias),
        gdn_norm_g=gdn_norm_g[:, None, :],
        w_out=w_out.astype(BF16),
        norm2_g=norm2_g[:, None, :],
        w_ff1=w_ff1.astype(BF16),
        w_ff2=w_ff2.astype(BF16),
    )


def kernel(x, c, ctx, c_ctx, w_ada, b_ada, norm1_g, w_in, q_a_g, w_q_b, kv_a_g, w_kv_b, conv_w, a_log,
           dt_bias, gdn_norm_g, w_out, norm2_g, w_ff1, w_ff2, final_norm_g):
    b, t_lat, d = x.shape
    t_ctx = ctx.shape[1]
    depth = w_ada.shape[0]
    assert t_ctx % TOKEN_TILE == 0 and t_lat % TOKEN_TILE == 0 and t_lat % GRID_W == 0
    assert b + 1 <= PAD_ROWS
    n_ctx_tiles = t_ctx // TOKEN_TILE

    xs = jnp.concatenate([ctx, x], axis=1)
    cc = jnp.zeros((PAD_ROWS, d), F32).at[0:b].set(c).at[b].set(c_ctx)
    mods = _ada_call(cc, w_ada, b_ada)
    mod = jnp.stack([mods[:, 0:b], jnp.broadcast_to(mods[:, b:b + 1], (depth, b, 6 * d))], axis=2)
    mod = mod[:, :, :, None, :]
    tabs = _rope_tables(t_lat, t_ctx)
    cm = _chunk_sum_matrices()
    shift = _conv_shift_matrix()
    fng = final_norm_g[None, :]

    wts = _stacked_weights(norm1_g, w_in, q_a_g, w_q_b, kv_a_g, w_kv_b, conv_w, a_log, dt_bias,
                           gdn_norm_g, w_out, norm2_g, w_ff1, w_ff2)
    for i in range(depth):
        last = i == depth - 1
        qt, k, vt, gq, gk, gv, gb, z = _proj_call(xs, mod, i, wts, tabs, shift, n_ctx_tiles)
        attn = _attn_call(qt, k, vt, t_ctx)
        o_f, o_b = _gdn_call(gq, gk, gv, gb, cm, t_ctx // GDN_BLOCK)
        xs = _out_call(xs, attn, o_f, o_b, z, mod, i, wts, fng, n_ctx_tiles, last)
    return xs
```

```python
import functools

import jax
import jax.numpy as jnp
from jax import lax
from jax.experimental import pallas as pl
from jax.experimental.pallas import tpu as pltpu

F32 = jnp.float32
BF16 = jnp.bfloat16

GRID_W = 64
MLA_HEADS = 8
MLA_NOPE_DIM = 64
MLA_ROPE_DIM = 32
MLA_V_DIM = 64
MLA_Q_RANK = 256
MLA_KV_RANK = 128
MLA_QK_DIM = MLA_NOPE_DIM + MLA_ROPE_DIM
GDN_HEADS = 4
GDN_HEAD_DIM = 128
GDN_WIDTH = GDN_HEADS * GDN_HEAD_DIM
GDN_CONV = 5
GDN_CHUNK = 64
N_DIR = 2
ROPE_THETA = 10000.0
NORM_EPS = 1e-6
LOG2_E = 1.4426950408889634

LANES = 128
BF16_SUBLANES = 16
VMEM_LIMIT_BYTES = 56 * 1024 * 1024

TOKEN_TILE = 256
GDN_BLOCK = 256
GDN_BATCH = 2
PROJ_BATCH = 2
OUT_BATCH = 2
ATTN_KEY_TILE = 256
ATTN_SUBTILES = 2
NEG_BIG = -1e30
PAD_ROWS = 16


def _sigmoid(x):
    return 1.0 / (1.0 + jnp.exp(-x))


def _silu(x):
    return x * _sigmoid(x)


def _dot(a, b):
    return jnp.dot(a, b, preferred_element_type=F32)


def _dot_nt(a, b):
    return lax.dot_general(a, b, (((1,), (1,)), ((), ())), preferred_element_type=F32)


def _dot_tn(a, b):
    return lax.dot_general(a, b, (((0,), (0,)), ((), ())), preferred_element_type=F32)


def _const_spec(shape):
    nd = len(shape)
    return pl.BlockSpec(shape, lambda *_: (0,) * nd, pipeline_mode=pl.Buffered(1))


def _layer_spec(stacked, layer):
    shape = stacked.shape[1:]
    return pl.BlockSpec((None,) + shape, lambda *_: (layer,) + (0,) * len(shape), pipeline_mode=pl.Buffered(1))


def _params(*sem):
    return pltpu.CompilerParams(dimension_semantics=sem, vmem_limit_bytes=VMEM_LIMIT_BYTES)


def _ada_kernel(c_ref, w_ref, b_ref, o_ref):
    s = _silu(c_ref[...]).astype(BF16)
    o_ref[0] = _dot(s, w_ref[0].astype(BF16)) + b_ref[0]


def _ada_call(cc, w_ada, b_ada):
    n_layers, d, d6 = w_ada.shape
    tn = 1536
    return pl.pallas_call(
        _ada_kernel,
        grid=(n_layers, d6 // tn),
        in_specs=[
            pl.BlockSpec((PAD_ROWS, d), lambda l, n: (0, 0)),
            pl.BlockSpec((1, d, tn), lambda l, n: (l, 0, n)),
            pl.BlockSpec((1, 1, tn), lambda l, n: (l, 0, n)),
        ],
        out_specs=pl.BlockSpec((1, PAD_ROWS, tn), lambda l, n: (l, 0, n)),
        out_shape=jax.ShapeDtypeStruct((n_layers, PAD_ROWS, d6), F32),
        compiler_params=_params("parallel", "parallel"),
        name="ada",
    )(cc, w_ada, b_ada.reshape(n_layers, 1, d6))


def _rope(x, c, sa, sb):
    return x * c + pltpu.roll(x, LANES - 16, 1) * sa + pltpu.roll(x, 16, 1) * sb


def _proj_kernel(xc_ref, xp_ref, xn_ref, mod_ref, n1g_ref, wall_ref, qag_ref, kvag_ref,
                 wq_ref, wk_ref, wv_ref, ctab_ref, sa_ref, sb_ref, convw_ref, shift_ref, alog_ref, dtb_ref,
                 qt_ref, k_ref, vt_ref, gq_ref, gk_ref, gv_ref, gb_ref, zs_ref,
                 hext_ref, pqkv_ref, cbuf_ref, *, tm, n_ctx_tiles, n_tiles, d_model):
    t = pl.program_id(1)
    halo = BF16_SUBLANES
    n_qkv = 3 * GDN_WIDTH
    n1g = n1g_ref[...]
    seg_first = jnp.logical_or(t == 0, t == n_ctx_tiles)
    seg_last = jnp.logical_or(t == n_ctx_tiles - 1, t == n_tiles - 1)
    o_kva = MLA_Q_RANK
    o_kr = o_kva + MLA_KV_RANK
    o_z = o_kr + LANES
    o_ab = o_z + GDN_WIDTH
    ctab = ctab_ref[...]
    sa = sa_ref[...]
    sb = sb_ref[...]
    cw = convw_ref[...]
    rows = tm + 2 * halo
    lane = lax.broadcasted_iota(jnp.int32, (tm, LANES), 1)
    pms = {}
    convs = {}

    def st_front(bb):
        mod = mod_ref[bb]
        sh1 = mod[:, 0:d_model]
        gs = n1g * (1.0 + mod[:, d_model:2 * d_model])

        def modulate(x):
            y = x * lax.rsqrt(jnp.mean(x * x, axis=-1, keepdims=True) + NORM_EPS)
            return (y * gs + sh1).astype(BF16)

        hext_ref[bb, 0:halo, :] = modulate(xp_ref[bb])
        hext_ref[bb, halo:halo + tm, :] = modulate(xc_ref[bb])
        hext_ref[bb, halo + tm:, :] = modulate(xn_ref[bb])

    def st_front_mm(bb):
        pqkv_ref[bb] = _dot(hext_ref[bb], wall_ref[:, 0:n_qkv])

        @pl.when(seg_first)
        def _():
            pqkv_ref[bb, 0:halo, :] = jnp.zeros((halo, n_qkv), F32)

        @pl.when(seg_last)
        def _():
            pqkv_ref[bb, halo + tm:, :] = jnp.zeros((halo, n_qkv), F32)

        pms[bb] = _dot(hext_ref[bb, halo:halo + tm, :], wall_ref[:, n_qkv:])

    def st_mla(bb):
        pm = pms[bb]
        qa = pm[:, 0:o_kva]
        kva = pm[:, o_kva:o_kr]
        kr = pm[:, o_kr:o_z]
        qn = (qa * lax.rsqrt(jnp.mean(qa * qa, axis=-1, keepdims=True) + NORM_EPS) * qag_ref[...]).astype(BF16)
        q = _dot(qn, wq_ref[...])
        scale = MLA_QK_DIM ** -0.5 * LOG2_E
        for h in range(MLA_HEADS):
            qh = q[:, h * LANES:(h + 1) * LANES]
            qt_ref[bb, h] = (_rope(qh, ctab, sa, sb) * scale).T.astype(BF16)

        kvn = (kva * lax.rsqrt(jnp.mean(kva * kva, axis=-1, keepdims=True) + NORM_EPS)
               * kvag_ref[...]).astype(BF16)
        kk = _dot(kvn, wk_ref[...])
        kpe = _rope(kr, ctab, sa, sb)
        for h in range(MLA_HEADS):
            k_ref[bb, :, h * LANES:(h + 1) * LANES] = (kk[:, h * LANES:(h + 1) * LANES] + kpe).astype(BF16)
        vv = _dot(kvn, wv_ref[...])
        for h in range(MLA_HEADS):
            vh = jnp.where(lane == MLA_V_DIM, 1.0, vv[:, h * LANES:(h + 1) * LANES])
            vt_ref[bb, h] = vh.T.astype(BF16)

    def st_cbuf(bb):
        pm = pms[bb]
        zs_ref[bb] = _silu(pm[:, o_z:o_ab]).astype(BF16)
        for j in range(GDN_CONV):
            cbuf_ref[bb, j * rows:(j + 1) * rows, :] = (pqkv_ref[bb] * cw[j:j + 1, :]).astype(BF16)

    def st_conv_mm(bb):
        convs[bb] = _dot(shift_ref[...], cbuf_ref[bb])

    def st_post(bb):
        conv = convs[bb]
        ab = pms[bb][:, o_ab:o_ab + LANES]
        for cb in range(3 * GDN_HEADS):
            cs = slice(cb * LANES, (cb + 1) * LANES)
            act = _silu(conv[:, cs])
            grp, hh = divmod(cb, GDN_HEADS)
            hs = slice(hh * LANES, (hh + 1) * LANES)
            if grp < 2:
                act = act * lax.rsqrt(jnp.sum(act * act, axis=-1, keepdims=True) + NORM_EPS)
                (gq_ref if grp == 0 else gk_ref)[bb, :, hs] = act.astype(BF16)
            else:
                gv_ref[bb, :, hs] = act.astype(BF16)

        x = ab + dtb_ref[...]
        softplus = jnp.maximum(x, 0.0) + jnp.log(1.0 + jnp.exp(-jnp.abs(x)))
        gval = -jnp.exp(alog_ref[...]) * softplus
        gb_ref[bb] = jnp.where(lane < N_DIR * GDN_HEADS, gval, _sigmoid(ab))

    rows_b = range(PROJ_BATCH)
    for bb in rows_b:
        st_front(bb)
    st_front_mm(0)
    for bb in rows_b:
        if bb + 1 < PROJ_BATCH:
            st_front_mm(bb + 1)
        st_mla(bb)
        st_cbuf(bb)
        st_conv_mm(bb)
    for bb in rows_b:
        st_post(bb)


def _conv_shift_matrix():
    rows = TOKEN_TILE + 2 * BF16_SUBLANES
    r = jnp.arange(TOKEN_TILE)[:, None]
    col = jnp.arange(GDN_CONV * rows)[None, :]
    j = col // rows
    return (col % rows == BF16_SUBLANES + r + j - GDN_CONV // 2).astype(BF16)


def _proj_call(xs, mod, layer, wts, tabs, shift, n_ctx_tiles):
    b, ta, d = xs.shape
    tm = TOKEN_TILE
    n_tiles = ta // tm
    halo = BF16_SUBLANES
    hb = tm // halo
    n_halo_blocks = ta // halo
    d6 = 6 * d

    pb = PROJ_BATCH
    assert b % pb == 0
    tok = lambda w: pl.BlockSpec((pb, tm, w), lambda bi, t: (bi, t, 0))
    in_specs = [
        tok(d),
        pl.BlockSpec((pb, halo, d), lambda bi, t: (bi, jnp.maximum(t * hb - 1, 0), 0)),
        pl.BlockSpec((pb, halo, d), lambda bi, t: (bi, jnp.minimum((t + 1) * hb, n_halo_blocks - 1), 0)),
        pl.BlockSpec((None, pb, None, 1, d6),
                     lambda bi, t: (layer, bi, jnp.where(t < n_ctx_tiles, 1, 0), 0, 0)),
        _layer_spec(wts["norm1_g"], layer),
        _layer_spec(wts["w_all"], layer),
        _layer_spec(wts["q_a_g"], layer),
        _layer_spec(wts["kv_a_g"], layer),
        _layer_spec(wts["w_q"], layer),
        _layer_spec(wts["w_k"], layer),
        _layer_spec(wts["w_v"], layer),
        pl.BlockSpec((tm, LANES), lambda bi, t: (t, 0)),
        pl.BlockSpec((tm, LANES), lambda bi, t: (t, 0)),
        pl.BlockSpec((tm, LANES), lambda bi, t: (t, 0)),
        _layer_spec(wts["conv_w"], layer),
        _const_spec(shift.shape),
        _layer_spec(wts["a_log"], layer),
        _layer_spec(wts["dt_bias"], layer),
    ]
    head_t = pl.BlockSpec((pb, MLA_HEADS, LANES, tm), lambda bi, t: (bi, 0, 0, t))
    out_specs = [
        head_t,
        tok(MLA_HEADS * LANES),
        head_t,
        tok(GDN_WIDTH), tok(GDN_WIDTH), tok(GDN_WIDTH),
        tok(LANES),
        tok(GDN_WIDTH),
    ]
    out_shape = [
        jax.ShapeDtypeStruct((b, MLA_HEADS, LANES, ta), BF16),
        jax.ShapeDtypeStruct((b, ta, MLA_HEADS * LANES), BF16),
        jax.ShapeDtypeStruct((b, MLA_HEADS, LANES, ta), BF16),
        jax.ShapeDtypeStruct((b, ta, GDN_WIDTH), BF16),
        jax.ShapeDtypeStruct((b, ta, GDN_WIDTH), BF16),
        jax.ShapeDtypeStruct((b, ta, GDN_WIDTH), BF16),
        jax.ShapeDtypeStruct((b, ta, LANES), F32),
        jax.ShapeDtypeStruct((b, ta, GDN_WIDTH), BF16),
    ]
    kern = functools.partial(_proj_kernel, tm=tm, n_ctx_tiles=n_ctx_tiles, n_tiles=n_tiles, d_model=d)
    return pl.pallas_call(
        kern,
        grid=(b // pb, n_tiles),
        in_specs=in_specs,
        out_specs=out_specs,
        out_shape=out_shape,
        scratch_shapes=[pltpu.VMEM((pb, tm + 2 * halo, d), BF16),
                        pltpu.VMEM((pb, tm + 2 * halo, 3 * GDN_WIDTH), F32),
                        pltpu.VMEM((pb, GDN_CONV * (tm + 2 * halo), 3 * GDN_WIDTH), BF16)],
        compiler_params=_params("parallel", "parallel"),
        name="proj",
    )(xs, xs, xs, mod, wts["norm1_g"], wts["w_all"], wts["q_a_g"], wts["kv_a_g"],
      wts["w_q"], wts["w_k"], wts["w_v"], tabs[0], tabs[1], tabs[2], wts["conv_w"], shift,
      wts["a_log"], wts["dt_bias"])


def _attn_kernel(qt_ref, k_ref, vt_ref, o_ref, s_ref, *, variants, tq):
    qi = pl.program_id(1)
    kt = ATTN_KEY_TILE

    def run(sub_nks):
        units = [(sub, h, nk // kt) for sub, nk in enumerate(sub_nks) for h in range(MLA_HEADS)]

        def score_tile(u, t, slot):
            sub, h, _ = units[u]
            s = _dot(k_ref[t * kt:(t + 1) * kt, h * LANES:(h + 1) * LANES],
                     qt_ref[h, :, sub * tq:(sub + 1) * tq])
            s_ref[slot, t * kt:(t + 1) * kt, :] = s
            return jnp.max(s, axis=0, keepdims=True)

        def pv_tile(u, t, slot, m):
            h = units[u][1]
            p = jnp.exp2(s_ref[slot, t * kt:(t + 1) * kt, :] - m).astype(BF16)
            return _dot(vt_ref[h, :, t * kt:(t + 1) * kt], p)

        outs = []
        m = functools.reduce(jnp.maximum, [score_tile(0, t, 0) for t in range(units[0][2])])
        for u, (sub, h, nt) in enumerate(units):
            slot = u % 2
            nt_next = units[u + 1][2] if u + 1 < len(units) else 0
            acc = None
            m_parts = []
            for t in range(max(nt, nt_next)):
                if t < nt_next:
                    m_parts.append(score_tile(u + 1, t, 1 - slot))
                if t < nt:
                    o = pv_tile(u, t, slot, m)
                    acc = o if acc is None else acc + o
            outs.append(acc[0:MLA_V_DIM] / acc[MLA_V_DIM:MLA_V_DIM + 1])
            if m_parts:
                m = functools.reduce(jnp.maximum, m_parts)
            if h == MLA_HEADS - 1:
                o_ref[sub * tq:(sub + 1) * tq, :] = jnp.concatenate(outs, axis=0).T.astype(BF16)
                outs = []

    for sub_nks, steps in variants.items():
        @pl.when(functools.reduce(jnp.logical_or, [qi == i for i in steps]))
        def _(sub_nks=sub_nks):
            run(sub_nks)


def _attn_call(qt, k, vt, tc):
    b, ta, _ = k.shape
    tq = TOKEN_TILE
    n_ctx_q = tc // tq
    nq = ta // tq
    n_sub = ATTN_SUBTILES
    n_steps = pl.cdiv(nq, n_sub)
    variants = {}
    for i in range(n_steps):
        key = tuple(tc if g < n_ctx_q else ta for g in range(i * n_sub, min((i + 1) * n_sub, nq)))
        variants.setdefault(key, []).append(i)
    kern = functools.partial(_attn_kernel, variants=variants, tq=tq)
    return pl.pallas_call(
        kern,
        grid=(b, n_steps),
        in_specs=[
            pl.BlockSpec((None, MLA_HEADS, LANES, n_sub * tq), lambda bi, qi: (bi, 0, 0, qi)),
            pl.BlockSpec((None, ta, MLA_HEADS * LANES), lambda bi, qi: (bi, 0, 0)),
            pl.BlockSpec((None, MLA_HEADS, LANES, ta), lambda bi, qi: (bi, 0, 0, 0)),
        ],
        out_specs=pl.BlockSpec((None, n_sub * tq, MLA_HEADS * MLA_V_DIM), lambda bi, qi: (bi, qi, 0)),
        out_shape=jax.ShapeDtypeStruct((b, nq * tq, MLA_HEADS * MLA_V_DIM), BF16),
        scratch_shapes=[pltpu.VMEM((2, ta, tq), F32)],
        compiler_params=_params("parallel", "arbitrary"),
        name="attn",
    )(qt, k, vt)


def _gdn_kernel(qf_ref, kf_ref, vf_ref, gf_ref, qb_ref, kb_ref, vb_ref, gbk_ref, cm_ref,
                of_ref, ob_ref, s_ref):
    j = pl.program_id(1)
    blk = GDN_BLOCK
    c = GDN_CHUNK
    n_chunks = blk // c

    @pl.when(j == 0)
    def _():
        s_ref[...] = jnp.zeros(s_ref.shape, F32)

    row = lax.broadcasted_iota(jnp.int32, (blk, blk), 0)
    col = lax.broadcasted_iota(jnp.int32, (blk, blk), 1)
    same = (row // c) == (col // c)
    eye = jnp.where(row == col, 1.0, 0.0)
    pair_masks = []
    size = 1
    while size < c:
        pair_masks.append(jnp.logical_and((row // (2 * size)) == (col // (2 * size)),
                                          (row // size) != (col // size)))
        size *= 2
    scale = GDN_HEAD_DIM ** -0.5

    dir_refs = ((qf_ref, kf_ref, vf_ref, gf_ref, of_ref), (qb_ref, kb_ref, vb_ref, gbk_ref, ob_ref))

    def st_setup(wave):
        bb = wave["bb"]
        chains = wave["chains"] = []
        for d in range(N_DIR):
            q_ref, k_ref, v_ref, g_ref, o_ref = (r.at[bb] for r in dir_refs[d])
            incl = jnp.logical_and(same, (row >= col) if d == 0 else (row <= col))
            strict = jnp.logical_and(same, (row > col) if d == 0 else (row < col))
            gb = g_ref[...]
            g1 = gb.astype(BF16)
            r1 = gb - g1.astype(F32)
            g2 = r1.astype(BF16)
            g3 = (r1 - g2.astype(F32)).astype(BF16)
            cmd = cm_ref[d]
            gc = _dot(cmd, g1) + _dot(cmd, g2) + _dot(cmd, g3)
            last = c - 1 if d == 0 else 0
            gtot = jnp.concatenate(
                [jnp.broadcast_to(gc[ci * c + last:ci * c + last + 1, :], (c, LANES)) for ci in range(n_chunks)],
                axis=0)
            gct = gc.T
            for h in range(GDN_HEADS):
                ln = d * GDN_HEADS + h
                hs = slice(h * LANES, (h + 1) * LANES)
                gcol = gc[:, ln:ln + 1]
                gl = gtot[:, ln:ln + 1]
                beta = gb[:, N_DIR * GDN_HEADS + ln:N_DIR * GDN_HEADS + ln + 1]
                decay = jnp.exp(jnp.where(incl, gcol - gct[ln:ln + 1, :], NEG_BIG))
                k16 = k_ref[:, hs]
                q16 = q_ref[:, hs]
                k = k16.astype(F32)
                kbeta = k * beta
                egc = jnp.exp(gcol)
                chains.append(dict(
                    d=d, hs=hs, o_ref=o_ref, idx=bb * N_DIR * GDN_HEADS + ln, strict=strict, decay=decay,
                    kb16=k16, kbeta16=kbeta.astype(BF16), q16=q16,
                    rhs16=jnp.concatenate([v_ref[:, hs].astype(F32) * beta, kbeta * egc], axis=1).astype(BF16),
                    qd=q16.astype(F32) * (egc * scale), kt=k * jnp.exp(gl - gcol), glast=jnp.exp(gl)))

    def st_gram(wave):
        for ch in wave["chains"]:
            gram = _dot_nt(ch["kbeta16"], ch["kb16"])
            lower = jnp.where(ch["strict"], gram * ch["decay"], 0.0)
            ch["lower16"] = lower.astype(BF16)
            ch["t16"] = (eye - jnp.where(pair_masks[0], lower, 0.0)).astype(BF16)

    def st_merge_a(pm16):
        def run(wave):
            for ch in wave["chains"]:
                ch["p16"] = _dot(ch["lower16"] * pm16, ch["t16"]).astype(BF16)
        return run

    def st_merge_b(wave):
        for ch in wave["chains"]:
            ch["t16"] = ch["t16"] - _dot(ch["t16"], ch["p16"]).astype(BF16)

    def half_rows(x, s, d):
        x3 = x.reshape(blk // (2 * s), 2 * s, x.shape[1])
        return (x3[:, s:, :] if d == 0 else x3[:, :s, :]).reshape(blk // 2, x.shape[1])

    def st_merge_half_a(pm16, s):
        pm_half = [half_rows(pm16, s, d) for d in range(N_DIR)]

        def run(wave):
            for ch in wave["chains"]:
                e_half = half_rows(ch["lower16"], s, ch["d"]) * pm_half[ch["d"]]
                ch["p_half16"] = _dot(e_half, ch["t16"]).astype(BF16)
        return run

    def st_merge_half_b(s):
        def run(wave):
            n_pairs = blk // (2 * s)
            for ch in wave["chains"]:
                t3 = ch["t16"].reshape(n_pairs, 2 * s, blk)
                p3 = ch["p_half16"].reshape(n_pairs, s, blk)
                zero = jnp.zeros_like(p3)
                if ch["d"] == 0:
                    p_full = jnp.concatenate([zero, p3], axis=1).reshape(blk, blk)
                    r = _dot(t3[:, s:, :].reshape(blk // 2, blk), p_full).astype(BF16)
                    parts = [t3[:, :s, :], t3[:, s:, :] - r.reshape(n_pairs, s, blk)]
                else:
                    p_full = jnp.concatenate([p3, zero], axis=1).reshape(blk, blk)
                    r = _dot(t3[:, :s, :].reshape(blk // 2, blk), p_full).astype(BF16)
                    parts = [t3[:, :s, :] - r.reshape(n_pairs, s, blk), t3[:, s:, :]]
                ch["t16"] = jnp.concatenate(parts, axis=1).reshape(blk, blk)
        return run

    def st_apply(wave):
        for ch in wave["chains"]:
            x = _dot(ch["t16"], ch["rhs16"])
            ch["u"] = x[:, 0:LANES]
            ch["w"] = x[:, LANES:2 * LANES]
            ch["qk"] = _dot_nt(ch["q16"], ch["kb16"]) * (ch["decay"] * scale)

    def st_prep(wave):
        for ch in wave["chains"]:
            ch["s"] = s_ref[ch["idx"]]
            ch["wq16"] = [jnp.concatenate([ch["w"][ci * c:(ci + 1) * c], ch["qd"][ci * c:(ci + 1) * c]],
                                          axis=0).astype(BF16) for ci in range(n_chunks)]
            ch["ktt16"] = [ch["kt"][ci * c:(ci + 1) * c].T.astype(BF16) for ci in range(n_chunks)]
            ch["qk16"] = [ch["qk"][ci * c:(ci + 1) * c, ci * c:(ci + 1) * c].astype(BF16)
                          for ci in range(n_chunks)]

    def st_step_a(step):
        def run(wave):
            for ch in wave["chains"]:
                ci = ch["ci"] = step if ch["d"] == 0 else n_chunks - 1 - step
                ch["ws"] = _dot(ch["wq16"][ci], ch["s"].astype(BF16))
        return run

    def st_step_b(wave):
        for ch in wave["chains"]:
            ci, ws = ch["ci"], ch["ws"]
            r = slice(ci * c, (ci + 1) * c)
            vn16 = (ch["u"][r] - ws[0:c]).astype(BF16)
            o = ws[c:2 * c] + _dot(ch["qk16"][ci], vn16)
            glast = ch["glast"][ci * c:ci * c + 1, :]
            ch["s"] = ch["s"] * glast + _dot(ch["ktt16"][ci], vn16)
            ch["o_ref"][r, ch["hs"]] = o.astype(BF16)

    def st_store(wave):
        for ch in wave["chains"]:
            s_ref[ch["idx"]] = ch["s"]

    stages = [st_setup, st_gram]
    for lvl, pm in enumerate(pair_masks[1:], start=1):
        pm16 = jnp.where(pm, 1.0, 0.0).astype(BF16)
        s = 2 ** lvl
        if s >= BF16_SUBLANES:
            stages += [st_merge_half_a(pm16, s), st_merge_half_b(s)]
        else:
            stages += [st_merge_a(pm16), st_merge_b]
    stages += [st_apply, st_prep]
    for step in range(n_chunks):
        stages += [st_step_a(step), st_step_b]
    stages.append(st_store)

    waves = [dict(bb=bb) for bb in range(GDN_BATCH)]
    for stage in stages:
        for wave in waves:
            stage(wave)


def _gdn_call(gq, gk, gv, gb, cm, n_ctx_blocks):
    b, ta, w = gq.shape
    blk = GDN_BLOCK
    nb = ta // blk

    def bwd_block(j):
        return jnp.where(j < n_ctx_blocks, n_ctx_blocks - 1 - j, nb - 1 - (j - n_ctx_blocks))

    gbt = GDN_BATCH
    assert b % gbt == 0
    f_spec = lambda wd: pl.BlockSpec((gbt, blk, wd), lambda bi, j: (bi, j, 0))
    b_spec = lambda wd: pl.BlockSpec((gbt, blk, wd), lambda bi, j: (bi, bwd_block(j), 0))
    return pl.pallas_call(
        _gdn_kernel,
        grid=(b // gbt, nb),
        in_specs=[f_spec(w), f_spec(w), f_spec(w), f_spec(LANES),
                  b_spec(w), b_spec(w), b_spec(w), b_spec(LANES),
                  _const_spec(cm.shape)],
        out_specs=[f_spec(w), b_spec(w)],
        out_shape=[jax.ShapeDtypeStruct((b, ta, w), BF16), jax.ShapeDtypeStruct((b, ta, w), BF16)],
        scratch_shapes=[pltpu.VMEM((gbt * N_DIR * GDN_HEADS, GDN_HEAD_DIM, GDN_HEAD_DIM), F32)],
        compiler_params=_params("parallel", "arbitrary"),
        name="gdn",
    )(gq, gk, gv, gb, gq, gk, gv, gb, cm)


def _out_kernel(x_ref, attn_ref, of_ref, ob_ref, zs_ref, mod_ref, gng_ref, wout_ref, n2g_ref,
                w1_ref, w2_ref, fng_ref, o_ref, *, d_model, final):
    gng = gng_ref[...]
    st = [dict() for _ in range(OUT_BATCH)]

    def st_mix(bb):
        o = of_ref[bb].astype(F32) + ob_ref[bb].astype(F32)
        zs = zs_ref[bb].astype(F32)
        parts = [attn_ref[bb]]
        for h in range(GDN_HEADS):
            hs = slice(h * LANES, (h + 1) * LANES)
            oh = o[:, hs]
            y = oh * lax.rsqrt(jnp.mean(oh * oh, axis=-1, keepdims=True) + NORM_EPS) * gng
            parts.append((y * zs[:, hs]).astype(BF16))
        st[bb]["mix"] = jnp.concatenate(parts, axis=1)

    def st_proj(bb):
        mod = mod_ref[bb]
        g1 = mod[:, 2 * d_model:3 * d_model]
        sh2 = mod[:, 3 * d_model:4 * d_model]
        sc2 = mod[:, 4 * d_model:5 * d_model]
        x1 = x_ref[bb] + g1 * _dot(st[bb]["mix"], wout_ref[...])
        y2 = x1 * lax.rsqrt(jnp.mean(x1 * x1, axis=-1, keepdims=True) + NORM_EPS)
        st[bb]["x1"] = x1
        st[bb]["h2"] = ((y2 * n2g_ref[...]) * (1.0 + sc2) + sh2).astype(BF16)

    def st_ff1(bb):
        ff = jnp.maximum(_dot(st[bb]["h2"], w1_ref[...]), 0.0)
        st[bb]["ff"] = (ff * ff).astype(BF16)

    def st_ff2(bb):
        g2 = mod_ref[bb][:, 5 * d_model:6 * d_model]
        x2 = st[bb]["x1"] + g2 * _dot(st[bb]["ff"], w2_ref[...])
        if final:
            x2 = x2 * lax.rsqrt(jnp.mean(x2 * x2, axis=-1, keepdims=True) + NORM_EPS) * fng_ref[...]
        o_ref[bb] = x2

    for stage in (st_mix, st_proj, st_ff1, st_ff2):
        for bb in range(OUT_BATCH):
            stage(bb)


def _out_call(xs, attn, o_f, o_b, zs, mod, layer, wts, fng, n_ctx_tiles, final):
    b, ta, d = xs.shape
    tm = TOKEN_TILE
    d6 = 6 * d
    first = n_ctx_tiles if final else 0
    n_tiles = ta // tm - first
    ob = OUT_BATCH
    assert b % ob == 0
    tok = lambda w: pl.BlockSpec((ob, tm, w), lambda bi, t: (bi, t + first, 0))
    kern = functools.partial(_out_kernel, d_model=d, final=final)
    out_rows = n_tiles * tm
    return pl.pallas_call(
        kern,
        grid=(b // ob, n_tiles),
        in_specs=[
            tok(d),
            tok(MLA_HEADS * MLA_V_DIM),
            tok(GDN_WIDTH), tok(GDN_WIDTH), tok(GDN_WIDTH),
            pl.BlockSpec((None, ob, None, 1, d6),
                         lambda bi, t: (layer, bi, jnp.where(t + first < n_ctx_tiles, 1, 0), 0, 0)),
            _layer_spec(wts["gdn_norm_g"], layer),
            _layer_spec(wts["w_out"], layer),
            _layer_spec(wts["norm2_g"], layer),
            _layer_spec(wts["w_ff1"], layer),
            _layer_spec(wts["w_ff2"], layer),
            _const_spec((1, d)),
        ],
        out_specs=pl.BlockSpec((ob, tm, d), lambda bi, t: (bi, t, 0)),
        out_shape=jax.ShapeDtypeStruct((b, out_rows, d), F32),
        compiler_params=_params("parallel", "parallel"),
        name="out_final" if final else "out",
    )(xs, attn, o_f, o_b, zs, mod, wts["gdn_norm_g"], wts["w_out"], wts["norm2_g"],
      wts["w_ff1"], wts["w_ff2"], fng)


def _rope_tables(t_lat, t_ctx):
    rows = t_lat // GRID_W
    row = jnp.broadcast_to(jnp.arange(rows)[:, None], (rows, GRID_W)).reshape(-1).astype(F32)
    col = jnp.broadcast_to(jnp.arange(GRID_W)[None, :], (rows, GRID_W)).reshape(-1).astype(F32)
    axis_pairs = MLA_ROPE_DIM // 4
    inv_freq = ROPE_THETA ** (-jnp.arange(axis_pairs, dtype=F32) / axis_pairs)
    ang = jnp.concatenate([row[:, None] * inv_freq, col[:, None] * inv_freq], axis=-1)
    cos, sin = jnp.cos(ang), jnp.sin(ang)
    half = MLA_ROPE_DIM // 2
    lo, mid, hi = MLA_NOPE_DIM, MLA_NOPE_DIM + half, MLA_NOPE_DIM + 2 * half
    ta = t_ctx + t_lat
    ctab = jnp.ones((ta, LANES), F32).at[t_ctx:, lo:mid].set(cos).at[t_ctx:, mid:hi].set(cos)
    sa = jnp.zeros((ta, LANES), F32).at[t_ctx:, lo:mid].set(-sin)
    sb = jnp.zeros((ta, LANES), F32).at[t_ctx:, mid:hi].set(sin)
    return ctab, sa, sb


def _chunk_sum_matrices():
    i = jnp.arange(GDN_BLOCK)
    same = (i[:, None] // GDN_CHUNK) == (i[None, :] // GDN_CHUNK)
    lower = same & (i[:, None] >= i[None, :])
    upper = same & (i[:, None] <= i[None, :])
    return jnp.stack([lower, upper], axis=0).astype(BF16)


def _pad_heads(w, used):
    n_layers, k, _ = w.shape
    w = w.reshape(n_layers, k, MLA_HEADS, used)
    return jnp.pad(w, ((0, 0), (0, 0), (0, 0), (0, LANES - used))).reshape(n_layers, k, MLA_HEADS * LANES)


def _stacked_weights(norm1_g, w_in, q_a_g, w_q_b, kv_a_g, w_kv_b, conv_w, a_log, dt_bias, gdn_norm_g,
                     w_out, norm2_g, w_ff1, w_ff2):
    n_layers, d, _ = w_in.shape
    o = 0
    cols = []
    for sz in (MLA_Q_RANK, MLA_KV_RANK, MLA_ROPE_DIM, 3 * GDN_WIDTH, GDN_WIDTH,
               N_DIR * GDN_HEADS, N_DIR * GDN_HEADS):
        cols.append(w_in[:, :, o:o + sz])
        o += sz
    w_qa, w_kva, w_kr, w_qkv, w_z, w_a, w_b = cols
    w_kr_pad = jnp.zeros((n_layers, d, LANES), F32).at[:, :, MLA_NOPE_DIM:MLA_QK_DIM].set(w_kr)
    nab = N_DIR * GDN_HEADS
    w_ab_pad = jnp.zeros((n_layers, d, LANES), F32).at[:, :, 0:nab].set(w_a).at[:, :, nab:2 * nab].set(w_b)
    kv = w_kv_b.reshape(n_layers, MLA_KV_RANK, MLA_HEADS, MLA_NOPE_DIM + MLA_V_DIM)
    w_k = _pad_heads(kv[..., :MLA_NOPE_DIM].reshape(n_layers, MLA_KV_RANK, -1), MLA_NOPE_DIM)
    w_v = _pad_heads(kv[..., MLA_NOPE_DIM:].reshape(n_layers, MLA_KV_RANK, -1), MLA_V_DIM)
    row = lambda v: jnp.zeros((n_layers, 1, LANES), F32).at[:, 0, 0:nab].set(v.reshape(n_layers, -1))
    return dict(
        norm1_g=norm1_g[:, None, :],
        w_all=jnp.concatenate([w_qkv, w_qa, w_kva, w_kr_pad, w_z, w_ab_pad], axis=-1).astype(BF16),
        q_a_g=q_a_g[:, None, :],
        kv_a_g=kv_a_g[:, None, :],
        w_q=_pad_heads(w_q_b, MLA_QK_DIM).astype(BF16),
        w_k=w_k.astype(BF16),
        w_v=w_v.astype(BF16),
        conv_w=jnp.pad(conv_w, ((0, 0), (0, 8 - GDN_CONV), (0, 0))),
        a_log=row(a_log),
        dt_bias=row(dt_bias),
        gdn_norm_g=gdn_norm_g[:, None, :],
        w_out=w_out.astype(BF16),
        norm2_g=norm2_g[:, None, :],
        w_ff1=w_ff1.astype(BF16),
        w_ff2=w_ff2.astype(BF16),
    )


def kernel(x, c, ctx, c_ctx, w_ada, b_ada, norm1_g, w_in, q_a_g, w_q_b, kv_a_g, w_kv_b, conv_w, a_log,
           dt_bias, gdn_norm_g, w_out, norm2_g, w_ff1, w_ff2, final_norm_g):
    b, t_lat, d = x.shape
    t_ctx = ctx.shape[1]
    depth = w_ada.shape[0]
    assert t_ctx % TOKEN_TILE == 0 and t_lat % TOKEN_TILE == 0 and t_lat % GRID_W == 0
    assert b + 1 <= PAD_ROWS
    n_ctx_tiles = t_ctx // TOKEN_TILE

    xs = jnp.concatenate([ctx, x], axis=1)
    cc = jnp.zeros((PAD_ROWS, d), F32).at[0:b].set(c).at[b].set(c_ctx)
    mods = _ada_call(cc, w_ada, b_ada)
    mod = jnp.stack([mods[:, 0:b], jnp.broadcast_to(mods[:, b:b + 1], (depth, b, 6 * d))], axis=2)
    mod = mod[:, :, :, None, :]
    tabs = _rope_tables(t_lat, t_ctx)
    cm = _chunk_sum_matrices()
    shift = _conv_shift_matrix()
    fng = final_norm_g[None, :]

    wts = _stacked_weights(norm1_g, w_in, q_a_g, w_q_b, kv_a_g, w_kv_b, conv_w, a_log, dt_bias,
                           gdn_norm_g, w_out, norm2_g, w_ff1, w_ff2)
    for i in range(depth):
        last = i == depth - 1
        qt, k, vt, gq, gk, gv, gb, zs = _proj_call(xs, mod, i, wts, tabs, shift, n_ctx_tiles)
        attn = _attn_call(qt, k, vt, t_ctx)
        o_f, o_b = _gdn_call(gq, gk, gv, gb, cm, t_ctx // GDN_BLOCK)
        xs = _out_call(xs, attn, o_f, o_b, zs, mod, i, wts, fng, n_ctx_tiles, last)
    return xs
```

```python
import functools

import jax
import jax.numpy as jnp
from jax import lax
from jax.experimental import pallas as pl
from jax.experimental.pallas import tpu as pltpu

F32 = jnp.float32
BF16 = jnp.bfloat16

GRID_W = 64
MLA_HEADS = 8
MLA_NOPE_DIM = 64
MLA_ROPE_DIM = 32
MLA_V_DIM = 64
MLA_Q_RANK = 256
MLA_KV_RANK = 128
MLA_QK_DIM = MLA_NOPE_DIM + MLA_ROPE_DIM
GDN_HEADS = 4
GDN_HEAD_DIM = 128
GDN_WIDTH = GDN_HEADS * GDN_HEAD_DIM
GDN_CONV = 5
CONV_SHIFTED_TAPS = tuple(j for j in range(GDN_CONV) if j != GDN_CONV // 2)
GDN_CHUNK = 64
N_DIR = 2
ROPE_THETA = 10000.0
NORM_EPS = 1e-6
LOG2_E = 1.4426950408889634

LANES = 128
BF16_SUBLANES = 16
VMEM_LIMIT_BYTES = 56 * 1024 * 1024

TOKEN_TILE = 256
GDN_BLOCK = 256
GDN_BATCH = 2
PROJ_BATCH = 2
OUT_BATCH = 2
ATTN_KEY_TILE = 256
ATTN_SUBTILES = 2
NEG_BIG = -1e30
PAD_ROWS = 16


def _sigmoid(x):
    return 1.0 / (1.0 + jnp.exp(-x))


def _silu(x):
    return x * _sigmoid(x)


def _dot(a, b):
    return jnp.dot(a, b, preferred_element_type=F32)


def _dot_nt(a, b):
    return lax.dot_general(a, b, (((1,), (1,)), ((), ())), preferred_element_type=F32)


def _dot_tn(a, b):
    return lax.dot_general(a, b, (((0,), (0,)), ((), ())), preferred_element_type=F32)


def _const_spec(shape):
    nd = len(shape)
    return pl.BlockSpec(shape, lambda *_: (0,) * nd, pipeline_mode=pl.Buffered(1))


def _layer_spec(stacked, layer):
    shape = stacked.shape[1:]
    return pl.BlockSpec((None,) + shape, lambda *_: (layer,) + (0,) * len(shape), pipeline_mode=pl.Buffered(1))


def _params(*sem):
    return pltpu.CompilerParams(dimension_semantics=sem, vmem_limit_bytes=VMEM_LIMIT_BYTES)


def _ada_kernel(c_ref, w_ref, b_ref, o_ref):
    s = _silu(c_ref[...]).astype(BF16)
    o_ref[0] = _dot(s, w_ref[0].astype(BF16)) + b_ref[0]


def _ada_call(cc, w_ada, b_ada):
    n_layers, d, d6 = w_ada.shape
    tn = 1536
    return pl.pallas_call(
        _ada_kernel,
        grid=(n_layers, d6 // tn),
        in_specs=[
            pl.BlockSpec((PAD_ROWS, d), lambda l, n: (0, 0)),
            pl.BlockSpec((1, d, tn), lambda l, n: (l, 0, n)),
            pl.BlockSpec((1, 1, tn), lambda l, n: (l, 0, n)),
        ],
        out_specs=pl.BlockSpec((1, PAD_ROWS, tn), lambda l, n: (l, 0, n)),
        out_shape=jax.ShapeDtypeStruct((n_layers, PAD_ROWS, d6), F32),
        compiler_params=_params("parallel", "parallel"),
        name="ada",
    )(cc, w_ada, b_ada.reshape(n_layers, 1, d6))


def _rope(x, c, sa, sb):
    return x * c + pltpu.roll(x, LANES - 16, 1) * sa + pltpu.roll(x, 16, 1) * sb


def _proj_kernel(xc_ref, xp_ref, xn_ref, mod_ref, n1g_ref, wall_ref, qag_ref, kvag_ref,
                 wq_ref, wk_ref, wv_ref, ctab_ref, sa_ref, sb_ref, convw_ref, shift_ref, alog_ref, dtb_ref,
                 qt_ref, k_ref, vt_ref, gq_ref, gk_ref, gv_ref, gb_ref, zs_ref,
                 hext_ref, pqkv_ref, cbuf_ref, *, tm, n_ctx_tiles, n_tiles, d_model):
    t = pl.program_id(1)
    halo = BF16_SUBLANES
    n_qkv = 3 * GDN_WIDTH
    n1g = n1g_ref[...]
    seg_first = jnp.logical_or(t == 0, t == n_ctx_tiles)
    seg_last = jnp.logical_or(t == n_ctx_tiles - 1, t == n_tiles - 1)
    o_kva = MLA_Q_RANK
    o_kr = o_kva + MLA_KV_RANK
    o_z = o_kr + LANES
    o_ab = o_z + GDN_WIDTH
    ctab = ctab_ref[...]
    sa = sa_ref[...]
    sb = sb_ref[...]
    cw = convw_ref[...]
    rows = tm + 2 * halo
    lane = lax.broadcasted_iota(jnp.int32, (tm, LANES), 1)
    pms = {}
    convs = {}

    def st_front(bb):
        mod = mod_ref[bb]
        sh1 = mod[:, 0:d_model]
        gs = n1g * (1.0 + mod[:, d_model:2 * d_model])

        def modulate(x):
            y = x * lax.rsqrt(jnp.mean(x * x, axis=-1, keepdims=True) + NORM_EPS)
            return (y * gs + sh1).astype(BF16)

        hext_ref[bb, 0:halo, :] = modulate(xp_ref[bb])
        hext_ref[bb, halo:halo + tm, :] = modulate(xc_ref[bb])
        hext_ref[bb, halo + tm:, :] = modulate(xn_ref[bb])

    def st_front_mm(bb):
        pqkv_ref[bb] = _dot(hext_ref[bb], wall_ref[:, 0:n_qkv])

        @pl.when(seg_first)
        def _():
            pqkv_ref[bb, 0:halo, :] = jnp.zeros((halo, n_qkv), F32)

        @pl.when(seg_last)
        def _():
            pqkv_ref[bb, halo + tm:, :] = jnp.zeros((halo, n_qkv), F32)

        pms[bb] = _dot(hext_ref[bb, halo:halo + tm, :], wall_ref[:, n_qkv:])

    def st_mla(bb):
        pm = pms[bb]
        qa = pm[:, 0:o_kva]
        kva = pm[:, o_kva:o_kr]
        kr = pm[:, o_kr:o_z]
        qn = (qa * lax.rsqrt(jnp.mean(qa * qa, axis=-1, keepdims=True) + NORM_EPS) * qag_ref[...]).astype(BF16)
        q = _dot(qn, wq_ref[...])
        scale = MLA_QK_DIM ** -0.5 * LOG2_E
        for h in range(MLA_HEADS):
            qh = q[:, h * LANES:(h + 1) * LANES]
            qt_ref[bb, h] = (_rope(qh, ctab, sa, sb) * scale).T.astype(BF16)

        kvn = (kva * lax.rsqrt(jnp.mean(kva * kva, axis=-1, keepdims=True) + NORM_EPS)
               * kvag_ref[...]).astype(BF16)
        kk = _dot(kvn, wk_ref[...])
        kpe = _rope(kr, ctab, sa, sb)
        for h in range(MLA_HEADS):
            k_ref[bb, :, h * LANES:(h + 1) * LANES] = (kk[:, h * LANES:(h + 1) * LANES] + kpe).astype(BF16)
        vv = _dot(kvn, wv_ref[...])
        for h in range(MLA_HEADS):
            vh = jnp.where(lane == MLA_V_DIM, 1.0, vv[:, h * LANES:(h + 1) * LANES])
            vt_ref[bb, h] = vh.T.astype(BF16)

    def st_cbuf(bb):
        pm = pms[bb]
        zs_ref[bb] = _silu(pm[:, o_z:o_ab]).astype(BF16)
        for n, j in enumerate(CONV_SHIFTED_TAPS):
            cbuf_ref[bb, n * rows:(n + 1) * rows, :] = (pqkv_ref[bb] * cw[j:j + 1, :]).astype(BF16)

    def st_conv_mm(bb):
        convs[bb] = _dot(shift_ref[...], cbuf_ref[bb])

    def st_post(bb):
        conv = convs[bb]
        ab = pms[bb][:, o_ab:o_ab + LANES]
        mid = GDN_CONV // 2
        for cb in range(3 * GDN_HEADS):
            cs = slice(cb * LANES, (cb + 1) * LANES)
            act = _silu(conv[:, cs] + pqkv_ref[bb, halo:halo + tm, cs] * cw[mid:mid + 1, cs])
            grp, hh = divmod(cb, GDN_HEADS)
            hs = slice(hh * LANES, (hh + 1) * LANES)
            if grp < 2:
                act = act * lax.rsqrt(jnp.sum(act * act, axis=-1, keepdims=True) + NORM_EPS)
                (gq_ref if grp == 0 else gk_ref)[bb, :, hs] = act.astype(BF16)
            else:
                gv_ref[bb, :, hs] = act.astype(BF16)

        x = ab + dtb_ref[...]
        softplus = jnp.maximum(x, 0.0) + jnp.log(1.0 + jnp.exp(-jnp.abs(x)))
        gval = -jnp.exp(alog_ref[...]) * softplus
        gb_ref[bb] = jnp.where(lane < N_DIR * GDN_HEADS, gval, _sigmoid(ab))

    rows_b = range(PROJ_BATCH)
    for bb in rows_b:
        st_front(bb)
    st_front_mm(0)
    for bb in rows_b:
        if bb + 1 < PROJ_BATCH:
            st_front_mm(bb + 1)
        st_mla(bb)
        st_cbuf(bb)
        st_conv_mm(bb)
    for bb in rows_b:
        st_post(bb)


def _conv_shift_matrix():
    rows = TOKEN_TILE + 2 * BF16_SUBLANES
    r = jnp.arange(TOKEN_TILE)[:, None]
    col = jnp.arange(len(CONV_SHIFTED_TAPS) * rows)[None, :]
    j = jnp.asarray(CONV_SHIFTED_TAPS)[col // rows]
    return (col % rows == BF16_SUBLANES + r + j - GDN_CONV // 2).astype(BF16)


def _proj_call(xs, mod, layer, wts, tabs, shift, n_ctx_tiles):
    b, ta, d = xs.shape
    tm = TOKEN_TILE
    n_tiles = ta // tm
    halo = BF16_SUBLANES
    hb = tm // halo
    n_halo_blocks = ta // halo
    d6 = 6 * d

    pb = PROJ_BATCH
    assert b % pb == 0
    tok = lambda w: pl.BlockSpec((pb, tm, w), lambda bi, t: (bi, t, 0))
    in_specs = [
        tok(d),
        pl.BlockSpec((pb, halo, d), lambda bi, t: (bi, jnp.maximum(t * hb - 1, 0), 0)),
        pl.BlockSpec((pb, halo, d), lambda bi, t: (bi, jnp.minimum((t + 1) * hb, n_halo_blocks - 1), 0)),
        pl.BlockSpec((None, pb, None, 1, d6),
                     lambda bi, t: (layer, bi, jnp.where(t < n_ctx_tiles, 1, 0), 0, 0)),
        _layer_spec(wts["norm1_g"], layer),
        _layer_spec(wts["w_all"], layer),
        _layer_spec(wts["q_a_g"], layer),
        _layer_spec(wts["kv_a_g"], layer),
        _layer_spec(wts["w_q"], layer),
        _layer_spec(wts["w_k"], layer),
        _layer_spec(wts["w_v"], layer),
        pl.BlockSpec((tm, LANES), lambda bi, t: (t, 0)),
        pl.BlockSpec((tm, LANES), lambda bi, t: (t, 0)),
        pl.BlockSpec((tm, LANES), lambda bi, t: (t, 0)),
        _layer_spec(wts["conv_w"], layer),
        _const_spec(shift.shape),
        _layer_spec(wts["a_log"], layer),
        _layer_spec(wts["dt_bias"], layer),
    ]
    head_t = pl.BlockSpec((pb, MLA_HEADS, LANES, tm), lambda bi, t: (bi, 0, 0, t))
    out_specs = [
        head_t,
        tok(MLA_HEADS * LANES),
        head_t,
        tok(GDN_WIDTH), tok(GDN_WIDTH), tok(GDN_WIDTH),
        tok(LANES),
        tok(GDN_WIDTH),
    ]
    out_shape = [
        jax.ShapeDtypeStruct((b, MLA_HEADS, LANES, ta), BF16),
        jax.ShapeDtypeStruct((b, ta, MLA_HEADS * LANES), BF16),
        jax.ShapeDtypeStruct((b, MLA_HEADS, LANES, ta), BF16),
        jax.ShapeDtypeStruct((b, ta, GDN_WIDTH), BF16),
        jax.ShapeDtypeStruct((b, ta, GDN_WIDTH), BF16),
        jax.ShapeDtypeStruct((b, ta, GDN_WIDTH), BF16),
        jax.ShapeDtypeStruct((b, ta, LANES), F32),
        jax.ShapeDtypeStruct((b, ta, GDN_WIDTH), BF16),
    ]
    kern = functools.partial(_proj_kernel, tm=tm, n_ctx_tiles=n_ctx_tiles, n_tiles=n_tiles, d_model=d)
    return pl.pallas_call(
        kern,
        grid=(b // pb, n_tiles),
        in_specs=in_specs,
        out_specs=out_specs,
        out_shape=out_shape,
        scratch_shapes=[pltpu.VMEM((pb, tm + 2 * halo, d), BF16),
                        pltpu.VMEM((pb, tm + 2 * halo, 3 * GDN_WIDTH), F32),
                        pltpu.VMEM((pb, len(CONV_SHIFTED_TAPS) * (tm + 2 * halo), 3 * GDN_WIDTH), BF16)],
        compiler_params=_params("parallel", "parallel"),
        name="proj",
    )(xs, xs, xs, mod, wts["norm1_g"], wts["w_all"], wts["q_a_g"], wts["kv_a_g"],
      wts["w_q"], wts["w_k"], wts["w_v"], tabs[0], tabs[1], tabs[2], wts["conv_w"], shift,
      wts["a_log"], wts["dt_bias"])


def _attn_kernel(qt_ref, k_ref, vt_ref, o_ref, s_ref, *, variants, tq):
    qi = pl.program_id(1)
    kt = ATTN_KEY_TILE

    def run(sub_nks):
        units = [(sub, h, nk // kt) for sub, nk in enumerate(sub_nks) for h in range(MLA_HEADS)]

        def score_tile(u, t, slot):
            sub, h, _ = units[u]
            s = _dot(k_ref[t * kt:(t + 1) * kt, h * LANES:(h + 1) * LANES],
                     qt_ref[h, :, sub * tq:(sub + 1) * tq])
            s_ref[slot, t * kt:(t + 1) * kt, :] = s
            return jnp.max(s, axis=0, keepdims=True)

        def pv_tile(u, t, slot, m):
            h = units[u][1]
            p = jnp.exp2(s_ref[slot, t * kt:(t + 1) * kt, :] - m).astype(BF16)
            return _dot(vt_ref[h, :, t * kt:(t + 1) * kt], p)

        outs = []
        m = functools.reduce(jnp.maximum, [score_tile(0, t, 0) for t in range(units[0][2])])
        for u, (sub, h, nt) in enumerate(units):
            slot = u % 2
            nt_next = units[u + 1][2] if u + 1 < len(units) else 0
            acc = None
            m_parts = []
            for t in range(max(nt, nt_next)):
                if t < nt_next:
                    m_parts.append(score_tile(u + 1, t, 1 - slot))
                if t < nt:
                    o = pv_tile(u, t, slot, m)
                    acc = o if acc is None else acc + o
            outs.append(acc[0:MLA_V_DIM] / acc[MLA_V_DIM:MLA_V_DIM + 1])
            if m_parts:
                m = functools.reduce(jnp.maximum, m_parts)
            if h == MLA_HEADS - 1:
                o_ref[sub * tq:(sub + 1) * tq, :] = jnp.concatenate(outs, axis=0).T.astype(BF16)
                outs = []

    for sub_nks, steps in variants.items():
        @pl.when(functools.reduce(jnp.logical_or, [qi == i for i in steps]))
        def _(sub_nks=sub_nks):
            run(sub_nks)


def _attn_call(qt, k, vt, tc):
    b, ta, _ = k.shape
    tq = TOKEN_TILE
    n_ctx_q = tc // tq
    nq = ta // tq
    n_sub = ATTN_SUBTILES
    n_steps = pl.cdiv(nq, n_sub)
    variants = {}
    for i in range(n_steps):
        key = tuple(tc if g < n_ctx_q else ta for g in range(i * n_sub, min((i + 1) * n_sub, nq)))
        variants.setdefault(key, []).append(i)
    kern = functools.partial(_attn_kernel, variants=variants, tq=tq)
    return pl.pallas_call(
        kern,
        grid=(b, n_steps),
        in_specs=[
            pl.BlockSpec((None, MLA_HEADS, LANES, n_sub * tq), lambda bi, qi: (bi, 0, 0, qi)),
            pl.BlockSpec((None, ta, MLA_HEADS * LANES), lambda bi, qi: (bi, 0, 0)),
            pl.BlockSpec((None, MLA_HEADS, LANES, ta), lambda bi, qi: (bi, 0, 0, 0)),
        ],
        out_specs=pl.BlockSpec((None, n_sub * tq, MLA_HEADS * MLA_V_DIM), lambda bi, qi: (bi, qi, 0)),
        out_shape=jax.ShapeDtypeStruct((b, nq * tq, MLA_HEADS * MLA_V_DIM), BF16),
        scratch_shapes=[pltpu.VMEM((2, ta, tq), F32)],
        compiler_params=_params("parallel", "arbitrary"),
        name="attn",
    )(qt, k, vt)


def _gdn_kernel(qf_ref, kf_ref, vf_ref, gf_ref, qb_ref, kb_ref, vb_ref, gbk_ref, cm_ref,
                of_ref, ob_ref, s_ref):
    j = pl.program_id(1)
    blk = GDN_BLOCK
    c = GDN_CHUNK
    n_chunks = blk // c

    @pl.when(j == 0)
    def _():
        s_ref[...] = jnp.zeros(s_ref.shape, F32)

    row = lax.broadcasted_iota(jnp.int32, (blk, blk), 0)
    col = lax.broadcasted_iota(jnp.int32, (blk, blk), 1)
    same = (row // c) == (col // c)
    eye = jnp.where(row == col, 1.0, 0.0)
    pair_masks = []
    size = 1
    while size < c:
        pair_masks.append(jnp.logical_and((row // (2 * size)) == (col // (2 * size)),
                                          (row // size) != (col // size)))
        size *= 2
    scale = GDN_HEAD_DIM ** -0.5

    dir_refs = ((qf_ref, kf_ref, vf_ref, gf_ref, of_ref), (qb_ref, kb_ref, vb_ref, gbk_ref, ob_ref))

    def st_setup(wave):
        bb = wave["bb"]
        chains = wave["chains"] = []
        for d in range(N_DIR):
            q_ref, k_ref, v_ref, g_ref, o_ref = (r.at[bb] for r in dir_refs[d])
            incl = jnp.logical_and(same, (row >= col) if d == 0 else (row <= col))
            strict = jnp.logical_and(same, (row > col) if d == 0 else (row < col))
            gb = g_ref[...]
            g1 = gb.astype(BF16)
            r1 = gb - g1.astype(F32)
            g2 = r1.astype(BF16)
            g3 = (r1 - g2.astype(F32)).astype(BF16)
            cmd = cm_ref[d]
            gc = _dot(cmd, g1) + _dot(cmd, g2) + _dot(cmd, g3)
            last = c - 1 if d == 0 else 0
            gtot = jnp.concatenate(
                [jnp.broadcast_to(gc[ci * c + last:ci * c + last + 1, :], (c, LANES)) for ci in range(n_chunks)],
                axis=0)
            gct = gc.T
            for h in range(GDN_HEADS):
                ln = d * GDN_HEADS + h
                hs = slice(h * LANES, (h + 1) * LANES)
                gcol = gc[:, ln:ln + 1]
                gl = gtot[:, ln:ln + 1]
                beta = gb[:, N_DIR * GDN_HEADS + ln:N_DIR * GDN_HEADS + ln + 1]
                decay = jnp.exp(jnp.where(incl, gcol - gct[ln:ln + 1, :], NEG_BIG))
                k16 = k_ref[:, hs]
                q16 = q_ref[:, hs]
                k = k16.astype(F32)
                kbeta = k * beta
                egc = jnp.exp(gcol)
                chains.append(dict(
                    d=d, hs=hs, o_ref=o_ref, idx=bb * N_DIR * GDN_HEADS + ln, strict=strict, decay=decay,
                    kb16=k16, kbeta16=kbeta.astype(BF16), q16=q16,
                    rhs16=jnp.concatenate([v_ref[:, hs].astype(F32) * beta, kbeta * egc], axis=1).astype(BF16),
                    qd=q16.astype(F32) * (egc * scale), kt=k * jnp.exp(gl - gcol), glast=jnp.exp(gl)))

    def st_gram(wave):
        for ch in wave["chains"]:
            gram = _dot_nt(ch["kbeta16"], ch["kb16"])
            lower = jnp.where(ch["strict"], gram * ch["decay"], 0.0)
            ch["lower16"] = lower.astype(BF16)
            ch["t16"] = (eye - jnp.where(pair_masks[0], lower, 0.0)).astype(BF16)

    def st_merge_a(pm16):
        def run(wave):
            for ch in wave["chains"]:
                ch["p16"] = _dot(ch["lower16"] * pm16, ch["t16"]).astype(BF16)
        return run

    def st_merge_b(wave):
        for ch in wave["chains"]:
            ch["t16"] = ch["t16"] - _dot(ch["t16"], ch["p16"]).astype(BF16)

    def half_rows(x, s, d):
        x3 = x.reshape(blk // (2 * s), 2 * s, x.shape[1])
        return (x3[:, s:, :] if d == 0 else x3[:, :s, :]).reshape(blk // 2, x.shape[1])

    def st_merge_half_a(pm16, s):
        pm_half = [half_rows(pm16, s, d) for d in range(N_DIR)]

        def run(wave):
            for ch in wave["chains"]:
                e_half = half_rows(ch["lower16"], s, ch["d"]) * pm_half[ch["d"]]
                ch["p_half16"] = _dot(e_half, ch["t16"]).astype(BF16)
        return run

    def st_merge_half_b(s):
        def run(wave):
            n_pairs = blk // (2 * s)
            for ch in wave["chains"]:
                t3 = ch["t16"].reshape(n_pairs, 2 * s, blk)
                p3 = ch["p_half16"].reshape(n_pairs, s, blk)
                zero = jnp.zeros_like(p3)
                if ch["d"] == 0:
                    p_full = jnp.concatenate([zero, p3], axis=1).reshape(blk, blk)
                    r = _dot(t3[:, s:, :].reshape(blk // 2, blk), p_full).astype(BF16)
                    parts = [t3[:, :s, :], t3[:, s:, :] - r.reshape(n_pairs, s, blk)]
                else:
                    p_full = jnp.concatenate([p3, zero], axis=1).reshape(blk, blk)
                    r = _dot(t3[:, :s, :].reshape(blk // 2, blk), p_full).astype(BF16)
                    parts = [t3[:, :s, :] - r.reshape(n_pairs, s, blk), t3[:, s:, :]]
                ch["t16"] = jnp.concatenate(parts, axis=1).reshape(blk, blk)
        return run

    def st_apply(wave):
        for ch in wave["chains"]:
            x = _dot(ch["t16"], ch["rhs16"])
            ch["u"] = x[:, 0:LANES]
            ch["w"] = x[:, LANES:2 * LANES]
            ch["qk"] = _dot_nt(ch["q16"], ch["kb16"]) * (ch["decay"] * scale)

    def st_prep(wave):
        for ch in wave["chains"]:
            ch["s"] = s_ref[ch["idx"]]
            ch["wq16"] = [jnp.concatenate([ch["w"][ci * c:(ci + 1) * c], ch["qd"][ci * c:(ci + 1) * c]],
                                          axis=0).astype(BF16) for ci in range(n_chunks)]
            ch["ktt16"] = [ch["kt"][ci * c:(ci + 1) * c].T.astype(BF16) for ci in range(n_chunks)]
            ch["qk16"] = [ch["qk"][ci * c:(ci + 1) * c, ci * c:(ci + 1) * c].astype(BF16)
                          for ci in range(n_chunks)]

    def st_step_a(step):
        def run(wave):
            for ch in wave["chains"]:
                ci = ch["ci"] = step if ch["d"] == 0 else n_chunks - 1 - step
                ch["ws"] = _dot(ch["wq16"][ci], ch["s"].astype(BF16))
        return run

    def st_step_b(wave):
        for ch in wave["chains"]:
            ci, ws = ch["ci"], ch["ws"]
            r = slice(ci * c, (ci + 1) * c)
            vn16 = (ch["u"][r] - ws[0:c]).astype(BF16)
            o = ws[c:2 * c] + _dot(ch["qk16"][ci], vn16)
            glast = ch["glast"][ci * c:ci * c + 1, :]
            ch["s"] = ch["s"] * glast + _dot(ch["ktt16"][ci], vn16)
            ch["o_ref"][r, ch["hs"]] = o.astype(BF16)

    def st_store(wave):
        for ch in wave["chains"]:
            s_ref[ch["idx"]] = ch["s"]

    stages = [st_setup, st_gram]
    for lvl, pm in enumerate(pair_masks[1:], start=1):
        pm16 = jnp.where(pm, 1.0, 0.0).astype(BF16)
        s = 2 ** lvl
        if s >= BF16_SUBLANES:
            stages += [st_merge_half_a(pm16, s), st_merge_half_b(s)]
        else:
            stages += [st_merge_a(pm16), st_merge_b]
    stages += [st_apply, st_prep]
    for step in range(n_chunks):
        stages += [st_step_a(step), st_step_b]
    stages.append(st_store)

    waves = [dict(bb=bb) for bb in range(GDN_BATCH)]
    for stage in stages:
        for wave in waves:
            stage(wave)


def _gdn_call(gq, gk, gv, gb, cm, n_ctx_blocks):
    b, ta, w = gq.shape
    blk = GDN_BLOCK
    nb = ta // blk

    def bwd_block(j):
        return jnp.where(j < n_ctx_blocks, n_ctx_blocks - 1 - j, nb - 1 - (j - n_ctx_blocks))

    gbt = GDN_BATCH
    assert b % gbt == 0
    f_spec = lambda wd: pl.BlockSpec((gbt, blk, wd), lambda bi, j: (bi, j, 0))
    b_spec = lambda wd: pl.BlockSpec((gbt, blk, wd), lambda bi, j: (bi, bwd_block(j), 0))
    return pl.pallas_call(
        _gdn_kernel,
        grid=(b // gbt, nb),
        in_specs=[f_spec(w), f_spec(w), f_spec(w), f_spec(LANES),
                  b_spec(w), b_spec(w), b_spec(w), b_spec(LANES),
                  _const_spec(cm.shape)],
        out_specs=[f_spec(w), b_spec(w)],
        out_shape=[jax.ShapeDtypeStruct((b, ta, w), BF16), jax.ShapeDtypeStruct((b, ta, w), BF16)],
        scratch_shapes=[pltpu.VMEM((gbt * N_DIR * GDN_HEADS, GDN_HEAD_DIM, GDN_HEAD_DIM), F32)],
        compiler_params=_params("parallel", "arbitrary"),
        name="gdn",
    )(gq, gk, gv, gb, gq, gk, gv, gb, cm)


def _out_kernel(x_ref, attn_ref, of_ref, ob_ref, zs_ref, mod_ref, gng_ref, wout_ref, n2g_ref,
                w1_ref, w2_ref, fng_ref, o_ref, *, d_model, final):
    gng = gng_ref[...]
    st = [dict() for _ in range(OUT_BATCH)]

    def st_mix(bb):
        o = of_ref[bb].astype(F32) + ob_ref[bb].astype(F32)
        zs = zs_ref[bb].astype(F32)
        parts = [attn_ref[bb]]
        for h in range(GDN_HEADS):
            hs = slice(h * LANES, (h + 1) * LANES)
            oh = o[:, hs]
            y = oh * lax.rsqrt(jnp.mean(oh * oh, axis=-1, keepdims=True) + NORM_EPS) * gng
            parts.append((y * zs[:, hs]).astype(BF16))
        st[bb]["mix"] = jnp.concatenate(parts, axis=1)

    def st_proj(bb):
        mod = mod_ref[bb]
        g1 = mod[:, 2 * d_model:3 * d_model]
        sh2 = mod[:, 3 * d_model:4 * d_model]
        sc2 = mod[:, 4 * d_model:5 * d_model]
        x1 = x_ref[bb] + g1 * _dot(st[bb]["mix"], wout_ref[...])
        y2 = x1 * lax.rsqrt(jnp.mean(x1 * x1, axis=-1, keepdims=True) + NORM_EPS)
        st[bb]["x1"] = x1
        st[bb]["h2"] = ((y2 * n2g_ref[...]) * (1.0 + sc2) + sh2).astype(BF16)

    def st_ff1(bb):
        ff = jnp.maximum(_dot(st[bb]["h2"], w1_ref[...]), 0.0)
        st[bb]["ff"] = (ff * ff).astype(BF16)

    def st_ff2(bb):
        g2 = mod_ref[bb][:, 5 * d_model:6 * d_model]
        x2 = st[bb]["x1"] + g2 * _dot(st[bb]["ff"], w2_ref[...])
        if final:
            x2 = x2 * lax.rsqrt(jnp.mean(x2 * x2, axis=-1, keepdims=True) + NORM_EPS) * fng_ref[...]
        o_ref[bb] = x2

    for stage in (st_mix, st_proj, st_ff1, st_ff2):
        for bb in range(OUT_BATCH):
            stage(bb)


def _out_call(xs, attn, o_f, o_b, zs, mod, layer, wts, fng, n_ctx_tiles, final):
    b, ta, d = xs.shape
    tm = TOKEN_TILE
    d6 = 6 * d
    first = n_ctx_tiles if final else 0
    n_tiles = ta // tm - first
    ob = OUT_BATCH
    assert b % ob == 0
    tok = lambda w: pl.BlockSpec((ob, tm, w), lambda bi, t: (bi, t + first, 0))
    kern = functools.partial(_out_kernel, d_model=d, final=final)
    out_rows = n_tiles * tm
    return pl.pallas_call(
        kern,
        grid=(b // ob, n_tiles),
        in_specs=[
            tok(d),
            tok(MLA_HEADS * MLA_V_DIM),
            tok(GDN_WIDTH), tok(GDN_WIDTH), tok(GDN_WIDTH),
            pl.BlockSpec((None, ob, None, 1, d6),
                         lambda bi, t: (layer, bi, jnp.where(t + first < n_ctx_tiles, 1, 0), 0, 0)),
            _layer_spec(wts["gdn_norm_g"], layer),
            _layer_spec(wts["w_out"], layer),
            _layer_spec(wts["norm2_g"], layer),
            _layer_spec(wts["w_ff1"], layer),
            _layer_spec(wts["w_ff2"], layer),
            _const_spec((1, d)),
        ],
        out_specs=pl.BlockSpec((ob, tm, d), lambda bi, t: (bi, t, 0)),
        out_shape=jax.ShapeDtypeStruct((b, out_rows, d), F32),
        compiler_params=_params("parallel", "parallel"),
        name="out_final" if final else "out",
    )(xs, attn, o_f, o_b, zs, mod, wts["gdn_norm_g"], wts["w_out"], wts["norm2_g"],
      wts["w_ff1"], wts["w_ff2"], fng)


def _rope_tables(t_lat, t_ctx):
    rows = t_lat // GRID_W
    row = jnp.broadcast_to(jnp.arange(rows)[:, None], (rows, GRID_W)).reshape(-1).astype(F32)
    col = jnp.broadcast_to(jnp.arange(GRID_W)[None, :], (rows, GRID_W)).reshape(-1).astype(F32)
    axis_pairs = MLA_ROPE_DIM // 4
    inv_freq = ROPE_THETA ** (-jnp.arange(axis_pairs, dtype=F32) / axis_pairs)
    ang = jnp.concatenate([row[:, None] * inv_freq, col[:, None] * inv_freq], axis=-1)
    cos, sin = jnp.cos(ang), jnp.sin(ang)
    half = MLA_ROPE_DIM // 2
    lo, mid, hi = MLA_NOPE_DIM, MLA_NOPE_DIM + half, MLA_NOPE_DIM + 2 * half
    ta = t_ctx + t_lat
    ctab = jnp.ones((ta, LANES), F32).at[t_ctx:, lo:mid].set(cos).at[t_ctx:, mid:hi].set(cos)
    sa = jnp.zeros((ta, LANES), F32).at[t_ctx:, lo:mid].set(-sin)
    sb = jnp.zeros((ta, LANES), F32).at[t_ctx:, mid:hi].set(sin)
    return ctab, sa, sb


def _chunk_sum_matrices():
    i = jnp.arange(GDN_BLOCK)
    same = (i[:, None] // GDN_CHUNK) == (i[None, :] // GDN_CHUNK)
    lower = same & (i[:, None] >= i[None, :])
    upper = same & (i[:, None] <= i[None, :])
    return jnp.stack([lower, upper], axis=0).astype(BF16)


def _pad_heads(w, used):
    n_layers, k, _ = w.shape
    w = w.reshape(n_layers, k, MLA_HEADS, used)
    return jnp.pad(w, ((0, 0), (0, 0), (0, 0), (0, LANES - used))).reshape(n_layers, k, MLA_HEADS * LANES)


def _stacked_weights(norm1_g, w_in, q_a_g, w_q_b, kv_a_g, w_kv_b, conv_w, a_log, dt_bias, gdn_norm_g,
                     w_out, norm2_g, w_ff1, w_ff2):
    n_layers, d, _ = w_in.shape
    o = 0
    cols = []
    for sz in (MLA_Q_RANK, MLA_KV_RANK, MLA_ROPE_DIM, 3 * GDN_WIDTH, GDN_WIDTH,
               N_DIR * GDN_HEADS, N_DIR * GDN_HEADS):
        cols.append(w_in[:, :, o:o + sz])
        o += sz
    w_qa, w_kva, w_kr, w_qkv, w_z, w_a, w_b = cols
    w_kr_pad = jnp.zeros((n_layers, d, LANES), F32).at[:, :, MLA_NOPE_DIM:MLA_QK_DIM].set(w_kr)
    nab = N_DIR * GDN_HEADS
    w_ab_pad = jnp.zeros((n_layers, d, LANES), F32).at[:, :, 0:nab].set(w_a).at[:, :, nab:2 * nab].set(w_b)
    kv = w_kv_b.reshape(n_layers, MLA_KV_RANK, MLA_HEADS, MLA_NOPE_DIM + MLA_V_DIM)
    w_k = _pad_heads(kv[..., :MLA_NOPE_DIM].reshape(n_layers, MLA_KV_RANK, -1), MLA_NOPE_DIM)
    w_v = _pad_heads(kv[..., MLA_NOPE_DIM:].reshape(n_layers, MLA_KV_RANK, -1), MLA_V_DIM)
    row = lambda v: jnp.zeros((n_layers, 1, LANES), F32).at[:, 0, 0:nab].set(v.reshape(n_layers, -1))
    return dict(
        norm1_g=norm1_g[:, None, :],
        w_all=jnp.concatenate([w_qkv, w_qa, w_kva, w_kr_pad, w_z, w_ab_pad], axis=-1).astype(BF16),
        q_a_g=q_a_g[:, None, :],
        kv_a_g=kv_a_g[:, None, :],
        w_q=_pad_heads(w_q_b, MLA_QK_DIM).astype(BF16),
        w_k=w_k.astype(BF16),
        w_v=w_v.astype(BF16),
        conv_w=jnp.pad(conv_w, ((0, 0), (0, 8 - GDN_CONV), (0, 0))),
        a_log=row(a_log),
        dt_bias=row(dt_bias),
        gdn_norm_g=gdn_norm_g[:, None, :],
        w_out=w_out.astype(BF16),
        norm2_g=norm2_g[:, None, :],
        w_ff1=w_ff1.astype(BF16),
        w_ff2=w_ff2.astype(BF16),
    )


def kernel(x, c, ctx, c_ctx, w_ada, b_ada, norm1_g, w_in, q_a_g, w_q_b, kv_a_g, w_kv_b, conv_w, a_log,
           dt_bias, gdn_norm_g, w_out, norm2_g, w_ff1, w_ff2, final_norm_g):
    b, t_lat, d = x.shape
    t_ctx = ctx.shape[1]
    depth = w_ada.shape[0]
    assert t_ctx % TOKEN_TILE == 0 and t_lat % TOKEN_TILE == 0 and t_lat % GRID_W == 0
    assert b + 1 <= PAD_ROWS
    n_ctx_tiles = t_ctx // TOKEN_TILE

    xs = jnp.concatenate([ctx, x], axis=1)
    cc = jnp.zeros((PAD_ROWS, d), F32).at[0:b].set(c).at[b].set(c_ctx)
    mods = _ada_call(cc, w_ada, b_ada)
    mod = jnp.stack([mods[:, 0:b], jnp.broadcast_to(mods[:, b:b + 1], (depth, b, 6 * d))], axis=2)
    mod = mod[:, :, :, None, :]
    tabs = _rope_tables(t_lat, t_ctx)
    cm = _chunk_sum_matrices()
    shift = _conv_shift_matrix()
    fng = final_norm_g[None, :]

    wts = _stacked_weights(norm1_g, w_in, q_a_g, w_q_b, kv_a_g, w_kv_b, conv_w, a_log, dt_bias,
                           gdn_norm_g, w_out, norm2_g, w_ff1, w_ff2)
    for i in range(depth):
        last = i == depth - 1
        qt, k, vt, gq, gk, gv, gb, zs = _proj_call(xs, mod, i, wts, tabs, shift, n_ctx_tiles)
        attn = _attn_call(qt, k, vt, t_ctx)
        o_f, o_b = _gdn_call(gq, gk, gv, gb, cm, t_ctx // GDN_BLOCK)
        xs = _out_call(xs, attn, o_f, o_b, zs, mod, i, wts, fng, n_ctx_tiles, last)
    return xs
```
